```python
import math
import jax, jax.numpy as jnp
from jax import lax
import numpy as np

D_MODEL = 2048
BATCH = 16
SEQ = 2048
DEPTH = 2

N_MIXERS = 2
N_ATTN_LAYERS = (DEPTH + 1) // 2
N_LRU_LAYERS = DEPTH // 2
CHUNK = 64
LEFT_CHUNKS = 8
BAND = (LEFT_CHUNKS + 1) * CHUNK
ATT_HEADS = 16
ATT_HEAD_DIM = 128
D_ATT = ATT_HEADS * ATT_HEAD_DIM
MAX_REL_DIST = 256
N_REL = 2 * MAX_REL_DIST + 1
D_RNN = D_MODEL
LRU_BLOCKS = 16
LRU_BLOCK_W = D_RNN // LRU_BLOCKS
CONV_W = 4
RG_C = 8.0
LN_EPS = 1e-5
NEG_INF = -1e30
DEEPNORM_ALPHA = (2.0 * DEPTH) ** 0.25
DEEPNORM_BETA = (8.0 * DEPTH) ** -0.25

kernel_name = "chunked_attn_rglru_deepnorm_hybrid"


def _layer_norm(x, gain, bias):
    xf = x.astype(jnp.float32)
    mu = jnp.mean(xf, axis=-1, keepdims=True)
    var = jnp.mean(jnp.square(xf - mu), axis=-1, keepdims=True)
    y = (xf - mu) * lax.rsqrt(var + LN_EPS)
    return (y * gain.astype(jnp.float32) + bias.astype(jnp.float32)).astype(x.dtype)


def _rel_index():
    i = np.arange(CHUNK)[:, None]
    j = np.arange(BAND)[None, :]
    dist = i - (j - LEFT_CHUNKS * CHUNK)
    return (np.clip(dist, -MAX_REL_DIST, MAX_REL_DIST) + MAX_REL_DIST).astype(np.int32)


def _chunk_attention_mixer(h, w_in, w_out, rel_table):
    B, S, _ = h.shape
    nc = S // CHUNK
    proj = h @ w_in
    q, k, v, g = jnp.split(proj, 4, axis=-1)
    q = q.reshape(B, S, ATT_HEADS, ATT_HEAD_DIM) * (ATT_HEAD_DIM ** -0.5)
    pad = ((0, 0), (LEFT_CHUNKS * CHUNK, 0), (0, 0), (0, 0))
    kp = jnp.pad(k.reshape(B, S, ATT_HEADS, ATT_HEAD_DIM), pad)
    vp = jnp.pad(v.reshape(B, S, ATT_HEADS, ATT_HEAD_DIM), pad)
    qc = q.reshape(B, nc, CHUNK, ATT_HEADS, ATT_HEAD_DIM).transpose(1, 0, 2, 3, 4)
    bias = rel_table.astype(jnp.float32)[:, jnp.asarray(_rel_index())]
    band_pos = jnp.arange(BAND) - LEFT_CHUNKS * CHUNK

    def one_chunk(args):
        c, q_blk = args
        kb = lax.dynamic_slice_in_dim(kp, c * CHUNK, BAND, axis=1)
        vb = lax.dynamic_slice_in_dim(vp, c * CHUNK, BAND, axis=1)
        s = jnp.einsum('bqhd,bkhd->bhqk', q_blk, kb).astype(jnp.float32) + bias
        valid = (band_pos + c * CHUNK) >= 0
        s = jnp.where(valid[None, None, None, :], s, NEG_INF)
        p = jax.nn.softmax(s, axis=-1).astype(vb.dtype)
        return jnp.einsum('bhqk,bkhd->bqhd', p, vb)

    o = lax.map(one_chunk, (jnp.arange(nc), qc))
    o = o.transpose(1, 0, 2, 3, 4).reshape(B, S, D_ATT)
    return (o * jax.nn.silu(g)) @ w_out


def _lru_combine(left, right):
    a1, b1 = left
    a2, b2 = right
    return a1 * a2, a2 * b1 + b2


def _rglru_mixer(h, w_in, conv_w, conv_b, w_a, b_a, w_x, b_x, lam, w_out):
    B, S, _ = h.shape
    proj = h @ w_in
    u, g = jnp.split(proj, 2, axis=-1)
    up = jnp.pad(u, ((0, 0), (CONV_W - 1, 0), (0, 0)))
    u = conv_b + sum(up[:, t:t + S] * conv_w[t] for t in range(CONV_W))
    ub = u.reshape(B, S, LRU_BLOCKS, LRU_BLOCK_W)
    r = jax.nn.sigmoid(jnp.einsum('bsni,nij->bsnj', ub, w_a) + b_a).reshape(B, S, D_RNN)
    i = jax.nn.sigmoid(jnp.einsum('bsni,nij->bsnj', ub, w_x) + b_x).reshape(B, S, D_RNN)
    log_a = -RG_C * r.astype(jnp.float32) * jax.nn.softplus(-lam.astype(jnp.float32))
    a = jnp.exp(log_a)
    mult = jnp.sqrt(-jnp.expm1(2.0 * log_a))
    b = mult * (i * u).astype(jnp.float32)
    _, hs = lax.associative_scan(_lru_combine, (a, b), axis=1)
    y = hs.astype(h.dtype) * jax.nn.silu(g)
    return y @ w_out


def setup_inputs(seed: int = 0) -> dict:
    key = jax.random.key(seed)
    ks = jax.random.split(key, 16)
    f32 = jnp.float32
    x = jax.random.normal(ks[0], (BATCH, SEQ, D_MODEL), f32)

    attn_w_in = jax.random.normal(ks[1], (N_ATTN_LAYERS, D_MODEL, 4 * D_ATT), f32) * D_MODEL ** -0.5
    col_scale = jnp.concatenate([jnp.ones((2 * D_ATT,), f32),
                                 jnp.full((D_ATT,), DEEPNORM_BETA, f32),
                                 jnp.ones((D_ATT,), f32)])
    attn_w_in = attn_w_in * col_scale
    attn_w_out = jax.random.normal(ks[2], (N_ATTN_LAYERS, D_ATT, D_MODEL), f32) * (D_ATT ** -0.5) * DEEPNORM_BETA
    attn_rel_bias = jax.random.normal(ks[3], (N_ATTN_LAYERS, ATT_HEADS, N_REL), f32) * 0.1

    lru_w_in = jax.random.normal(ks[4], (N_LRU_LAYERS, D_MODEL, 2 * D_RNN), f32) * D_MODEL ** -0.5
    lru_conv_w = jax.random.normal(ks[5], (N_LRU_LAYERS, CONV_W, D_RNN), f32) * CONV_W ** -0.5
    lru_conv_b = jax.random.normal(ks[6], (N_LRU_LAYERS, D_RNN), f32) * 0.02
    lru_wa = jax.random.normal(ks[7], (N_LRU_LAYERS, LRU_BLOCKS, LRU_BLOCK_W, LRU_BLOCK_W), f32) * LRU_BLOCK_W ** -0.5
    lru_ba = jax.random.normal(ks[8], (N_LRU_LAYERS, LRU_BLOCKS, LRU_BLOCK_W), f32) * 0.02
    lru_wx = jax.random.normal(ks[9], (N_LRU_LAYERS, LRU_BLOCKS, LRU_BLOCK_W, LRU_BLOCK_W), f32) * LRU_BLOCK_W ** -0.5
    lru_bx = jax.random.normal(ks[10], (N_LRU_LAYERS, LRU_BLOCKS, LRU_BLOCK_W), f32) * 0.02
    a_c = jax.random.uniform(ks[11], (N_LRU_LAYERS, D_RNN), f32, 0.9, 0.999)
    s = a_c ** (1.0 / RG_C)
    lru_lambda = jnp.log(s) - jnp.log1p(-s)
    lru_w_out = jax.random.normal(ks[12], (N_LRU_LAYERS, D_RNN, D_MODEL), f32) * (D_RNN ** -0.5) * DEEPNORM_BETA

    ln_gain = 1.0 + 0.02 * jax.random.normal(ks[13], (DEPTH, D_MODEL), f32)
    ln_bias = 0.02 * jax.random.normal(ks[14], (DEPTH, D_MODEL), f32)
    return {"x": x, "attn_w_in": attn_w_in, "attn_w_out": attn_w_out, "attn_rel_bias": attn_rel_bias,
            "lru_w_in": lru_w_in, "lru_conv_w": lru_conv_w, "lru_conv_b": lru_conv_b,
            "lru_wa": lru_wa, "lru_ba": lru_ba, "lru_wx": lru_wx, "lru_bx": lru_bx,
            "lru_lambda": lru_lambda, "lru_w_out": lru_w_out,
            "ln_gain": ln_gain, "ln_bias": ln_bias}


def reference(x, attn_w_in, attn_w_out, attn_rel_bias, lru_w_in, lru_conv_w, lru_conv_b,
              lru_wa, lru_ba, lru_wx, lru_bx, lru_lambda, lru_w_out, ln_gain, ln_bias):
    h = x
    for layer in range(DEPTH):
        j = layer // N_MIXERS
        if layer % N_MIXERS == 0:
            y = _chunk_attention_mixer(h, attn_w_in[j], attn_w_out[j], attn_rel_bias[j])
        else:
            y = _rglru_mixer(h, lru_w_in[j], lru_conv_w[j], lru_conv_b[j], lru_wa[j], lru_ba[j],
                             lru_wx[j], lru_bx[j], lru_lambda[j], lru_w_out[j])
        h = _layer_norm(DEEPNORM_ALPHA * h + y, ln_gain[layer], ln_bias[layer])
    return h
```

```python
import functools

import jax
import jax.numpy as jnp
import numpy as np
from jax import lax
from jax.experimental import pallas as pl
from jax.experimental.pallas import tpu as pltpu

DEPTH = 2
CHUNK = 64
LEFT_CHUNKS = 8
LEFT = LEFT_CHUNKS * CHUNK
ATT_HEADS = 16
HEAD_DIM = 128
MAX_REL_DIST = 256
LRU_BLOCKS = 16
LRU_BLOCK_W = 128
CONV_W = 4
RG_C = 8.0
LN_EPS = 1e-5
NEG_INF = -1e30
DEEPNORM_ALPHA = (2.0 * DEPTH) ** 0.25

LANES = 128
SUBLANES = 8
MIB = 1024 * 1024

PROJ_TM = 1024
PROJ_TN = 1024
ATT_TQ = 256
ATT_WIN = LEFT + ATT_TQ
ATT_NCLS = LEFT // ATT_TQ + 1
ATT_HB = 2
OUT_TM = 256
LRU_TM = 256

F32 = jnp.float32
BF16 = jnp.bfloat16


def _proj_kernel(x_ref, w_ref, s_ref, o_ref):
    x = x_ref[...].astype(BF16)
    acc = jnp.dot(x, w_ref[...], preferred_element_type=F32)
    o_ref[...] = (acc * s_ref[...]).astype(o_ref.dtype)


def _proj(x, w, col_scale):
    m, k = x.shape
    n = w.shape[1]
    return pl.pallas_call(
        _proj_kernel,
        grid=(m // PROJ_TM, n // PROJ_TN),
        in_specs=[
            pl.BlockSpec((PROJ_TM, k), lambda i, j: (i, 0)),
            pl.BlockSpec((k, PROJ_TN), lambda i, j: (0, j)),
            pl.BlockSpec((1, PROJ_TN), lambda i, j: (0, j)),
        ],
        out_specs=pl.BlockSpec((PROJ_TM, PROJ_TN), lambda i, j: (i, j)),
        out_shape=jax.ShapeDtypeStruct((m, n), BF16),
        compiler_params=pltpu.CompilerParams(
            dimension_semantics=("parallel", "arbitrary"),
            vmem_limit_bytes=48 * MIB),
        name="attn_in_proj",
    )(x, w, col_scale)


def _attention_kernel(q_ref, k_ref, v_ref, g_ref, bias_ref, o_ref):
    qi = pl.program_id(2)
    ks = pl.multiple_of(jnp.maximum(qi * ATT_TQ - LEFT, 0), ATT_TQ)
    cls = jnp.minimum(qi, ATT_NCLS - 1)
    for hh in range(ATT_HB):
        cols = slice(hh * HEAD_DIM, (hh + 1) * HEAD_DIM)
        q = q_ref[:, cols]
        k = k_ref[pl.ds(ks, ATT_WIN), cols]
        v = v_ref[pl.ds(ks, ATT_WIN), cols]
        s = lax.dot_general(q, k, (((1,), (1,)), ((), ())),
                            preferred_element_type=F32)
        s = s + bias_ref[hh, cls]
        m = jnp.max(s, axis=-1, keepdims=True)
        p = jnp.exp(s - m)
        l = jnp.sum(p, axis=-1, keepdims=True)
        o = jnp.dot(p.astype(BF16), v, preferred_element_type=F32)
        o = o / l
        g = g_ref[:, cols].astype(F32)
        o_ref[:, cols] = (o * (g * jax.nn.sigmoid(g))).astype(o_ref.dtype)


def _attention_bias_tiles(rel_table):
    i = np.arange(ATT_TQ)[:, None]
    j = np.arange(ATT_WIN)[None, :]
    tiles = []
    for c in range(ATT_NCLS):
        off = min(c * ATT_TQ, LEFT)
        dist = i - j + off
        idx = np.clip(dist, -MAX_REL_DIST, MAX_REL_DIST) + MAX_REL_DIST
        dchunk = i // CHUNK - j // CHUNK + off // CHUNK
        allowed = (dchunk >= 0) & (dchunk <= LEFT_CHUNKS)
        b = rel_table.astype(F32)[:, jnp.asarray(idx.astype(np.int32))]
        tiles.append(jnp.where(jnp.asarray(allowed)[None], b, NEG_INF))
    return jnp.stack(tiles, axis=1)


def _attention(qkvg, bias_tiles, batch, seq):
    m = qkvg.shape[0]
    d_att = ATT_HEADS * HEAD_DIM
    hw = ATT_HB * HEAD_DIM
    ncol = d_att // hw
    nq = seq // ATT_TQ
    return pl.pallas_call(
        _attention_kernel,
        grid=(ATT_HEADS // ATT_HB, batch, nq),
        in_specs=[
            pl.BlockSpec((ATT_TQ, hw), lambda h, b, q: (b * nq + q, h)),
            pl.BlockSpec((seq, hw), lambda h, b, q: (b, ncol + h)),
            pl.BlockSpec((seq, hw), lambda h, b, q: (b, 2 * ncol + h)),
            pl.BlockSpec((ATT_TQ, hw), lambda h, b, q: (b * nq + q, 3 * ncol + h)),
            pl.BlockSpec((ATT_HB, ATT_NCLS, ATT_TQ, ATT_WIN),
                         lambda h, b, q: (h, 0, 0, 0)),
        ],
        out_specs=pl.BlockSpec((ATT_TQ, hw), lambda h, b, q: (b * nq + q, h)),
        out_shape=jax.ShapeDtypeStruct((m, d_att), BF16),
        compiler_params=pltpu.CompilerParams(
            dimension_semantics=("parallel", "parallel", "arbitrary"),
            vmem_limit_bytes=48 * MIB),
        name="chunk_attention",
    )(qkvg, qkvg, qkvg, qkvg, bias_tiles)


def _out_ln_kernel(y_ref, w_ref, r_ref, gain_ref, bias_ref, o_ref):
    acc = jnp.dot(y_ref[...], w_ref[...], preferred_element_type=F32)
    z = DEEPNORM_ALPHA * r_ref[...] + acc
    mu = jnp.mean(z, axis=-1, keepdims=True)
    zc = z - mu
    var = jnp.mean(zc * zc, axis=-1, keepdims=True)
    o_ref[...] = zc * lax.rsqrt(var + LN_EPS) * gain_ref[...] + bias_ref[...]


def _out_ln(y, w, resid, gain, bias, name):
    m, k = y.shape
    n = w.shape[1]
    return pl.pallas_call(
        _out_ln_kernel,
        grid=(m // OUT_TM,),
        in_specs=[
            pl.BlockSpec((OUT_TM, k), lambda i: (i, 0)),
            pl.BlockSpec((k, n), lambda i: (0, 0)),
            pl.BlockSpec((OUT_TM, n), lambda i: (i, 0)),
            pl.BlockSpec((1, n), lambda i: (0, 0)),
            pl.BlockSpec((1, n), lambda i: (0, 0)),
        ],
        out_specs=pl.BlockSpec((OUT_TM, n), lambda i: (i, 0)),
        out_shape=jax.ShapeDtypeStruct((m, n), F32),
        compiler_params=pltpu.CompilerParams(
            dimension_semantics=("parallel",),
            vmem_limit_bytes=48 * MIB),
        name=name,
    )(y, w, resid, gain, bias)


def _linear_scan(a, b):
    t = a.shape[0]
    row = lax.broadcasted_iota(jnp.int32, (SUBLANES, a.shape[1]), 0)
    d = 1
    while d < t:
        if d < SUBLANES:
            a_sh = pltpu.roll(a, d, 0)
            b_sh = pltpu.roll(b, d, 0)
            keep = row >= d
            a_sh = jnp.concatenate(
                [jnp.where(keep, a_sh[:SUBLANES], 1.0), a_sh[SUBLANES:]], axis=0)
            b_sh = jnp.concatenate(
                [jnp.where(keep, b_sh[:SUBLANES], 0.0), b_sh[SUBLANES:]], axis=0)
            b = a * b_sh + b
            a = a * a_sh
        else:
            b = jnp.concatenate([b[:d], a[d:] * b[:-d] + b[d:]], axis=0)
            a = jnp.concatenate([a[:d], a[d:] * a[:-d]], axis=0)
        d *= 2
    return a, b


def _lru_kernel(h_ref, win_ref, cw_ref, cb_ref, wax_ref, bax_ref, lam_ref, y_ref,
                ubuf, gbuf, hcarry):
    t = pl.program_id(1)
    tm = h_ref.shape[0]
    d_rnn = y_ref.shape[1]

    @pl.when(t == 0)
    def _():
        ubuf[0:SUBLANES, :] = jnp.zeros((SUBLANES, d_rnn), F32)
        hcarry[...] = jnp.zeros_like(hcarry)

    x = h_ref[...].astype(BF16)
    proj = jnp.dot(x, win_ref[...], preferred_element_type=F32)
    ubuf[SUBLANES:SUBLANES + tm, :] = proj[:, :d_rnn]
    gbuf[...] = proj[:, d_rnn:]

    lam = lam_ref[...]
    softplus_neg_lam = jnp.maximum(-lam, 0.0) + jnp.log1p(jnp.exp(-jnp.abs(lam)))
    log_a_scale = -RG_C * softplus_neg_lam

    for n in range(LRU_BLOCKS):
        cols = slice(n * LRU_BLOCK_W, (n + 1) * LRU_BLOCK_W)
        u = cb_ref[:, cols]
        for tap in range(CONV_W):
            start = SUBLANES - (CONV_W - 1) + tap
            u = u + ubuf[start:start + tm, cols] * cw_ref[tap:tap + 1, cols]
        gates = jnp.dot(u.astype(BF16), wax_ref[n], preferred_element_type=F32)
        gates = jax.nn.sigmoid(gates + bax_ref[n])
        r = gates[:, :LRU_BLOCK_W]
        i = gates[:, LRU_BLOCK_W:]
        log_a = log_a_scale[:, cols] * r
        a = jnp.exp(log_a)
        mult = jnp.sqrt(1.0 - a * a)
        b = mult * (i * u)
        a_cum, h_loc = _linear_scan(a, b)
        h = a_cum * hcarry[:, cols] + h_loc
        hcarry[:, cols] = h[tm - 1:tm, :]
        g = gbuf[:, cols]
        y_ref[:, cols] = (h * (g * jax.nn.sigmoid(g))).astype(y_ref.dtype)

    ubuf[0:SUBLANES, :] = ubuf[tm:tm + SUBLANES, :]


def _lru(h, w_in, conv_w, conv_b, wax, bax, lam, batch, seq):
    m, d = h.shape
    d_rnn = lam.shape[1]
    nt = seq // LRU_TM
    const = lambda b, t: (0, 0)
    return pl.pallas_call(
        _lru_kernel,
        grid=(batch, nt),
        in_specs=[
            pl.BlockSpec((LRU_TM, d), lambda b, t: (b * nt + t, 0)),
            pl.BlockSpec((d, 2 * d_rnn), const, pipeline_mode=pl.Buffered(1)),
            pl.BlockSpec((CONV_W, d_rnn), const),
            pl.BlockSpec((1, d_rnn), const),
            pl.BlockSpec((LRU_BLOCKS, LRU_BLOCK_W, 2 * LRU_BLOCK_W),
                         lambda b, t: (0, 0, 0)),
            pl.BlockSpec((LRU_BLOCKS, 1, 2 * LRU_BLOCK_W), lambda b, t: (0, 0, 0)),
            pl.BlockSpec((1, d_rnn), const),
        ],
        out_specs=pl.BlockSpec((LRU_TM, d_rnn), lambda b, t: (b * nt + t, 0)),
        out_shape=jax.ShapeDtypeStruct((m, d_rnn), BF16),
        scratch_shapes=[
            pltpu.VMEM((LRU_TM + SUBLANES, d_rnn), F32),
            pltpu.VMEM((LRU_TM, d_rnn), F32),
            pltpu.VMEM((1, d_rnn), F32),
        ],
        compiler_params=pltpu.CompilerParams(
            dimension_semantics=("parallel", "arbitrary"),
            vmem_limit_bytes=56 * MIB),
        name="rglru",
    )(h, w_in, conv_w, conv_b, wax, bax, lam)


def kernel(x, attn_w_in, attn_w_out, attn_rel_bias, lru_w_in, lru_conv_w, lru_conv_b,
           lru_wa, lru_ba, lru_wx, lru_bx, lru_lambda, lru_w_out, ln_gain, ln_bias):
    batch, seq, d_model = x.shape
    m = batch * seq
    d_att = ATT_HEADS * HEAD_DIM
    h = x.reshape(m, d_model)

    col_scale = jnp.concatenate([jnp.full((d_att,), HEAD_DIM ** -0.5, F32),
                                 jnp.ones((3 * d_att,), F32)])[None, :]
    qkvg = _proj(h, attn_w_in[0].astype(BF16), col_scale)
    og = _attention(qkvg, _attention_bias_tiles(attn_rel_bias[0]), batch, seq)
    h = _out_ln(og, attn_w_out[0].astype(BF16), h, ln_gain[0][None, :], ln_bias[0][None, :],
                "attn_out_ln")

    wax = jnp.concatenate([lru_wa[0], lru_wx[0]], axis=-1).astype(BF16)
    bax = jnp.concatenate([lru_ba[0], lru_bx[0]], axis=-1)[:, None, :]
    y = _lru(h, lru_w_in[0].astype(BF16), lru_conv_w[0], lru_conv_b[0][None, :], wax, bax,
             lru_lambda[0][None, :], batch, seq)
    h = _out_ln(y, lru_w_out[0].astype(BF16), h, ln_gain[1][None, :], ln_bias[1][None, :],
                "lru_out_ln")
    return h.reshape(batch, seq, d_model)
```

```python
import functools

import jax
import jax.numpy as jnp
import numpy as np
from jax import lax
from jax.experimental import pallas as pl
from jax.experimental.pallas import tpu as pltpu

DEPTH = 2
CHUNK = 64
LEFT_CHUNKS = 8
LEFT = LEFT_CHUNKS * CHUNK
ATT_HEADS = 16
HEAD_DIM = 128
MAX_REL_DIST = 256
LRU_BLOCKS = 16
LRU_BLOCK_W = 128
CONV_W = 4
RG_C = 8.0
LN_EPS = 1e-5
NEG_INF = -1e30
DEEPNORM_ALPHA = (2.0 * DEPTH) ** 0.25

LANES = 128
SUBLANES = 8
MIB = 1024 * 1024

PROJ_TM = 1024
PROJ_TN = 1024
ATT_TQ = 256
ATT_WIN = LEFT + ATT_TQ
ATT_NCLS = LEFT // ATT_TQ + 1
ATT_HB = 2
OUT_TM = 256
LRU_TM = 256

F32 = jnp.float32
BF16 = jnp.bfloat16


def _proj_kernel(x_ref, w_ref, s_ref, o_ref):
    x = x_ref[...].astype(BF16)
    acc = jnp.dot(x, w_ref[...], preferred_element_type=F32)
    o_ref[...] = (acc * s_ref[...]).astype(o_ref.dtype)


def _proj(x, w, col_scale):
    m, k = x.shape
    n = w.shape[1]
    return pl.pallas_call(
        _proj_kernel,
        grid=(m // PROJ_TM, n // PROJ_TN),
        in_specs=[
            pl.BlockSpec((PROJ_TM, k), lambda i, j: (i, 0)),
            pl.BlockSpec((k, PROJ_TN), lambda i, j: (0, j)),
            pl.BlockSpec((1, PROJ_TN), lambda i, j: (0, j)),
        ],
        out_specs=pl.BlockSpec((PROJ_TM, PROJ_TN), lambda i, j: (i, j)),
        out_shape=jax.ShapeDtypeStruct((m, n), BF16),
        compiler_params=pltpu.CompilerParams(
            dimension_semantics=("parallel", "arbitrary"),
            vmem_limit_bytes=48 * MIB),
        name="attn_in_proj",
    )(x, w, col_scale)


def _attention_kernel(q_ref, k_ref, v_ref, g_ref, bias_ref, o_ref):
    qi = pl.program_id(2)
    ks = pl.multiple_of(jnp.maximum(qi * ATT_TQ - LEFT, 0), ATT_TQ)
    cls = jnp.minimum(qi, ATT_NCLS - 1)
    for hh in range(ATT_HB):
        cols = slice(hh * HEAD_DIM, (hh + 1) * HEAD_DIM)
        q = q_ref[:, cols]
        k = k_ref[pl.ds(ks, ATT_WIN), cols]
        v = v_ref[pl.ds(ks, ATT_WIN), cols]
        s = lax.dot_general(q, k, (((1,), (1,)), ((), ())),
                            preferred_element_type=F32)
        s = s + bias_ref[hh, cls]
        m = jnp.max(s, axis=-1, keepdims=True)
        p = jnp.exp(s - m)
        l = jnp.sum(p, axis=-1, keepdims=True)
        o = jnp.dot(p.astype(BF16), v, preferred_element_type=F32)
        o = o / l
        g = g_ref[:, cols].astype(F32)
        o_ref[:, cols] = (o * (g * jax.nn.sigmoid(g))).astype(o_ref.dtype)


def _attention_bias_tiles(rel_table):
    n_heads = rel_table.shape[0]
    i = np.arange(ATT_TQ)[:, None]
    j = np.arange(ATT_WIN)[None, :]
    span = ATT_TQ + ATT_WIN - 1
    edge = span // 2
    ext = jnp.pad(rel_table.astype(F32), ((0, 0), (edge, edge)), mode="edge")
    tiles = []
    for c in range(ATT_NCLS):
        off = min(c * ATT_TQ, LEFT)
        lo = edge + MAX_REL_DIST + ATT_TQ - 1 + off - (span - 1)
        r = ext[:, lo:lo + span][:, ::-1]
        r = jnp.pad(r, ((0, 0), (0, 1)))
        skew = jnp.tile(r, (1, ATT_TQ))[:, :ATT_TQ * span].reshape(n_heads, ATT_TQ, span)
        b = skew[:, :, ATT_TQ - 1:ATT_TQ - 1 + ATT_WIN]
        dchunk = i // CHUNK - j // CHUNK + off // CHUNK
        allowed = (dchunk >= 0) & (dchunk <= LEFT_CHUNKS)
        tiles.append(jnp.where(jnp.asarray(allowed)[None], b, NEG_INF))
    return jnp.stack(tiles, axis=1)


def _attention(qkvg, bias_tiles, batch, seq):
    m = qkvg.shape[0]
    d_att = ATT_HEADS * HEAD_DIM
    hw = ATT_HB * HEAD_DIM
    ncol = d_att // hw
    nq = seq // ATT_TQ
    return pl.pallas_call(
        _attention_kernel,
        grid=(ATT_HEADS // ATT_HB, batch, nq),
        in_specs=[
            pl.BlockSpec((ATT_TQ, hw), lambda h, b, q: (b * nq + q, h)),
            pl.BlockSpec((seq, hw), lambda h, b, q: (b, ncol + h)),
            pl.BlockSpec((seq, hw), lambda h, b, q: (b, 2 * ncol + h)),
            pl.BlockSpec((ATT_TQ, hw), lambda h, b, q: (b * nq + q, 3 * ncol + h)),
            pl.BlockSpec((ATT_HB, ATT_NCLS, ATT_TQ, ATT_WIN),
                         lambda h, b, q: (h, 0, 0, 0)),
        ],
        out_specs=pl.BlockSpec((ATT_TQ, hw), lambda h, b, q: (b * nq + q, h)),
        out_shape=jax.ShapeDtypeStruct((m, d_att), BF16),
        compiler_params=pltpu.CompilerParams(
            dimension_semantics=("parallel", "parallel", "arbitrary"),
            vmem_limit_bytes=48 * MIB),
        name="chunk_attention",
    )(qkvg, qkvg, qkvg, qkvg, bias_tiles)


def _out_ln_kernel(y_ref, w_ref, r_ref, gain_ref, bias_ref, o_ref):
    acc = jnp.dot(y_ref[...], w_ref[...], preferred_element_type=F32)
    z = DEEPNORM_ALPHA * r_ref[...] + acc
    mu = jnp.mean(z, axis=-1, keepdims=True)
    zc = z - mu
    var = jnp.mean(zc * zc, axis=-1, keepdims=True)
    o_ref[...] = zc * lax.rsqrt(var + LN_EPS) * gain_ref[...] + bias_ref[...]


def _out_ln(y, w, resid, gain, bias, name):
    m, k = y.shape
    n = w.shape[1]
    return pl.pallas_call(
        _out_ln_kernel,
        grid=(m // OUT_TM,),
        in_specs=[
            pl.BlockSpec((OUT_TM, k), lambda i: (i, 0)),
            pl.BlockSpec((k, n), lambda i: (0, 0)),
            pl.BlockSpec((OUT_TM, n), lambda i: (i, 0)),
            pl.BlockSpec((1, n), lambda i: (0, 0)),
            pl.BlockSpec((1, n), lambda i: (0, 0)),
        ],
        out_specs=pl.BlockSpec((OUT_TM, n), lambda i: (i, 0)),
        out_shape=jax.ShapeDtypeStruct((m, n), F32),
        compiler_params=pltpu.CompilerParams(
            dimension_semantics=("parallel",),
            vmem_limit_bytes=48 * MIB),
        name=name,
    )(y, w, resid, gain, bias)


def _linear_scan(a, b):
    t = a.shape[0]
    row = lax.broadcasted_iota(jnp.int32, (SUBLANES, a.shape[1]), 0)
    d = 1
    while d < t:
        if d < SUBLANES:
            a_sh = pltpu.roll(a, d, 0)
            b_sh = pltpu.roll(b, d, 0)
            keep = row >= d
            a_sh = jnp.concatenate(
                [jnp.where(keep, a_sh[:SUBLANES], 1.0), a_sh[SUBLANES:]], axis=0)
            b_sh = jnp.concatenate(
                [jnp.where(keep, b_sh[:SUBLANES], 0.0), b_sh[SUBLANES:]], axis=0)
            b = a * b_sh + b
            a = a * a_sh
        else:
            b = jnp.concatenate([b[:d], a[d:] * b[:-d] + b[d:]], axis=0)
            a = jnp.concatenate([a[:d], a[d:] * a[:-d]], axis=0)
        d *= 2
    return a, b


def _lru_kernel(h_ref, win_ref, cw_ref, cb_ref, wax_ref, bax_ref, lam_ref, y_ref,
                ubuf, gbuf, hcarry):
    t = pl.program_id(1)
    tm = h_ref.shape[0]
    d_rnn = y_ref.shape[1]

    @pl.when(t == 0)
    def _():
        ubuf[0:SUBLANES, :] = jnp.zeros((SUBLANES, d_rnn), F32)
        hcarry[...] = jnp.zeros_like(hcarry)

    x = h_ref[...].astype(BF16)
    proj = jnp.dot(x, win_ref[...], preferred_element_type=F32)
    ubuf[SUBLANES:SUBLANES + tm, :] = proj[:, :d_rnn]
    gbuf[...] = proj[:, d_rnn:]

    lam = lam_ref[...]
    softplus_neg_lam = jnp.maximum(-lam, 0.0) + jnp.log1p(jnp.exp(-jnp.abs(lam)))
    log_a_scale = -RG_C * softplus_neg_lam

    for n in range(LRU_BLOCKS):
        cols = slice(n * LRU_BLOCK_W, (n + 1) * LRU_BLOCK_W)
        u = cb_ref[:, cols]
        for tap in range(CONV_W):
            start = SUBLANES - (CONV_W - 1) + tap
            u = u + ubuf[start:start + tm, cols] * cw_ref[tap:tap + 1, cols]
        gates = jnp.dot(u.astype(BF16), wax_ref[n], preferred_element_type=F32)
        gates = jax.nn.sigmoid(gates + bax_ref[n])
        r = gates[:, :LRU_BLOCK_W]
        i = gates[:, LRU_BLOCK_W:]
        log_a = log_a_scale[:, cols] * r
        a = jnp.exp(log_a)
        mult = jnp.sqrt(1.0 - a * a)
        b = mult * (i * u)
        a_cum, h_loc = _linear_scan(a, b)
        h = a_cum * hcarry[:, cols] + h_loc
        hcarry[:, cols] = h[tm - 1:tm, :]
        g = gbuf[:, cols]
        y_ref[:, cols] = (h * (g * jax.nn.sigmoid(g))).astype(y_ref.dtype)

    ubuf[0:SUBLANES, :] = ubuf[tm:tm + SUBLANES, :]


def _lru(h, w_in, conv_w, conv_b, wax, bax, lam, batch, seq):
    m, d = h.shape
    d_rnn = lam.shape[1]
    nt = seq // LRU_TM
    const = lambda b, t: (0, 0)
    return pl.pallas_call(
        _lru_kernel,
        grid=(batch, nt),
        in_specs=[
            pl.BlockSpec((LRU_TM, d), lambda b, t: (b * nt + t, 0)),
            pl.BlockSpec((d, 2 * d_rnn), const, pipeline_mode=pl.Buffered(1)),
            pl.BlockSpec((CONV_W, d_rnn), const),
            pl.BlockSpec((1, d_rnn), const),
            pl.BlockSpec((LRU_BLOCKS, LRU_BLOCK_W, 2 * LRU_BLOCK_W),
                         lambda b, t: (0, 0, 0)),
            pl.BlockSpec((LRU_BLOCKS, 1, 2 * LRU_BLOCK_W), lambda b, t: (0, 0, 0)),
            pl.BlockSpec((1, d_rnn), const),
        ],
        out_specs=pl.BlockSpec((LRU_TM, d_rnn), lambda b, t: (b * nt + t, 0)),
        out_shape=jax.ShapeDtypeStruct((m, d_rnn), BF16),
        scratch_shapes=[
            pltpu.VMEM((LRU_TM + SUBLANES, d_rnn), F32),
            pltpu.VMEM((LRU_TM, d_rnn), F32),
            pltpu.VMEM((1, d_rnn), F32),
        ],
        compiler_params=pltpu.CompilerParams(
            dimension_semantics=("parallel", "arbitrary"),
            vmem_limit_bytes=56 * MIB),
        name="rglru",
    )(h, w_in, conv_w, conv_b, wax, bax, lam)


def kernel(x, attn_w_in, attn_w_out, attn_rel_bias, lru_w_in, lru_conv_w, lru_conv_b,
           lru_wa, lru_ba, lru_wx, lru_bx, lru_lambda, lru_w_out, ln_gain, ln_bias):
    batch, seq, d_model = x.shape
    m = batch * seq
    d_att = ATT_HEADS * HEAD_DIM
    h = x.reshape(m, d_model)

    col_scale = jnp.concatenate([jnp.full((d_att,), HEAD_DIM ** -0.5, F32),
                                 jnp.ones((3 * d_att,), F32)])[None, :]
    qkvg = _proj(h, attn_w_in[0].astype(BF16), col_scale)
    og = _attention(qkvg, _attention_bias_tiles(attn_rel_bias[0]), batch, seq)
    h = _out_ln(og, attn_w_out[0].astype(BF16), h, ln_gain[0][None, :], ln_bias[0][None, :],
                "attn_out_ln")

    wax = jnp.concatenate([lru_wa[0], lru_wx[0]], axis=-1).astype(BF16)
    bax = jnp.concatenate([lru_ba[0], lru_bx[0]], axis=-1)[:, None, :]
    y = _lru(h, lru_w_in[0].astype(BF16), lru_conv_w[0], lru_conv_b[0][None, :], wax, bax,
             lru_lambda[0][None, :], batch, seq)
    h = _out_ln(y, lru_w_out[0].astype(BF16), h, ln_gain[1][None, :], ln_bias[1][None, :],
                "lru_out_ln")
    return h.reshape(batch, seq, d_model)
```

```python
import math

import jax
import jax.numpy as jnp
import numpy as np
from jax import lax
from jax.experimental import pallas as pl
from jax.experimental.pallas import tpu as pltpu

DEPTH = 2
CHUNK = 64
LEFT_CHUNKS = 8
LEFT = LEFT_CHUNKS * CHUNK
ATT_HEADS = 16
HEAD_DIM = 128
MAX_REL_DIST = 256
LRU_BLOCKS = 16
LRU_BLOCK_W = 128
CONV_W = 4
RG_C = 8.0
LN_EPS = 1e-5
NEG_INF = -1e30
DEEPNORM_ALPHA = (2.0 * DEPTH) ** 0.25
LOG2_E = math.log2(math.e)
Q_SCALE = HEAD_DIM ** -0.5 * LOG2_E

LANES = 128
SUBLANES = 8
MIB = 1024 * 1024

PROJ_TM = 1024
PROJ_TN = 1024
ATT_TQ = 256
ATT_WIN = LEFT + ATT_TQ
ATT_HB = 2
OUT_TM = 512
OUT_SUB = 256
LRU_TM = 256
LRU_SUB = LRU_TM // SUBLANES

F32 = jnp.float32
BF16 = jnp.bfloat16


def _proj_kernel(x_ref, w_ref, s_ref, o_ref):
    x = x_ref[...].astype(BF16)
    acc = jnp.dot(x, w_ref[...], preferred_element_type=F32)
    o_ref[...] = (acc * s_ref[...]).astype(o_ref.dtype)


def _proj(x, w, col_scale):
    m, k = x.shape
    n = w.shape[1]
    return pl.pallas_call(
        _proj_kernel,
        grid=(m // PROJ_TM, n // PROJ_TN),
        in_specs=[
            pl.BlockSpec((PROJ_TM, k), lambda i, j: (i, 0)),
            pl.BlockSpec((k, PROJ_TN), lambda i, j: (0, j)),
            pl.BlockSpec((1, PROJ_TN), lambda i, j: (0, j)),
        ],
        out_specs=pl.BlockSpec((PROJ_TM, PROJ_TN), lambda i, j: (i, j)),
        out_shape=jax.ShapeDtypeStruct((m, n), BF16),
        compiler_params=pltpu.CompilerParams(
            dimension_semantics=("parallel", "arbitrary"),
            vmem_limit_bytes=48 * MIB),
        name="attn_in_proj",
    )(x, w, col_scale)


def _attention_kernel(q_ref, k_ref, v_ref, g_ref, bias_ref, o_ref, vt_ref):
    qi = pl.program_id(2)
    n_kblk = ATT_WIN // ATT_TQ
    kblk = jnp.maximum(qi - LEFT // ATT_TQ, 0)
    ks = pl.multiple_of(kblk * ATT_TQ, ATT_TQ)
    boff = pl.multiple_of(LEFT - jnp.minimum(qi * ATT_TQ, LEFT), ATT_TQ)

    @pl.when(qi == 0)
    def _():
        for hh in range(ATT_HB):
            cols = slice(hh * HEAD_DIM, (hh + 1) * HEAD_DIM)
            for c in range(v_ref.shape[0] // ATT_TQ):
                blk = v_ref[c * ATT_TQ:(c + 1) * ATT_TQ, cols].astype(F32)
                vt_ref[hh, c] = blk.T.astype(BF16)

    heads = [slice(hh * HEAD_DIM, (hh + 1) * HEAD_DIM) for hh in range(ATT_HB)]
    sts = []
    for cols in heads:
        k = k_ref[pl.ds(ks, ATT_WIN), cols]
        sts.append(lax.dot_general(k, q_ref[:, cols], (((1,), (1,)), ((), ())),
                                   preferred_element_type=F32))
    ps, ls = [], []
    for hh, st in enumerate(sts):
        st = st + bias_ref[hh, pl.ds(boff, ATT_WIN), :]
        m = jnp.max(st, axis=0, keepdims=True)
        p = jnp.exp2(st - m)
        ls.append(jnp.sum(p, axis=0, keepdims=True))
        ps.append(p.astype(BF16))
    ots = []
    for hh, p in enumerate(ps):
        ot = jnp.dot(vt_ref[hh, kblk], p[:ATT_TQ], preferred_element_type=F32)
        for c in range(1, n_kblk):
            ot = ot + jnp.dot(vt_ref[hh, kblk + c], p[c * ATT_TQ:(c + 1) * ATT_TQ],
                              preferred_element_type=F32)
        ots.append(ot)
    for cols, ot, l in zip(heads, ots, ls):
        o = (ot / l).T
        g = g_ref[:, cols].astype(F32)
        o_ref[:, cols] = (o * (g * jax.nn.sigmoid(g))).astype(o_ref.dtype)


def _attention_bias_tiles(rel_table):
    n_heads = rel_table.shape[0]
    n_rel = LEFT + ATT_WIN
    i = np.arange(ATT_TQ)[:, None]
    j = np.arange(n_rel)[None, :] - LEFT
    span = ATT_TQ + n_rel - 1
    edge = (span - (2 * MAX_REL_DIST + 1)) // 2
    ext = jnp.pad(rel_table.astype(F32), ((0, 0), (edge, edge)), mode="edge")
    r = jnp.pad(ext[:, ::-1], ((0, 0), (0, 1)))
    skew = jnp.tile(r, (1, ATT_TQ))[:, :ATT_TQ * span].reshape(n_heads, ATT_TQ, span)
    b = skew[:, :, ATT_TQ - 1:ATT_TQ - 1 + n_rel]
    dchunk = i // CHUNK - j // CHUNK
    allowed = (dchunk >= 0) & (dchunk <= LEFT_CHUNKS)
    b = jnp.where(jnp.asarray(allowed)[None], b * LOG2_E, NEG_INF)
    return b.transpose(0, 2, 1)


def _attention(qkvg, bias_tiles, batch, seq):
    m = qkvg.shape[0]
    d_att = ATT_HEADS * HEAD_DIM
    hw = ATT_HB * HEAD_DIM
    ncol = d_att // hw
    nq = seq // ATT_TQ
    return pl.pallas_call(
        _attention_kernel,
        grid=(ATT_HEADS // ATT_HB, batch, nq),
        in_specs=[
            pl.BlockSpec((ATT_TQ, hw), lambda h, b, q: (b * nq + q, h)),
            pl.BlockSpec((seq, hw), lambda h, b, q: (b, ncol + h)),
            pl.BlockSpec((seq, hw), lambda h, b, q: (b, 2 * ncol + h)),
            pl.BlockSpec((ATT_TQ, hw), lambda h, b, q: (b * nq + q, 3 * ncol + h)),
            pl.BlockSpec((ATT_HB, LEFT + ATT_WIN, ATT_TQ), lambda h, b, q: (h, 0, 0)),
        ],
        out_specs=pl.BlockSpec((ATT_TQ, hw), lambda h, b, q: (b * nq + q, h)),
        out_shape=jax.ShapeDtypeStruct((m, d_att), BF16),
        scratch_shapes=[
            pltpu.VMEM((ATT_HB, seq // ATT_TQ, HEAD_DIM, ATT_TQ), BF16),
        ],
        compiler_params=pltpu.CompilerParams(
            dimension_semantics=("parallel", "parallel", "arbitrary"),
            vmem_limit_bytes=48 * MIB),
        name="chunk_attention",
    )(qkvg, qkvg, qkvg, qkvg, bias_tiles)


def _out_ln_kernel(y_ref, w_ref, r_ref, gain_ref, bias_ref, o_ref):
    for i in range(OUT_TM // OUT_SUB):
        rows = slice(i * OUT_SUB, (i + 1) * OUT_SUB)
        acc = jnp.dot(y_ref[rows, :], w_ref[...], preferred_element_type=F32)
        z = DEEPNORM_ALPHA * r_ref[rows, :] + acc
        mu = jnp.mean(z, axis=-1, keepdims=True)
        zc = z - mu
        var = jnp.mean(zc * zc, axis=-1, keepdims=True)
        o_ref[rows, :] = zc * lax.rsqrt(var + LN_EPS) * gain_ref[...] + bias_ref[...]


def _out_ln(y, w, resid, gain, bias, name):
    m, k = y.shape
    n = w.shape[1]
    return pl.pallas_call(
        _out_ln_kernel,
        grid=(m // OUT_TM,),
        in_specs=[
            pl.BlockSpec((OUT_TM, k), lambda i: (i, 0)),
            pl.BlockSpec((k, n), lambda i: (0, 0)),
            pl.BlockSpec((OUT_TM, n), lambda i: (i, 0)),
            pl.BlockSpec((1, n), lambda i: (0, 0)),
            pl.BlockSpec((1, n), lambda i: (0, 0)),
        ],
        out_specs=pl.BlockSpec((OUT_TM, n), lambda i: (i, 0)),
        out_shape=jax.ShapeDtypeStruct((m, n), F32),
        compiler_params=pltpu.CompilerParams(
            dimension_semantics=("parallel",),
            vmem_limit_bytes=48 * MIB),
        name=name,
    )(y, w, resid, gain, bias)


def _time_permutation():
    r = np.arange(LRU_TM)
    p = np.zeros((LRU_TM, LRU_TM), np.float32)
    p[r, LRU_SUB * (r % SUBLANES) + r // SUBLANES] = 1.0
    return p


def _sublane_scan(a, b):
    row = lax.broadcasted_iota(jnp.int32, a.shape, 0)
    d = 1
    while d < SUBLANES:
        keep = row >= d
        a_sh = jnp.where(keep, pltpu.roll(a, d, 0), 1.0)
        b_sh = jnp.where(keep, pltpu.roll(b, d, 0), 0.0)
        b = a * b_sh + b
        a = a * a_sh
        d *= 2
    return a, b


def _lru_kernel(h_ref, perm_ref, win_ref, cw_ref, cb_ref, wax_ref, bax_ref, lam_ref, y_ref,
                ubuf, gbuf, ybuf, tailbuf, hcarry):
    t = pl.program_id(1)
    tm = h_ref.shape[0]
    d_rnn = y_ref.shape[1]
    hist = (CONV_W - 1) * SUBLANES

    @pl.when(t == 0)
    def _():
        tailbuf[...] = jnp.zeros_like(tailbuf)
        hcarry[...] = jnp.zeros_like(hcarry)

    x = h_ref[...].astype(BF16)
    xp = jnp.dot(perm_ref[0], x, preferred_element_type=F32).astype(BF16)
    proj = jnp.dot(xp, win_ref[...], preferred_element_type=F32)
    ubuf[hist:hist + tm, :] = proj[:, :d_rnn]
    gbuf[...] = proj[:, d_rnn:]

    row0 = lax.broadcasted_iota(jnp.int32, (SUBLANES, d_rnn), 0) == 0
    for j in range(CONV_W - 1):
        rows = slice(j * SUBLANES, (j + 1) * SUBLANES)
        cur = ubuf[hist + tm - hist + j * SUBLANES:hist + tm - hist + (j + 1) * SUBLANES, :]
        prev = tailbuf[rows, :]
        ubuf[rows, :] = jnp.where(row0, pltpu.roll(prev, 1, 0), pltpu.roll(cur, 1, 0))
        tailbuf[rows, :] = cur

    lam = lam_ref[...]
    softplus_neg_lam = jnp.maximum(-lam, 0.0) + jnp.log1p(jnp.exp(-jnp.abs(lam)))
    log2_a_scale = (-RG_C * LOG2_E) * softplus_neg_lam

    for n in range(LRU_BLOCKS):
        cols = slice(n * LRU_BLOCK_W, (n + 1) * LRU_BLOCK_W)
        u = cb_ref[:, cols]
        for tap in range(CONV_W):
            start = tap * SUBLANES
            u = u + ubuf[start:start + tm, cols] * cw_ref[tap:tap + 1, cols]
        gates = jnp.dot(u.astype(BF16), wax_ref[n], preferred_element_type=F32)
        gates = jax.nn.sigmoid(gates + bax_ref[n])
        r = gates[:, :LRU_BLOCK_W]
        i = gates[:, LRU_BLOCK_W:]
        a = jnp.exp2(log2_a_scale[:, cols] * r)
        one_m_a2 = 1.0 - a * a
        mult = jnp.where(one_m_a2 > 0.0, one_m_a2 * lax.rsqrt(one_m_a2), 0.0)
        b = mult * (i * u)

        hs, ps = [b[:SUBLANES]], [a[:SUBLANES]]
        for k in range(1, LRU_SUB):
            rows = slice(k * SUBLANES, (k + 1) * SUBLANES)
            hs.append(a[rows] * hs[-1] + b[rows])
            ps.append(a[rows] * ps[-1])
        p_inc, h_inc = _sublane_scan(ps[-1], hs[-1])
        h0 = hcarry[:, cols]
        row = lax.broadcasted_iota(jnp.int32, p_inc.shape, 0)
        p_exc = jnp.where(row == 0, 1.0, pltpu.roll(p_inc, 1, 0))
        h_exc = jnp.where(row == 0, 0.0, pltpu.roll(h_inc, 1, 0))
        carry = p_exc * h0 + h_exc
        hcarry[:, cols] = (p_inc * h0 + h_inc)[SUBLANES - 1:SUBLANES, :]
        h = jnp.concatenate([hk + pk * carry for hk, pk in zip(hs, ps)], axis=0)
        g = gbuf[:, cols]
        ybuf[:, cols] = (h * (g * jax.nn.sigmoid(g))).astype(BF16)

    y = jnp.dot(perm_ref[1], ybuf[...], preferred_element_type=F32)
    y_ref[...] = y.astype(y_ref.dtype)


def _lru(h, w_in, conv_w, conv_b, wax, bax, lam, batch, seq):
    m, d = h.shape
    d_rnn = lam.shape[1]
    nt = seq // LRU_TM
    perm = _time_permutation()
    perms = jnp.asarray(np.stack([perm, perm.T]), BF16)
    const = lambda b, t: (0, 0)
    const3 = lambda b, t: (0, 0, 0)
    hist = (CONV_W - 1) * SUBLANES
    return pl.pallas_call(
        _lru_kernel,
        grid=(batch, nt),
        in_specs=[
            pl.BlockSpec((LRU_TM, d), lambda b, t: (b * nt + t, 0)),
            pl.BlockSpec((2, LRU_TM, LRU_TM), const3),
            pl.BlockSpec((d, 2 * d_rnn), const, pipeline_mode=pl.Buffered(1)),
            pl.BlockSpec((CONV_W, d_rnn), const),
            pl.BlockSpec((1, d_rnn), const),
            pl.BlockSpec((LRU_BLOCKS, LRU_BLOCK_W, 2 * LRU_BLOCK_W), const3),
            pl.BlockSpec((LRU_BLOCKS, 1, 2 * LRU_BLOCK_W), const3),
            pl.BlockSpec((1, d_rnn), const),
        ],
        out_specs=pl.BlockSpec((LRU_TM, d_rnn), lambda b, t: (b * nt + t, 0)),
        out_shape=jax.ShapeDtypeStruct((m, d_rnn), BF16),
        scratch_shapes=[
            pltpu.VMEM((hist + LRU_TM, d_rnn), F32),
            pltpu.VMEM((LRU_TM, d_rnn), F32),
            pltpu.VMEM((LRU_TM, d_rnn), BF16),
            pltpu.VMEM((hist, d_rnn), F32),
            pltpu.VMEM((1, d_rnn), F32),
        ],
        compiler_params=pltpu.CompilerParams(
            dimension_semantics=("parallel", "arbitrary"),
            vmem_limit_bytes=56 * MIB),
        name="rglru",
    )(h, perms, w_in, conv_w, conv_b, wax, bax, lam)


def kernel(x, attn_w_in, attn_w_out, attn_rel_bias, lru_w_in, lru_conv_w, lru_conv_b,
           lru_wa, lru_ba, lru_wx, lru_bx, lru_lambda, lru_w_out, ln_gain, ln_bias):
    batch, seq, d_model = x.shape
    m = batch * seq
    d_att = ATT_HEADS * HEAD_DIM
    h = x.reshape(m, d_model)

    col_scale = jnp.concatenate([jnp.full((d_att,), Q_SCALE, F32),
                                 jnp.ones((3 * d_att,), F32)])[None, :]
    qkvg = _proj(h, attn_w_in[0].astype(BF16), col_scale)
    og = _attention(qkvg, _attention_bias_tiles(attn_rel_bias[0]), batch, seq)
    h = _out_ln(og, attn_w_out[0].astype(BF16), h, ln_gain[0][None, :], ln_bias[0][None, :],
                "attn_out_ln")

    wax = jnp.concatenate([lru_wa[0], lru_wx[0]], axis=-1).astype(BF16)
    bax = jnp.concatenate([lru_ba[0], lru_bx[0]], axis=-1)[:, None, :]
    y = _lru(h, lru_w_in[0].astype(BF16), lru_conv_w[0], lru_conv_b[0][None, :], wax, bax,
             lru_lambda[0][None, :], batch, seq)
    h = _out_ln(y, lru_w_out[0].astype(BF16), h, ln_gain[1][None, :], ln_bias[1][None, :],
                "lru_out_ln")
    return h.reshape(batch, seq, d_model)
```

```python
import math

import jax
import jax.numpy as jnp
import numpy as np
from jax import lax
from jax.experimental import pallas as pl
from jax.experimental.pallas import tpu as pltpu

DEPTH = 2
CHUNK = 64
LEFT_CHUNKS = 8
LEFT = LEFT_CHUNKS * CHUNK
ATT_HEADS = 16
HEAD_DIM = 128
MAX_REL_DIST = 256
LRU_BLOCKS = 16
LRU_BLOCK_W = 128
CONV_W = 4
RG_C = 8.0
LN_EPS = 1e-5
NEG_INF = -1e30
DEEPNORM_ALPHA = (2.0 * DEPTH) ** 0.25
LOG2_E = math.log2(math.e)
Q_SCALE = HEAD_DIM ** -0.5 * LOG2_E

LANES = 128
SUBLANES = 8
MIB = 1024 * 1024

PROJ_TM = 1024
PROJ_TN = 1024
ATT_TQ = 256
ATT_WIN = LEFT + ATT_TQ
ATT_HB = 4
OUT_TM = 512
OUT_SUB = 256
LRU_TM = 256
LRU_SUB = LRU_TM // SUBLANES

F32 = jnp.float32
BF16 = jnp.bfloat16


def _proj_kernel(x_ref, w_ref, s_ref, o_ref):
    x = x_ref[...].astype(BF16)
    acc = jnp.dot(x, w_ref[...], preferred_element_type=F32)
    o_ref[...] = (acc * s_ref[...]).astype(o_ref.dtype)


def _proj(x, w, col_scale):
    m, k = x.shape
    n = w.shape[1]
    return pl.pallas_call(
        _proj_kernel,
        grid=(m // PROJ_TM, n // PROJ_TN),
        in_specs=[
            pl.BlockSpec((PROJ_TM, k), lambda i, j: (i, 0)),
            pl.BlockSpec((k, PROJ_TN), lambda i, j: (0, j)),
            pl.BlockSpec((1, PROJ_TN), lambda i, j: (0, j)),
        ],
        out_specs=pl.BlockSpec((PROJ_TM, PROJ_TN), lambda i, j: (i, j)),
        out_shape=jax.ShapeDtypeStruct((m, n), BF16),
        compiler_params=pltpu.CompilerParams(
            dimension_semantics=("parallel", "arbitrary"),
            vmem_limit_bytes=48 * MIB),
        name="attn_in_proj",
    )(x, w, col_scale)


def _attention_kernel(q_ref, k_ref, v_ref, g_ref, bias_ref, o_ref, vt_ref):
    qi = pl.program_id(2)
    n_kblk = ATT_WIN // ATT_TQ
    kblk = jnp.maximum(qi - LEFT // ATT_TQ, 0)
    ks = pl.multiple_of(kblk * ATT_TQ, ATT_TQ)
    boff = pl.multiple_of(LEFT - jnp.minimum(qi * ATT_TQ, LEFT), ATT_TQ)

    @pl.when(qi == 0)
    def _():
        for hh in range(ATT_HB):
            cols = slice(hh * HEAD_DIM, (hh + 1) * HEAD_DIM)
            for c in range(v_ref.shape[0] // ATT_TQ):
                blk = v_ref[c * ATT_TQ:(c + 1) * ATT_TQ, cols].astype(F32)
                vt_ref[hh, c] = blk.T.astype(BF16)

    def head_cols(hh):
        return slice(hh * HEAD_DIM, (hh + 1) * HEAD_DIM)

    def scores(hh):
        q = q_ref[:, head_cols(hh)]
        sts = []
        for c in range(n_kblk):
            k = k_ref[pl.ds(ks + c * ATT_TQ, ATT_TQ), head_cols(hh)]
            st = lax.dot_general(k, q, (((1,), (1,)), ((), ())), preferred_element_type=F32)
            sts.append(st + bias_ref[hh, pl.ds(boff + c * ATT_TQ, ATT_TQ), :])
        return sts

    def attend(hh, sts):
        m = jnp.max(sts[0], axis=0, keepdims=True)
        for st in sts[1:]:
            m = jnp.maximum(m, jnp.max(st, axis=0, keepdims=True))
        l, ot = None, None
        for c, st in enumerate(sts):
            p = jnp.exp2(st - m)
            lc = jnp.sum(p, axis=0, keepdims=True)
            oc = jnp.dot(vt_ref[hh, kblk + c], p.astype(BF16), preferred_element_type=F32)
            l = lc if l is None else l + lc
            ot = oc if ot is None else ot + oc
        return ot, l

    def finish(hh, ot, l):
        o = (ot / l).T
        g = g_ref[:, head_cols(hh)].astype(F32)
        o_ref[:, head_cols(hh)] = (o * (g * jax.nn.sigmoid(g))).astype(o_ref.dtype)

    sts_next = scores(0)
    for hh in range(ATT_HB):
        sts = sts_next
        if hh + 1 < ATT_HB:
            sts_next = scores(hh + 1)
        finish(hh, *attend(hh, sts))


def _attention_bias_tiles(rel_table):
    n_heads = rel_table.shape[0]
    n_rel = LEFT + ATT_WIN
    i = np.arange(ATT_TQ)[:, None]
    j = np.arange(n_rel)[None, :] - LEFT
    span = ATT_TQ + n_rel - 1
    edge = (span - (2 * MAX_REL_DIST + 1)) // 2
    ext = jnp.pad(rel_table.astype(F32), ((0, 0), (edge, edge)), mode="edge")
    r = jnp.pad(ext[:, ::-1], ((0, 0), (0, 1)))
    skew = jnp.tile(r, (1, ATT_TQ))[:, :ATT_TQ * span].reshape(n_heads, ATT_TQ, span)
    b = skew[:, :, ATT_TQ - 1:ATT_TQ - 1 + n_rel]
    dchunk = i // CHUNK - j // CHUNK
    allowed = (dchunk >= 0) & (dchunk <= LEFT_CHUNKS)
    b = jnp.where(jnp.asarray(allowed)[None], b * LOG2_E, NEG_INF)
    return b.transpose(0, 2, 1)


def _attention(qkvg, bias_tiles, batch, seq):
    m = qkvg.shape[0]
    d_att = ATT_HEADS * HEAD_DIM
    hw = ATT_HB * HEAD_DIM
    ncol = d_att // hw
    nq = seq // ATT_TQ
    return pl.pallas_call(
        _attention_kernel,
        grid=(ATT_HEADS // ATT_HB, batch, nq),
        in_specs=[
            pl.BlockSpec((ATT_TQ, hw), lambda h, b, q: (b * nq + q, h)),
            pl.BlockSpec((seq, hw), lambda h, b, q: (b, ncol + h)),
            pl.BlockSpec((seq, hw), lambda h, b, q: (b, 2 * ncol + h)),
            pl.BlockSpec((ATT_TQ, hw), lambda h, b, q: (b * nq + q, 3 * ncol + h)),
            pl.BlockSpec((ATT_HB, LEFT + ATT_WIN, ATT_TQ), lambda h, b, q: (h, 0, 0)),
        ],
        out_specs=pl.BlockSpec((ATT_TQ, hw), lambda h, b, q: (b * nq + q, h)),
        out_shape=jax.ShapeDtypeStruct((m, d_att), BF16),
        scratch_shapes=[
            pltpu.VMEM((ATT_HB, seq // ATT_TQ, HEAD_DIM, ATT_TQ), BF16),
        ],
        compiler_params=pltpu.CompilerParams(
            dimension_semantics=("parallel", "parallel", "arbitrary"),
            vmem_limit_bytes=48 * MIB),
        name="chunk_attention",
    )(qkvg, qkvg, qkvg, qkvg, bias_tiles)


def _out_ln_kernel(y_ref, w_ref, r_ref, gain_ref, bias_ref, o_ref):
    for i in range(OUT_TM // OUT_SUB):
        rows = slice(i * OUT_SUB, (i + 1) * OUT_SUB)
        acc = jnp.dot(y_ref[rows, :], w_ref[...], preferred_element_type=F32)
        z = DEEPNORM_ALPHA * r_ref[rows, :] + acc
        mu = jnp.mean(z, axis=-1, keepdims=True)
        zc = z - mu
        var = jnp.mean(zc * zc, axis=-1, keepdims=True)
        o_ref[rows, :] = zc * lax.rsqrt(var + LN_EPS) * gain_ref[...] + bias_ref[...]


def _out_ln(y, w, resid, gain, bias, name):
    m, k = y.shape
    n = w.shape[1]
    return pl.pallas_call(
        _out_ln_kernel,
        grid=(m // OUT_TM,),
        in_specs=[
            pl.BlockSpec((OUT_TM, k), lambda i: (i, 0)),
            pl.BlockSpec((k, n), lambda i: (0, 0)),
            pl.BlockSpec((OUT_TM, n), lambda i: (i, 0)),
            pl.BlockSpec((1, n), lambda i: (0, 0)),
            pl.BlockSpec((1, n), lambda i: (0, 0)),
        ],
        out_specs=pl.BlockSpec((OUT_TM, n), lambda i: (i, 0)),
        out_shape=jax.ShapeDtypeStruct((m, n), F32),
        compiler_params=pltpu.CompilerParams(
            dimension_semantics=("parallel",),
            vmem_limit_bytes=48 * MIB),
        name=name,
    )(y, w, resid, gain, bias)


def _time_permutation():
    r = np.arange(LRU_TM)
    p = np.zeros((LRU_TM, LRU_TM), np.float32)
    p[r, LRU_SUB * (r % SUBLANES) + r // SUBLANES] = 1.0
    return p


def _sublane_scan(a, b):
    row = lax.broadcasted_iota(jnp.int32, a.shape, 0)
    d = 1
    while d < SUBLANES:
        keep = row >= d
        a_sh = jnp.where(keep, pltpu.roll(a, d, 0), 1.0)
        b_sh = jnp.where(keep, pltpu.roll(b, d, 0), 0.0)
        b = a * b_sh + b
        a = a * a_sh
        d *= 2
    return a, b


def _lru_kernel(h_ref, perm_ref, win_ref, cw_ref, cb_ref, wax_ref, bax_ref, lam_ref, y_ref,
                ubuf, gbuf, ybuf, tailbuf, hcarry):
    t = pl.program_id(1)
    tm = h_ref.shape[0]
    d_rnn = y_ref.shape[1]
    hist = (CONV_W - 1) * SUBLANES

    @pl.when(t == 0)
    def _():
        tailbuf[...] = jnp.zeros_like(tailbuf)
        hcarry[...] = jnp.zeros_like(hcarry)

    x = h_ref[...].astype(BF16)
    xp = jnp.dot(perm_ref[0], x, preferred_element_type=F32).astype(BF16)
    proj = jnp.dot(xp, win_ref[...], preferred_element_type=F32)
    ubuf[hist:hist + tm, :] = proj[:, :d_rnn]
    gbuf[...] = proj[:, d_rnn:]

    row0 = lax.broadcasted_iota(jnp.int32, (SUBLANES, d_rnn), 0) == 0
    for j in range(CONV_W - 1):
        rows = slice(j * SUBLANES, (j + 1) * SUBLANES)
        cur = ubuf[hist + tm - hist + j * SUBLANES:hist + tm - hist + (j + 1) * SUBLANES, :]
        prev = tailbuf[rows, :]
        ubuf[rows, :] = jnp.where(row0, pltpu.roll(prev, 1, 0), pltpu.roll(cur, 1, 0))
        tailbuf[rows, :] = cur

    lam = lam_ref[...]
    softplus_neg_lam = jnp.maximum(-lam, 0.0) + jnp.log1p(jnp.exp(-jnp.abs(lam)))
    log2_a_scale = (-RG_C * LOG2_E) * softplus_neg_lam

    for n in range(LRU_BLOCKS):
        cols = slice(n * LRU_BLOCK_W, (n + 1) * LRU_BLOCK_W)
        u = cb_ref[:, cols]
        for tap in range(CONV_W):
            start = tap * SUBLANES
            u = u + ubuf[start:start + tm, cols] * cw_ref[tap:tap + 1, cols]
        gates = jnp.dot(u.astype(BF16), wax_ref[n], preferred_element_type=F32)
        gates = jax.nn.sigmoid(gates + bax_ref[n])
        r = gates[:, :LRU_BLOCK_W]
        i = gates[:, LRU_BLOCK_W:]
        a = jnp.exp2(log2_a_scale[:, cols] * r)
        one_m_a2 = 1.0 - a * a
        mult = jnp.where(one_m_a2 > 0.0, one_m_a2 * lax.rsqrt(one_m_a2), 0.0)
        b = mult * (i * u)

        hs, ps = [b[:SUBLANES]], [a[:SUBLANES]]
        for k in range(1, LRU_SUB):
            rows = slice(k * SUBLANES, (k + 1) * SUBLANES)
            hs.append(a[rows] * hs[-1] + b[rows])
            ps.append(a[rows] * ps[-1])
        p_inc, h_inc = _sublane_scan(ps[-1], hs[-1])
        h0 = hcarry[:, cols]
        row = lax.broadcasted_iota(jnp.int32, p_inc.shape, 0)
        p_exc = jnp.where(row == 0, 1.0, pltpu.roll(p_inc, 1, 0))
        h_exc = jnp.where(row == 0, 0.0, pltpu.roll(h_inc, 1, 0))
        carry = p_exc * h0 + h_exc
        hcarry[:, cols] = (p_inc * h0 + h_inc)[SUBLANES - 1:SUBLANES, :]
        h = jnp.concatenate([hk + pk * carry for hk, pk in zip(hs, ps)], axis=0)
        g = gbuf[:, cols]
        ybuf[:, cols] = (h * (g * jax.nn.sigmoid(g))).astype(BF16)

    y = jnp.dot(perm_ref[1], ybuf[...], preferred_element_type=F32)
    y_ref[...] = y.astype(y_ref.dtype)


def _lru(h, w_in, conv_w, conv_b, wax, bax, lam, batch, seq):
    m, d = h.shape
    d_rnn = lam.shape[1]
    nt = seq // LRU_TM
    perm = _time_permutation()
    perms = jnp.asarray(np.stack([perm, perm.T]), BF16)
    const = lambda b, t: (0, 0)
    const3 = lambda b, t: (0, 0, 0)
    hist = (CONV_W - 1) * SUBLANES
    return pl.pallas_call(
        _lru_kernel,
        grid=(batch, nt),
        in_specs=[
            pl.BlockSpec((LRU_TM, d), lambda b, t: (b * nt + t, 0)),
            pl.BlockSpec((2, LRU_TM, LRU_TM), const3),
            pl.BlockSpec((d, 2 * d_rnn), const, pipeline_mode=pl.Buffered(1)),
            pl.BlockSpec((CONV_W, d_rnn), const),
            pl.BlockSpec((1, d_rnn), const),
            pl.BlockSpec((LRU_BLOCKS, LRU_BLOCK_W, 2 * LRU_BLOCK_W), const3),
            pl.BlockSpec((LRU_BLOCKS, 1, 2 * LRU_BLOCK_W), const3),
            pl.BlockSpec((1, d_rnn), const),
        ],
        out_specs=pl.BlockSpec((LRU_TM, d_rnn), lambda b, t: (b * nt + t, 0)),
        out_shape=jax.ShapeDtypeStruct((m, d_rnn), BF16),
        scratch_shapes=[
            pltpu.VMEM((hist + LRU_TM, d_rnn), F32),
            pltpu.VMEM((LRU_TM, d_rnn), F32),
            pltpu.VMEM((LRU_TM, d_rnn), BF16),
            pltpu.VMEM((hist, d_rnn), F32),
            pltpu.VMEM((1, d_rnn), F32),
        ],
        compiler_params=pltpu.CompilerParams(
            dimension_semantics=("parallel", "arbitrary"),
            vmem_limit_bytes=56 * MIB),
        name="rglru",
    )(h, perms, w_in, conv_w, conv_b, wax, bax, lam)


def kernel(x, attn_w_in, attn_w_out, attn_rel_bias, lru_w_in, lru_conv_w, lru_conv_b,
           lru_wa, lru_ba, lru_wx, lru_bx, lru_lambda, lru_w_out, ln_gain, ln_bias):
    batch, seq, d_model = x.shape
    m = batch * seq
    d_att = ATT_HEADS * HEAD_DIM
    h = x.reshape(m, d_model)

    col_scale = jnp.concatenate([jnp.full((d_att,), Q_SCALE, F32),
                                 jnp.ones((3 * d_att,), F32)])[None, :]
    qkvg = _proj(h, attn_w_in[0].astype(BF16), col_scale)
    og = _attention(qkvg, _attention_bias_tiles(attn_rel_bias[0]), batch, seq)
    h = _out_ln(og, attn_w_out[0].astype(BF16), h, ln_gain[0][None, :], ln_bias[0][None, :],
                "attn_out_ln")

    wax = jnp.concatenate([lru_wa[0], lru_wx[0]], axis=-1).astype(BF16)
    bax = jnp.concatenate([lru_ba[0], lru_bx[0]], axis=-1)[:, None, :]
    y = _lru(h, lru_w_in[0].astype(BF16), lru_conv_w[0], lru_conv_b[0][None, :], wax, bax,
             lru_lambda[0][None, :], batch, seq)
    h = _out_ln(y, lru_w_out[0].astype(BF16), h, ln_gain[1][None, :], ln_bias[1][None, :],
                "lru_out_ln")
    return h.reshape(batch, seq, d_model)
```

```python
import math

import jax
import jax.numpy as jnp
import numpy as np
from jax import lax
from jax.experimental import pallas as pl
from jax.experimental.pallas import tpu as pltpu

DEPTH = 2
CHUNK = 64
LEFT_CHUNKS = 8
LEFT = LEFT_CHUNKS * CHUNK
ATT_HEADS = 16
HEAD_DIM = 128
MAX_REL_DIST = 256
LRU_BLOCKS = 16
LRU_BLOCK_W = 128
CONV_W = 4
RG_C = 8.0
LN_EPS = 1e-5
NEG_INF = -1e30
DEEPNORM_ALPHA = (2.0 * DEPTH) ** 0.25
LOG2_E = math.log2(math.e)
Q_SCALE = HEAD_DIM ** -0.5 * LOG2_E

LANES = 128
SUBLANES = 8
MIB = 1024 * 1024

PROJ_TM = 1024
PROJ_TN = 1024
ATT_TQ = 256
ATT_WIN = LEFT + ATT_TQ
ATT_HB = 4
ATT_AHEAD = 2
OUT_TM = 512
OUT_SUB = 256
LRU_TM = 256
LRU_SUB = LRU_TM // SUBLANES
LRU_GROUP = 1
LRU_AHEAD = 2

F32 = jnp.float32
BF16 = jnp.bfloat16


def _proj_kernel(x_ref, w_ref, s_ref, o_ref):
    x = x_ref[...].astype(BF16)
    acc = jnp.dot(x, w_ref[...], preferred_element_type=F32)
    o_ref[...] = (acc * s_ref[...]).astype(o_ref.dtype)


def _proj(x, w, col_scale):
    m, k = x.shape
    n = w.shape[1]
    return pl.pallas_call(
        _proj_kernel,
        grid=(m // PROJ_TM, n // PROJ_TN),
        in_specs=[
            pl.BlockSpec((PROJ_TM, k), lambda i, j: (i, 0)),
            pl.BlockSpec((k, PROJ_TN), lambda i, j: (0, j)),
            pl.BlockSpec((1, PROJ_TN), lambda i, j: (0, j)),
        ],
        out_specs=pl.BlockSpec((PROJ_TM, PROJ_TN), lambda i, j: (i, j)),
        out_shape=jax.ShapeDtypeStruct((m, n), BF16),
        compiler_params=pltpu.CompilerParams(
            dimension_semantics=("parallel", "arbitrary"),
            vmem_limit_bytes=48 * MIB),
        name="attn_in_proj",
    )(x, w, col_scale)


def _attention_kernel(q_ref, k_ref, v_ref, g_ref, bias_ref, o_ref, vt_ref):
    qi = pl.program_id(2)
    n_kblk = ATT_WIN // ATT_TQ
    kblk = jnp.maximum(qi - LEFT // ATT_TQ, 0)
    ks = pl.multiple_of(kblk * ATT_TQ, ATT_TQ)
    boff = pl.multiple_of(LEFT - jnp.minimum(qi * ATT_TQ, LEFT), ATT_TQ)

    @pl.when(qi == 0)
    def _():
        for hh in range(ATT_HB):
            cols = slice(hh * HEAD_DIM, (hh + 1) * HEAD_DIM)
            for c in range(v_ref.shape[0] // ATT_TQ):
                blk = v_ref[c * ATT_TQ:(c + 1) * ATT_TQ, cols].astype(F32)
                vt_ref[hh, c] = blk.T.astype(BF16)

    def head_cols(hh):
        return slice(hh * HEAD_DIM, (hh + 1) * HEAD_DIM)

    def scores(hh):
        q = q_ref[:, head_cols(hh)]
        sts = []
        for c in range(n_kblk):
            k = k_ref[pl.ds(ks + c * ATT_TQ, ATT_TQ), head_cols(hh)]
            st = lax.dot_general(k, q, (((1,), (1,)), ((), ())), preferred_element_type=F32)
            sts.append(st + bias_ref[hh, pl.ds(boff + c * ATT_TQ, ATT_TQ), :])
        return sts

    def attend(hh, sts):
        m = jnp.max(sts[0], axis=0, keepdims=True)
        for st in sts[1:]:
            m = jnp.maximum(m, jnp.max(st, axis=0, keepdims=True))
        l, ot = None, None
        for c, st in enumerate(sts):
            p = jnp.exp2(st - m)
            lc = jnp.sum(p, axis=0, keepdims=True)
            oc = jnp.dot(vt_ref[hh, kblk + c], p.astype(BF16), preferred_element_type=F32)
            l = lc if l is None else l + lc
            ot = oc if ot is None else ot + oc
        return ot, l

    def finish(hh, ot, l):
        o = (ot / l).T
        g = g_ref[:, head_cols(hh)].astype(F32)
        o_ref[:, head_cols(hh)] = (o * (g * jax.nn.sigmoid(g))).astype(o_ref.dtype)

    pending = [scores(hh) for hh in range(ATT_AHEAD)]
    for hh in range(ATT_HB):
        sts = pending.pop(0)
        if hh + ATT_AHEAD < ATT_HB:
            pending.append(scores(hh + ATT_AHEAD))
        finish(hh, *attend(hh, sts))


def _attention_bias_tiles(rel_table):
    n_heads = rel_table.shape[0]
    n_rel = LEFT + ATT_WIN
    i = np.arange(ATT_TQ)[:, None]
    j = np.arange(n_rel)[None, :] - LEFT
    span = ATT_TQ + n_rel - 1
    edge = (span - (2 * MAX_REL_DIST + 1)) // 2
    ext = jnp.pad(rel_table.astype(F32), ((0, 0), (edge, edge)), mode="edge")
    r = jnp.pad(ext[:, ::-1], ((0, 0), (0, 1)))
    skew = jnp.tile(r, (1, ATT_TQ))[:, :ATT_TQ * span].reshape(n_heads, ATT_TQ, span)
    b = skew[:, :, ATT_TQ - 1:ATT_TQ - 1 + n_rel]
    dchunk = i // CHUNK - j // CHUNK
    allowed = (dchunk >= 0) & (dchunk <= LEFT_CHUNKS)
    b = jnp.where(jnp.asarray(allowed)[None], b * LOG2_E, NEG_INF)
    return b.transpose(0, 2, 1)


def _attention(qkvg, bias_tiles, batch, seq):
    m = qkvg.shape[0]
    d_att = ATT_HEADS * HEAD_DIM
    hw = ATT_HB * HEAD_DIM
    ncol = d_att // hw
    nq = seq // ATT_TQ
    return pl.pallas_call(
        _attention_kernel,
        grid=(ATT_HEADS // ATT_HB, batch, nq),
        in_specs=[
            pl.BlockSpec((ATT_TQ, hw), lambda h, b, q: (b * nq + q, h)),
            pl.BlockSpec((seq, hw), lambda h, b, q: (b, ncol + h)),
            pl.BlockSpec((seq, hw), lambda h, b, q: (b, 2 * ncol + h)),
            pl.BlockSpec((ATT_TQ, hw), lambda h, b, q: (b * nq + q, 3 * ncol + h)),
            pl.BlockSpec((ATT_HB, LEFT + ATT_WIN, ATT_TQ), lambda h, b, q: (h, 0, 0)),
        ],
        out_specs=pl.BlockSpec((ATT_TQ, hw), lambda h, b, q: (b * nq + q, h)),
        out_shape=jax.ShapeDtypeStruct((m, d_att), BF16),
        scratch_shapes=[
            pltpu.VMEM((ATT_HB, seq // ATT_TQ, HEAD_DIM, ATT_TQ), BF16),
        ],
        compiler_params=pltpu.CompilerParams(
            dimension_semantics=("parallel", "parallel", "arbitrary"),
            vmem_limit_bytes=48 * MIB),
        name="chunk_attention",
    )(qkvg, qkvg, qkvg, qkvg, bias_tiles)


def _out_ln_kernel(y_ref, w_ref, r_ref, gain_ref, bias_ref, o_ref):
    for i in range(OUT_TM // OUT_SUB):
        rows = slice(i * OUT_SUB, (i + 1) * OUT_SUB)
        acc = jnp.dot(y_ref[rows, :], w_ref[...], preferred_element_type=F32)
        z = DEEPNORM_ALPHA * r_ref[rows, :] + acc
        mu = jnp.mean(z, axis=-1, keepdims=True)
        zc = z - mu
        var = jnp.mean(zc * zc, axis=-1, keepdims=True)
        o_ref[rows, :] = zc * lax.rsqrt(var + LN_EPS) * gain_ref[...] + bias_ref[...]


def _out_ln(y, w, resid, gain, bias, name):
    m, k = y.shape
    n = w.shape[1]
    const = lambda i: (0, 0)
    return pl.pallas_call(
        _out_ln_kernel,
        grid=(m // OUT_TM,),
        in_specs=[
            pl.BlockSpec((OUT_TM, k), lambda i: (i, 0)),
            pl.BlockSpec((k, n), const, pipeline_mode=pl.Buffered(1)),
            pl.BlockSpec((OUT_TM, n), lambda i: (i, 0)),
            pl.BlockSpec((1, n), const),
            pl.BlockSpec((1, n), const),
        ],
        out_specs=pl.BlockSpec((OUT_TM, n), lambda i: (i, 0)),
        out_shape=jax.ShapeDtypeStruct((m, n), F32),
        compiler_params=pltpu.CompilerParams(
            dimension_semantics=("parallel",),
            vmem_limit_bytes=48 * MIB),
        name=name,
    )(y, w, resid, gain, bias)


def _time_permutation():
    r = np.arange(LRU_TM)
    p = np.zeros((LRU_TM, LRU_TM), np.float32)
    p[r, LRU_SUB * (r % SUBLANES) + r // SUBLANES] = 1.0
    return p


def _sublane_scan(a, b):
    row = lax.broadcasted_iota(jnp.int32, a.shape, 0)
    d = 1
    while d < SUBLANES:
        keep = row >= d
        a_sh = jnp.where(keep, pltpu.roll(a, d, 0), 1.0)
        b_sh = jnp.where(keep, pltpu.roll(b, d, 0), 0.0)
        b = a * b_sh + b
        a = a * a_sh
        d *= 2
    return a, b


def _lru_kernel(h_ref, perm_ref, win_ref, cw_ref, cb_ref, wax_ref, bax_ref, lam_ref, y_ref,
                xbuf, ybuf, tailbuf, hcarry):
    t = pl.program_id(1)
    tm = h_ref.shape[0]
    hist = (CONV_W - 1) * SUBLANES

    @pl.when(t == 0)
    def _():
        tailbuf[...] = jnp.zeros_like(tailbuf)
        hcarry[...] = jnp.zeros_like(hcarry)

    x = h_ref[...].astype(BF16)
    xbuf[...] = jnp.dot(perm_ref[0], x, preferred_element_type=F32).astype(BF16)

    lam = lam_ref[...]
    softplus_neg_lam = jnp.maximum(-lam, 0.0) + jnp.log1p(jnp.exp(-jnp.abs(lam)))
    log2_a_scale = (-RG_C * LOG2_E) * softplus_neg_lam
    row0 = lax.broadcasted_iota(jnp.int32, (SUBLANES, LRU_BLOCK_W), 0) == 0

    def project(grp):
        cols = slice(grp * LRU_GROUP * 2 * LRU_BLOCK_W, (grp + 1) * LRU_GROUP * 2 * LRU_BLOCK_W)
        return jnp.dot(xbuf[...], win_ref[:, cols], preferred_element_type=F32)

    def recur(n, ug):
        cols = slice(n * LRU_BLOCK_W, (n + 1) * LRU_BLOCK_W)
        u_raw = ug[:, :LRU_BLOCK_W]
        g = ug[:, LRU_BLOCK_W:]
        cur = u_raw[tm - hist:, :]
        prev = tailbuf[:, cols]
        tailbuf[:, cols] = cur
        tiles = []
        for j in range(CONV_W - 1):
            rows = slice(j * SUBLANES, (j + 1) * SUBLANES)
            tiles.append(jnp.where(row0, pltpu.roll(prev[rows], 1, 0),
                                   pltpu.roll(cur[rows], 1, 0)))
        ext = jnp.concatenate(tiles + [u_raw], axis=0)
        u = cb_ref[:, cols]
        for tap in range(CONV_W):
            start = tap * SUBLANES
            u = u + ext[start:start + tm] * cw_ref[tap:tap + 1, cols]
        gates = jnp.dot(u.astype(BF16), wax_ref[n], preferred_element_type=F32)
        gates = jax.nn.sigmoid(gates + bax_ref[n])
        r = gates[:, :LRU_BLOCK_W]
        i = gates[:, LRU_BLOCK_W:]
        a = jnp.exp2(log2_a_scale[:, cols] * r)
        one_m_a2 = 1.0 - a * a
        mult = jnp.where(one_m_a2 > 0.0, one_m_a2 * lax.rsqrt(one_m_a2), 0.0)
        b = mult * (i * u)

        hs, ps = [b[:SUBLANES]], [a[:SUBLANES]]
        for k in range(1, LRU_SUB):
            rows = slice(k * SUBLANES, (k + 1) * SUBLANES)
            hs.append(a[rows] * hs[-1] + b[rows])
            ps.append(a[rows] * ps[-1])
        p_inc, h_inc = _sublane_scan(ps[-1], hs[-1])
        h0 = hcarry[:, cols]
        row = lax.broadcasted_iota(jnp.int32, p_inc.shape, 0)
        p_exc = jnp.where(row == 0, 1.0, pltpu.roll(p_inc, 1, 0))
        h_exc = jnp.where(row == 0, 0.0, pltpu.roll(h_inc, 1, 0))
        carry = p_exc * h0 + h_exc
        hcarry[:, cols] = (p_inc * h0 + h_inc)[SUBLANES - 1:SUBLANES, :]
        h = jnp.concatenate([hk + pk * carry for hk, pk in zip(hs, ps)], axis=0)
        ybuf[:, cols] = (h * (g * jax.nn.sigmoid(g))).astype(BF16)

    n_groups = LRU_BLOCKS // LRU_GROUP
    pending = [project(i) for i in range(LRU_AHEAD)]
    for grp in range(n_groups):
        ug = pending.pop(0)
        if grp + LRU_AHEAD < n_groups:
            pending.append(project(grp + LRU_AHEAD))
        for j in range(LRU_GROUP):
            recur(grp * LRU_GROUP + j, ug[:, j * 2 * LRU_BLOCK_W:(j + 1) * 2 * LRU_BLOCK_W])

    y = jnp.dot(perm_ref[1], ybuf[...], preferred_element_type=F32)
    y_ref[...] = y.astype(y_ref.dtype)


def _group_lru_in_proj(w_in):
    d, n2 = w_in.shape
    w = w_in.astype(BF16).reshape(d, 2, LRU_BLOCKS, LRU_BLOCK_W)
    return w.transpose(0, 2, 1, 3).reshape(d, n2)


def _lru(h, w_in, conv_w, conv_b, wax, bax, lam, batch, seq):
    m, d = h.shape
    d_rnn = lam.shape[1]
    nt = seq // LRU_TM
    perm = _time_permutation()
    perms = jnp.asarray(np.stack([perm, perm.T]), BF16)
    const = lambda b, t: (0, 0)
    const3 = lambda b, t: (0, 0, 0)
    hist = (CONV_W - 1) * SUBLANES
    return pl.pallas_call(
        _lru_kernel,
        grid=(batch, nt),
        in_specs=[
            pl.BlockSpec((LRU_TM, d), lambda b, t: (b * nt + t, 0)),
            pl.BlockSpec((2, LRU_TM, LRU_TM), const3),
            pl.BlockSpec((d, 2 * d_rnn), const, pipeline_mode=pl.Buffered(1)),
            pl.BlockSpec((CONV_W, d_rnn), const),
            pl.BlockSpec((1, d_rnn), const),
            pl.BlockSpec((LRU_BLOCKS, LRU_BLOCK_W, 2 * LRU_BLOCK_W), const3),
            pl.BlockSpec((LRU_BLOCKS, 1, 2 * LRU_BLOCK_W), const3),
            pl.BlockSpec((1, d_rnn), const),
        ],
        out_specs=pl.BlockSpec((LRU_TM, d_rnn), lambda b, t: (b * nt + t, 0)),
        out_shape=jax.ShapeDtypeStruct((m, d_rnn), BF16),
        scratch_shapes=[
            pltpu.VMEM((LRU_TM, d), BF16),
            pltpu.VMEM((LRU_TM, d_rnn), BF16),
            pltpu.VMEM((hist, d_rnn), F32),
            pltpu.VMEM((1, d_rnn), F32),
        ],
        compiler_params=pltpu.CompilerParams(
            dimension_semantics=("parallel", "arbitrary"),
            vmem_limit_bytes=56 * MIB),
        name="rglru",
    )(h, perms, w_in, conv_w, conv_b, wax, bax, lam)


def kernel(x, attn_w_in, attn_w_out, attn_rel_bias, lru_w_in, lru_conv_w, lru_conv_b,
           lru_wa, lru_ba, lru_wx, lru_bx, lru_lambda, lru_w_out, ln_gain, ln_bias):
    batch, seq, d_model = x.shape
    m = batch * seq
    d_att = ATT_HEADS * HEAD_DIM
    h = x.reshape(m, d_model)

    col_scale = jnp.concatenate([jnp.full((d_att,), Q_SCALE, F32),
                                 jnp.ones((3 * d_att,), F32)])[None, :]
    qkvg = _proj(h, attn_w_in[0].astype(BF16), col_scale)
    og = _attention(qkvg, _attention_bias_tiles(attn_rel_bias[0]), batch, seq)
    h = _out_ln(og, attn_w_out[0].astype(BF16), h, ln_gain[0][None, :], ln_bias[0][None, :],
                "attn_out_ln")

    wax = jnp.concatenate([lru_wa[0], lru_wx[0]], axis=-1).astype(BF16)
    bax = jnp.concatenate([lru_ba[0], lru_bx[0]], axis=-1)[:, None, :]
    y = _lru(h, _group_lru_in_proj(lru_w_in[0]), lru_conv_w[0], lru_conv_b[0][None, :], wax, bax,
             lru_lambda[0][None, :], batch, seq)
    h = _out_ln(y, lru_w_out[0].astype(BF16), h, ln_gain[1][None, :], ln_bias[1][None, :],
                "lru_out_ln")
    return h.reshape(batch, seq, d_model)
```

```python
import math

import jax
import jax.numpy as jnp
import numpy as np
from jax import lax
from jax.experimental import pallas as pl
from jax.experimental.pallas import tpu as pltpu

DEPTH = 2
CHUNK = 64
LEFT_CHUNKS = 8
LEFT = LEFT_CHUNKS * CHUNK
ATT_HEADS = 16
HEAD_DIM = 128
MAX_REL_DIST = 256
LRU_BLOCKS = 16
LRU_BLOCK_W = 128
CONV_W = 4
RG_C = 8.0
LN_EPS = 1e-5
NEG_INF = -1e30
DEEPNORM_ALPHA = (2.0 * DEPTH) ** 0.25
LOG2_E = math.log2(math.e)
Q_SCALE = HEAD_DIM ** -0.5 * LOG2_E

LANES = 128
SUBLANES = 8
MIB = 1024 * 1024

PROJ_TM = 1024
PROJ_TN = 1024
ATT_TQ = 256
ATT_WIN = LEFT + ATT_TQ
ATT_HB = 4
ATT_AHEAD = 2
OUT_TM = 512
OUT_SUB = 256
LRU_TM = 256
LRU_SUB = LRU_TM // SUBLANES
LRU_AHEAD = 2

F32 = jnp.float32
BF16 = jnp.bfloat16


def _proj_kernel(x_ref, w_ref, s_ref, o_ref):
    x = x_ref[...].astype(BF16)
    acc = jnp.dot(x, w_ref[...], preferred_element_type=F32)
    o_ref[...] = (acc * s_ref[...]).astype(o_ref.dtype)


def _proj(x, w, col_scale):
    m, k = x.shape
    n = w.shape[1]
    return pl.pallas_call(
        _proj_kernel,
        grid=(m // PROJ_TM, n // PROJ_TN),
        in_specs=[
            pl.BlockSpec((PROJ_TM, k), lambda i, j: (i, 0)),
            pl.BlockSpec((k, PROJ_TN), lambda i, j: (0, j)),
            pl.BlockSpec((1, PROJ_TN), lambda i, j: (0, j)),
        ],
        out_specs=pl.BlockSpec((PROJ_TM, PROJ_TN), lambda i, j: (i, j)),
        out_shape=jax.ShapeDtypeStruct((m, n), BF16),
        compiler_params=pltpu.CompilerParams(
            dimension_semantics=("parallel", "arbitrary"),
            vmem_limit_bytes=48 * MIB),
        name="attn_in_proj",
    )(x, w, col_scale)


def _attention_kernel(q_ref, k_ref, v_ref, g_ref, bias_ref, o_ref, vt_ref):
    qi = pl.program_id(2)
    n_kblk = ATT_WIN // ATT_TQ
    kblk = jnp.maximum(qi - LEFT // ATT_TQ, 0)
    ks = pl.multiple_of(kblk * ATT_TQ, ATT_TQ)
    boff = pl.multiple_of(LEFT - jnp.minimum(qi * ATT_TQ, LEFT), ATT_TQ)

    @pl.when(qi == 0)
    def _():
        for hh in range(ATT_HB):
            cols = slice(hh * HEAD_DIM, (hh + 1) * HEAD_DIM)
            for c in range(v_ref.shape[0] // ATT_TQ):
                blk = v_ref[c * ATT_TQ:(c + 1) * ATT_TQ, cols].astype(F32)
                vt_ref[hh, c] = blk.T.astype(BF16)

    def head_cols(hh):
        return slice(hh * HEAD_DIM, (hh + 1) * HEAD_DIM)

    def scores(hh):
        q = q_ref[:, head_cols(hh)]
        sts = []
        for c in range(n_kblk):
            k = k_ref[pl.ds(ks + c * ATT_TQ, ATT_TQ), head_cols(hh)]
            st = lax.dot_general(k, q, (((1,), (1,)), ((), ())), preferred_element_type=F32)
            sts.append(st + bias_ref[hh, pl.ds(boff + c * ATT_TQ, ATT_TQ), :])
        return sts

    def attend(hh, sts):
        m = jnp.max(sts[0], axis=0, keepdims=True)
        for st in sts[1:]:
            m = jnp.maximum(m, jnp.max(st, axis=0, keepdims=True))
        l, ot = None, None
        for c, st in enumerate(sts):
            p = jnp.exp2(st - m)
            lc = jnp.sum(p, axis=0, keepdims=True)
            oc = jnp.dot(vt_ref[hh, kblk + c], p.astype(BF16), preferred_element_type=F32)
            l = lc if l is None else l + lc
            ot = oc if ot is None else ot + oc
        return ot, l

    def finish(hh, ot, l):
        o = (ot / l).T
        g = g_ref[:, head_cols(hh)].astype(F32)
        o_ref[:, head_cols(hh)] = (o * (g * jax.nn.sigmoid(g))).astype(o_ref.dtype)

    pending = [scores(hh) for hh in range(ATT_AHEAD)]
    for hh in range(ATT_HB):
        sts = pending.pop(0)
        if hh + ATT_AHEAD < ATT_HB:
            pending.append(scores(hh + ATT_AHEAD))
        finish(hh, *attend(hh, sts))


def _attention_bias_tiles(rel_table):
    n_heads = rel_table.shape[0]
    n_rel = LEFT + ATT_WIN
    i = np.arange(ATT_TQ)[:, None]
    j = np.arange(n_rel)[None, :] - LEFT
    span = ATT_TQ + n_rel - 1
    edge = (span - (2 * MAX_REL_DIST + 1)) // 2
    r = jnp.pad(rel_table.astype(F32), ((0, 0), (edge, edge)), mode="edge")[:, ::-1]
    sq = math.isqrt(ATT_TQ)
    wide = n_rel + ATT_TQ - sq
    fine = jnp.stack([r[:, sq - 1 - t:sq - 1 - t + wide] for t in range(sq)], axis=1)
    b = jnp.stack([fine[:, :, ATT_TQ - sq * (p + 1):ATT_TQ - sq * (p + 1) + n_rel]
                   for p in range(ATT_TQ // sq)], axis=1)
    b = b.reshape(n_heads, ATT_TQ, n_rel)
    dchunk = i // CHUNK - j // CHUNK
    allowed = (dchunk >= 0) & (dchunk <= LEFT_CHUNKS)
    b = jnp.where(jnp.asarray(allowed)[None], b * LOG2_E, NEG_INF)
    return b.transpose(0, 2, 1)


def _attention(qkvg, bias_tiles, batch, seq):
    m = qkvg.shape[0]
    d_att = ATT_HEADS * HEAD_DIM
    hw = ATT_HB * HEAD_DIM
    ncol = d_att // hw
    nq = seq // ATT_TQ
    return pl.pallas_call(
        _attention_kernel,
        grid=(ATT_HEADS // ATT_HB, batch, nq),
        in_specs=[
            pl.BlockSpec((ATT_TQ, hw), lambda h, b, q: (b * nq + q, h)),
            pl.BlockSpec((seq, hw), lambda h, b, q: (b, ncol + h)),
            pl.BlockSpec((seq, hw), lambda h, b, q: (b, 2 * ncol + h)),
            pl.BlockSpec((ATT_TQ, hw), lambda h, b, q: (b * nq + q, 3 * ncol + h)),
            pl.BlockSpec((ATT_HB, LEFT + ATT_WIN, ATT_TQ), lambda h, b, q: (h, 0, 0)),
        ],
        out_specs=pl.BlockSpec((ATT_TQ, hw), lambda h, b, q: (b * nq + q, h)),
        out_shape=jax.ShapeDtypeStruct((m, d_att), BF16),
        scratch_shapes=[
            pltpu.VMEM((ATT_HB, seq // ATT_TQ, HEAD_DIM, ATT_TQ), BF16),
        ],
        compiler_params=pltpu.CompilerParams(
            dimension_semantics=("parallel", "parallel", "arbitrary"),
            vmem_limit_bytes=48 * MIB),
        name="chunk_attention",
    )(qkvg, qkvg, qkvg, qkvg, bias_tiles)


def _out_ln_kernel(y_ref, w_ref, r_ref, gain_ref, bias_ref, o_ref):
    for i in range(OUT_TM // OUT_SUB):
        rows = slice(i * OUT_SUB, (i + 1) * OUT_SUB)
        acc = jnp.dot(y_ref[rows, :], w_ref[...], preferred_element_type=F32)
        z = DEEPNORM_ALPHA * r_ref[rows, :] + acc
        mu = jnp.mean(z, axis=-1, keepdims=True)
        zc = z - mu
        var = jnp.mean(zc * zc, axis=-1, keepdims=True)
        o_ref[rows, :] = zc * lax.rsqrt(var + LN_EPS) * gain_ref[...] + bias_ref[...]


def _out_ln(y, w, resid, gain, bias, name):
    m, k = y.shape
    n = w.shape[1]
    const = lambda i: (0, 0)
    return pl.pallas_call(
        _out_ln_kernel,
        grid=(m // OUT_TM,),
        in_specs=[
            pl.BlockSpec((OUT_TM, k), lambda i: (i, 0)),
            pl.BlockSpec((k, n), const, pipeline_mode=pl.Buffered(1)),
            pl.BlockSpec((OUT_TM, n), lambda i: (i, 0)),
            pl.BlockSpec((1, n), const),
            pl.BlockSpec((1, n), const),
        ],
        out_specs=pl.BlockSpec((OUT_TM, n), lambda i: (i, 0)),
        out_shape=jax.ShapeDtypeStruct((m, n), F32),
        compiler_params=pltpu.CompilerParams(
            dimension_semantics=("parallel",),
            vmem_limit_bytes=48 * MIB),
        name=name,
    )(y, w, resid, gain, bias)


def _time_permutation():
    r = np.arange(LRU_TM)
    p = np.zeros((LRU_TM, LRU_TM), np.float32)
    p[r, LRU_SUB * (r % SUBLANES) + r // SUBLANES] = 1.0
    return p


def _sublane_scan(a, b):
    row = lax.broadcasted_iota(jnp.int32, a.shape, 0)
    d = 1
    while d < SUBLANES:
        keep = row >= d
        a_sh = jnp.where(keep, pltpu.roll(a, d, 0), 1.0)
        b_sh = jnp.where(keep, pltpu.roll(b, d, 0), 0.0)
        b = a * b_sh + b
        a = a * a_sh
        d *= 2
    return a, b


def _lru_kernel(h_ref, perm_ref, win_ref, cw_ref, cb_ref, wax_ref, bax_ref, lam_ref, y_ref,
                xbuf, ybuf, tailbuf, hcarry):
    t = pl.program_id(1)
    tm = h_ref.shape[0]
    hist = (CONV_W - 1) * SUBLANES

    @pl.when(t == 0)
    def _():
        tailbuf[...] = jnp.zeros_like(tailbuf)
        hcarry[...] = jnp.zeros_like(hcarry)

    x = h_ref[...].astype(BF16)
    xbuf[...] = jnp.dot(perm_ref[0], x, preferred_element_type=F32).astype(BF16)

    lam = lam_ref[...]
    softplus_neg_lam = jnp.maximum(-lam, 0.0) + jnp.log1p(jnp.exp(-jnp.abs(lam)))
    log2_a_scale = (-RG_C * LOG2_E) * softplus_neg_lam
    row0 = lax.broadcasted_iota(jnp.int32, (SUBLANES, LRU_BLOCK_W), 0) == 0

    d_rnn = y_ref.shape[1]

    def project(n):
        w = jnp.concatenate(
            [win_ref[:, n * LRU_BLOCK_W:(n + 1) * LRU_BLOCK_W],
             win_ref[:, d_rnn + n * LRU_BLOCK_W:d_rnn + (n + 1) * LRU_BLOCK_W]], axis=1)
        return jnp.dot(xbuf[...], w, preferred_element_type=F32)

    def recur(n, ug):
        cols = slice(n * LRU_BLOCK_W, (n + 1) * LRU_BLOCK_W)
        u_raw = ug[:, :LRU_BLOCK_W]
        g = ug[:, LRU_BLOCK_W:]
        cur = u_raw[tm - hist:, :]
        prev = tailbuf[:, cols]
        tailbuf[:, cols] = cur
        tiles = []
        for j in range(CONV_W - 1):
            rows = slice(j * SUBLANES, (j + 1) * SUBLANES)
            tiles.append(jnp.where(row0, pltpu.roll(prev[rows], 1, 0),
                                   pltpu.roll(cur[rows], 1, 0)))
        ext = jnp.concatenate(tiles + [u_raw], axis=0)
        u = cb_ref[:, cols]
        for tap in range(CONV_W):
            start = tap * SUBLANES
            u = u + ext[start:start + tm] * cw_ref[tap:tap + 1, cols]
        gates = jnp.dot(u.astype(BF16), wax_ref[n], preferred_element_type=F32)
        gates = jax.nn.sigmoid(gates + bax_ref[n])
        r = gates[:, :LRU_BLOCK_W]
        i = gates[:, LRU_BLOCK_W:]
        a = jnp.exp2(log2_a_scale[:, cols] * r)
        one_m_a2 = 1.0 - a * a
        mult = jnp.where(one_m_a2 > 0.0, one_m_a2 * lax.rsqrt(one_m_a2), 0.0)
        b = mult * (i * u)

        hs, ps = [b[:SUBLANES]], [a[:SUBLANES]]
        for k in range(1, LRU_SUB):
            rows = slice(k * SUBLANES, (k + 1) * SUBLANES)
            hs.append(a[rows] * hs[-1] + b[rows])
            ps.append(a[rows] * ps[-1])
        p_inc, h_inc = _sublane_scan(ps[-1], hs[-1])
        h0 = hcarry[:, cols]
        row = lax.broadcasted_iota(jnp.int32, p_inc.shape, 0)
        p_exc = jnp.where(row == 0, 1.0, pltpu.roll(p_inc, 1, 0))
        h_exc = jnp.where(row == 0, 0.0, pltpu.roll(h_inc, 1, 0))
        carry = p_exc * h0 + h_exc
        hcarry[:, cols] = (p_inc * h0 + h_inc)[SUBLANES - 1:SUBLANES, :]
        h = jnp.concatenate([hk + pk * carry for hk, pk in zip(hs, ps)], axis=0)
        ybuf[:, cols] = (h * (g * jax.nn.sigmoid(g))).astype(BF16)

    pending = [project(n) for n in range(LRU_AHEAD)]
    for n in range(LRU_BLOCKS):
        ug = pending.pop(0)
        if n + LRU_AHEAD < LRU_BLOCKS:
            pending.append(project(n + LRU_AHEAD))
        recur(n, ug)

    y = jnp.dot(perm_ref[1], ybuf[...], preferred_element_type=F32)
    y_ref[...] = y.astype(y_ref.dtype)


def _lru(h, w_in, conv_w, conv_b, wax, bax, lam, batch, seq):
    m, d = h.shape
    d_rnn = lam.shape[1]
    nt = seq // LRU_TM
    perm = _time_permutation()
    perms = jnp.asarray(np.stack([perm, perm.T]), BF16)
    const = lambda b, t: (0, 0)
    const3 = lambda b, t: (0, 0, 0)
    hist = (CONV_W - 1) * SUBLANES
    return pl.pallas_call(
        _lru_kernel,
        grid=(batch, nt),
        in_specs=[
            pl.BlockSpec((LRU_TM, d), lambda b, t: (b * nt + t, 0)),
            pl.BlockSpec((2, LRU_TM, LRU_TM), const3),
            pl.BlockSpec((d, 2 * d_rnn), const, pipeline_mode=pl.Buffered(1)),
            pl.BlockSpec((CONV_W, d_rnn), const),
            pl.BlockSpec((1, d_rnn), const),
            pl.BlockSpec((LRU_BLOCKS, LRU_BLOCK_W, 2 * LRU_BLOCK_W), const3),
            pl.BlockSpec((LRU_BLOCKS, 1, 2 * LRU_BLOCK_W), const3),
            pl.BlockSpec((1, d_rnn), const),
        ],
        out_specs=pl.BlockSpec((LRU_TM, d_rnn), lambda b, t: (b * nt + t, 0)),
        out_shape=jax.ShapeDtypeStruct((m, d_rnn), BF16),
        scratch_shapes=[
            pltpu.VMEM((LRU_TM, d), BF16),
            pltpu.VMEM((LRU_TM, d_rnn), BF16),
            pltpu.VMEM((hist, d_rnn), F32),
            pltpu.VMEM((1, d_rnn), F32),
        ],
        compiler_params=pltpu.CompilerParams(
            dimension_semantics=("parallel", "arbitrary"),
            vmem_limit_bytes=56 * MIB),
        name="rglru",
    )(h, perms, w_in, conv_w, conv_b, wax, bax, lam)


def kernel(x, attn_w_in, attn_w_out, attn_rel_bias, lru_w_in, lru_conv_w, lru_conv_b,
           lru_wa, lru_ba, lru_wx, lru_bx, lru_lambda, lru_w_out, ln_gain, ln_bias):
    batch, seq, d_model = x.shape
    m = batch * seq
    d_att = ATT_HEADS * HEAD_DIM
    h = x.reshape(m, d_model)

    col_scale = jnp.concatenate([jnp.full((d_att,), Q_SCALE, F32),
                                 jnp.ones((3 * d_att,), F32)])[None, :]
    qkvg = _proj(h, attn_w_in[0].astype(BF16), col_scale)
    og = _attention(qkvg, _attention_bias_tiles(attn_rel_bias[0]), batch, seq)
    h = _out_ln(og, attn_w_out[0].astype(BF16), h, ln_gain[0][None, :], ln_bias[0][None, :],
                "attn_out_ln")

    wax = jnp.concatenate([lru_wa[0], lru_wx[0]], axis=-1).astype(BF16)
    bax = jnp.concatenate([lru_ba[0], lru_bx[0]], axis=-1)[:, None, :]
    y = _lru(h, lru_w_in[0].astype(BF16), lru_conv_w[0], lru_conv_b[0][None, :], wax, bax,
             lru_lambda[0][None, :], batch, seq)
    h = _out_ln(y, lru_w_out[0].astype(BF16), h, ln_gain[1][None, :], ln_bias[1][None, :],
                "lru_out_ln")
    return h.reshape(batch, seq, d_model)
```

```python
import math

import jax
import jax.numpy as jnp
import numpy as np
from jax import lax
from jax.experimental import pallas as pl
from jax.experimental.pallas import tpu as pltpu

DEPTH = 2
CHUNK = 64
LEFT_CHUNKS = 8
LEFT = LEFT_CHUNKS * CHUNK
ATT_HEADS = 16
HEAD_DIM = 128
MAX_REL_DIST = 256
LRU_BLOCKS = 16
LRU_BLOCK_W = 128
CONV_W = 4
RG_C = 8.0
LN_EPS = 1e-5
NEG_INF = -1e30
DEEPNORM_ALPHA = (2.0 * DEPTH) ** 0.25
LOG2_E = math.log2(math.e)
Q_SCALE = HEAD_DIM ** -0.5 * LOG2_E

LANES = 128
SUBLANES = 8
MIB = 1024 * 1024

PROJ_TM = 1024
PROJ_TN = 1024
ATT_TQ = 256
ATT_WIN = LEFT + ATT_TQ
ATT_HB = 8
ATT_AHEAD = 2
ATT_ONES_ROWS = 16
OUT_TM = 512
OUT_SUB = 256
LRU_TM = 256
LRU_SUB = LRU_TM // SUBLANES
LRU_AHEAD = 2

F32 = jnp.float32
BF16 = jnp.bfloat16


def _proj_kernel(x_ref, w_ref, s_ref, o_ref):
    x = x_ref[...].astype(BF16)
    acc = jnp.dot(x, w_ref[...], preferred_element_type=F32)
    o_ref[...] = (acc * s_ref[...]).astype(o_ref.dtype)


def _proj(x, w, col_scale):
    m, k = x.shape
    n = w.shape[1]
    return pl.pallas_call(
        _proj_kernel,
        grid=(m // PROJ_TM, n // PROJ_TN),
        in_specs=[
            pl.BlockSpec((PROJ_TM, k), lambda i, j: (i, 0)),
            pl.BlockSpec((k, PROJ_TN), lambda i, j: (0, j)),
            pl.BlockSpec((1, PROJ_TN), lambda i, j: (0, j)),
        ],
        out_specs=pl.BlockSpec((PROJ_TM, PROJ_TN), lambda i, j: (i, j)),
        out_shape=jax.ShapeDtypeStruct((m, n), BF16),
        compiler_params=pltpu.CompilerParams(
            dimension_semantics=("parallel", "arbitrary"),
            vmem_limit_bytes=48 * MIB),
        name="attn_in_proj",
    )(x, w, col_scale)


def _attention_kernel(q_ref, k_ref, v_ref, g_ref, bias_ref, o_ref, vt_ref):
    qi = pl.program_id(2)
    n_kblk = ATT_WIN // ATT_TQ
    kblk = jnp.maximum(qi - LEFT // ATT_TQ, 0)
    ks = pl.multiple_of(kblk * ATT_TQ, ATT_TQ)
    boff = pl.multiple_of(LEFT - jnp.minimum(qi * ATT_TQ, LEFT), ATT_TQ)

    @pl.when(qi == 0)
    def _():
        ones = jnp.ones((ATT_ONES_ROWS, ATT_TQ), BF16)
        for hh in range(ATT_HB):
            cols = slice(hh * HEAD_DIM, (hh + 1) * HEAD_DIM)
            for c in range(v_ref.shape[0] // ATT_TQ):
                blk = v_ref[c * ATT_TQ:(c + 1) * ATT_TQ, cols].astype(F32)
                vt_ref[hh, c, :HEAD_DIM, :] = blk.T.astype(BF16)
                vt_ref[hh, c, HEAD_DIM:, :] = ones

    def head_cols(hh):
        return slice(hh * HEAD_DIM, (hh + 1) * HEAD_DIM)

    def scores(hh):
        q = q_ref[:, head_cols(hh)]
        sts = []
        for c in range(n_kblk):
            k = k_ref[pl.ds(ks + c * ATT_TQ, ATT_TQ), head_cols(hh)]
            st = lax.dot_general(k, q, (((1,), (1,)), ((), ())), preferred_element_type=F32)
            sts.append(st + bias_ref[hh, pl.ds(boff + c * ATT_TQ, ATT_TQ), :])
        return sts

    def attend(hh, sts):
        m = jnp.max(sts[0], axis=0, keepdims=True)
        for st in sts[1:]:
            m = jnp.maximum(m, jnp.max(st, axis=0, keepdims=True))
        ot = None
        for c, st in enumerate(sts):
            p = jnp.exp2(st - m).astype(BF16)
            oc = jnp.dot(vt_ref[hh, kblk + c], p, preferred_element_type=F32)
            ot = oc if ot is None else ot + oc
        return ot

    def finish(hh, ot):
        o = (ot[:HEAD_DIM] / ot[HEAD_DIM:HEAD_DIM + 1]).T
        g = g_ref[:, head_cols(hh)].astype(F32)
        o_ref[:, head_cols(hh)] = (o * (g * jax.nn.sigmoid(g))).astype(o_ref.dtype)

    pending = [scores(hh) for hh in range(ATT_AHEAD)]
    for hh in range(ATT_HB):
        sts = pending.pop(0)
        if hh + ATT_AHEAD < ATT_HB:
            pending.append(scores(hh + ATT_AHEAD))
        finish(hh, attend(hh, sts))


def _attention_bias_tiles(rel_table):
    n_heads = rel_table.shape[0]
    n_rel = LEFT + ATT_WIN
    i = np.arange(ATT_TQ)[:, None]
    j = np.arange(n_rel)[None, :] - LEFT
    span = ATT_TQ + n_rel - 1
    edge = (span - (2 * MAX_REL_DIST + 1)) // 2
    r = jnp.pad(rel_table.astype(F32), ((0, 0), (edge, edge)), mode="edge")[:, ::-1]
    sq = math.isqrt(ATT_TQ)
    wide = n_rel + ATT_TQ - sq
    fine = jnp.stack([r[:, sq - 1 - t:sq - 1 - t + wide] for t in range(sq)], axis=1)
    b = jnp.stack([fine[:, :, ATT_TQ - sq * (p + 1):ATT_TQ - sq * (p + 1) + n_rel]
                   for p in range(ATT_TQ // sq)], axis=1)
    b = b.reshape(n_heads, ATT_TQ, n_rel)
    dchunk = i // CHUNK - j // CHUNK
    allowed = (dchunk >= 0) & (dchunk <= LEFT_CHUNKS)
    b = jnp.where(jnp.asarray(allowed)[None], b * LOG2_E, NEG_INF)
    return b.transpose(0, 2, 1)


def _attention(qkvg, bias_tiles, batch, seq):
    m = qkvg.shape[0]
    d_att = ATT_HEADS * HEAD_DIM
    hw = ATT_HB * HEAD_DIM
    ncol = d_att // hw
    nq = seq // ATT_TQ
    return pl.pallas_call(
        _attention_kernel,
        grid=(ATT_HEADS // ATT_HB, batch, nq),
        in_specs=[
            pl.BlockSpec((ATT_TQ, hw), lambda h, b, q: (b * nq + q, h)),
            pl.BlockSpec((seq, hw), lambda h, b, q: (b, ncol + h)),
            pl.BlockSpec((seq, hw), lambda h, b, q: (b, 2 * ncol + h)),
            pl.BlockSpec((ATT_TQ, hw), lambda h, b, q: (b * nq + q, 3 * ncol + h)),
            pl.BlockSpec((ATT_HB, LEFT + ATT_WIN, ATT_TQ), lambda h, b, q: (h, 0, 0)),
        ],
        out_specs=pl.BlockSpec((ATT_TQ, hw), lambda h, b, q: (b * nq + q, h)),
        out_shape=jax.ShapeDtypeStruct((m, d_att), BF16),
        scratch_shapes=[
            pltpu.VMEM((ATT_HB, seq // ATT_TQ, HEAD_DIM + ATT_ONES_ROWS, ATT_TQ), BF16),
        ],
        compiler_params=pltpu.CompilerParams(
            dimension_semantics=("parallel", "parallel", "arbitrary"),
            vmem_limit_bytes=56 * MIB),
        name="chunk_attention",
    )(qkvg, qkvg, qkvg, qkvg, bias_tiles)


def _out_ln_kernel(y_ref, w_ref, r_ref, gain_ref, bias_ref, o_ref):
    for i in range(OUT_TM // OUT_SUB):
        rows = slice(i * OUT_SUB, (i + 1) * OUT_SUB)
        acc = jnp.dot(y_ref[rows, :], w_ref[...], preferred_element_type=F32)
        z = DEEPNORM_ALPHA * r_ref[rows, :] + acc
        mu = jnp.mean(z, axis=-1, keepdims=True)
        zc = z - mu
        var = jnp.mean(zc * zc, axis=-1, keepdims=True)
        o_ref[rows, :] = zc * lax.rsqrt(var + LN_EPS) * gain_ref[...] + bias_ref[...]


def _out_ln(y, w, resid, gain, bias, name):
    m, k = y.shape
    n = w.shape[1]
    const = lambda i: (0, 0)
    return pl.pallas_call(
        _out_ln_kernel,
        grid=(m // OUT_TM,),
        in_specs=[
            pl.BlockSpec((OUT_TM, k), lambda i: (i, 0)),
            pl.BlockSpec((k, n), const, pipeline_mode=pl.Buffered(1)),
            pl.BlockSpec((OUT_TM, n), lambda i: (i, 0)),
            pl.BlockSpec((1, n), const),
            pl.BlockSpec((1, n), const),
        ],
        out_specs=pl.BlockSpec((OUT_TM, n), lambda i: (i, 0)),
        out_shape=jax.ShapeDtypeStruct((m, n), F32),
        compiler_params=pltpu.CompilerParams(
            dimension_semantics=("parallel",),
            vmem_limit_bytes=48 * MIB),
        name=name,
    )(y, w, resid, gain, bias)


def _time_permutation():
    r = np.arange(LRU_TM)
    p = np.zeros((LRU_TM, LRU_TM), np.float32)
    p[r, LRU_SUB * (r % SUBLANES) + r // SUBLANES] = 1.0
    return p


def _sublane_scan(a, b):
    row = lax.broadcasted_iota(jnp.int32, a.shape, 0)
    d = 1
    while d < SUBLANES:
        keep = row >= d
        a_sh = jnp.where(keep, pltpu.roll(a, d, 0), 1.0)
        b_sh = jnp.where(keep, pltpu.roll(b, d, 0), 0.0)
        b = a * b_sh + b
        a = a * a_sh
        d *= 2
    return a, b


def _lru_kernel(h_ref, perm_ref, win_ref, cw_ref, cb_ref, wax_ref, bax_ref, lam_ref, y_ref,
                xbuf, ybuf, tailbuf, hcarry):
    t = pl.program_id(1)
    tm = h_ref.shape[0]
    hist = (CONV_W - 1) * SUBLANES

    @pl.when(t == 0)
    def _():
        tailbuf[...] = jnp.zeros_like(tailbuf)
        hcarry[...] = jnp.zeros_like(hcarry)

    x = h_ref[...].astype(BF16)
    xbuf[...] = jnp.dot(perm_ref[0], x, preferred_element_type=F32).astype(BF16)

    lam = lam_ref[...]
    softplus_neg_lam = jnp.maximum(-lam, 0.0) + jnp.log1p(jnp.exp(-jnp.abs(lam)))
    log2_a_scale = (-RG_C * LOG2_E) * softplus_neg_lam
    row0 = lax.broadcasted_iota(jnp.int32, (SUBLANES, LRU_BLOCK_W), 0) == 0

    d_rnn = y_ref.shape[1]

    def project(n):
        w = jnp.concatenate(
            [win_ref[:, n * LRU_BLOCK_W:(n + 1) * LRU_BLOCK_W],
             win_ref[:, d_rnn + n * LRU_BLOCK_W:d_rnn + (n + 1) * LRU_BLOCK_W]], axis=1)
        return jnp.dot(xbuf[...], w, preferred_element_type=F32)

    def recur(n, ug):
        cols = slice(n * LRU_BLOCK_W, (n + 1) * LRU_BLOCK_W)
        u_raw = ug[:, :LRU_BLOCK_W]
        g = ug[:, LRU_BLOCK_W:]
        cur = u_raw[tm - hist:, :]
        prev = tailbuf[:, cols]
        tailbuf[:, cols] = cur
        tiles = []
        for j in range(CONV_W - 1):
            rows = slice(j * SUBLANES, (j + 1) * SUBLANES)
            tiles.append(jnp.where(row0, pltpu.roll(prev[rows], 1, 0),
                                   pltpu.roll(cur[rows], 1, 0)))
        ext = jnp.concatenate(tiles + [u_raw], axis=0)
        u = cb_ref[:, cols]
        for tap in range(CONV_W):
            start = tap * SUBLANES
            u = u + ext[start:start + tm] * cw_ref[tap:tap + 1, cols]
        gates = jnp.dot(u.astype(BF16), wax_ref[n], preferred_element_type=F32)
        gates = jax.nn.sigmoid(gates + bax_ref[n])
        r = gates[:, :LRU_BLOCK_W]
        i = gates[:, LRU_BLOCK_W:]
        a = jnp.exp2(log2_a_scale[:, cols] * r)
        one_m_a2 = 1.0 - a * a
        mult = jnp.where(one_m_a2 > 0.0, one_m_a2 * lax.rsqrt(one_m_a2), 0.0)
        b = mult * (i * u)

        hs, ps = [b[:SUBLANES]], [a[:SUBLANES]]
        for k in range(1, LRU_SUB):
            rows = slice(k * SUBLANES, (k + 1) * SUBLANES)
            hs.append(a[rows] * hs[-1] + b[rows])
            ps.append(a[rows] * ps[-1])
        p_inc, h_inc = _sublane_scan(ps[-1], hs[-1])
        h0 = hcarry[:, cols]
        row = lax.broadcasted_iota(jnp.int32, p_inc.shape, 0)
        p_exc = jnp.where(row == 0, 1.0, pltpu.roll(p_inc, 1, 0))
        h_exc = jnp.where(row == 0, 0.0, pltpu.roll(h_inc, 1, 0))
        carry = p_exc * h0 + h_exc
        hcarry[:, cols] = (p_inc * h0 + h_inc)[SUBLANES - 1:SUBLANES, :]
        h = jnp.concatenate([hk + pk * carry for hk, pk in zip(hs, ps)], axis=0)
        ybuf[:, cols] = (h * (g * jax.nn.sigmoid(g))).astype(BF16)

    pending = [project(n) for n in range(LRU_AHEAD)]
    for n in range(LRU_BLOCKS):
        ug = pending.pop(0)
        if n + LRU_AHEAD < LRU_BLOCKS:
            pending.append(project(n + LRU_AHEAD))
        recur(n, ug)

    y = jnp.dot(perm_ref[1], ybuf[...], preferred_element_type=F32)
    y_ref[...] = y.astype(y_ref.dtype)


def _lru(h, w_in, conv_w, conv_b, wax, bax, lam, batch, seq):
    m, d = h.shape
    d_rnn = lam.shape[1]
    nt = seq // LRU_TM
    perm = _time_permutation()
    perms = jnp.asarray(np.stack([perm, perm.T]), BF16)
    const = lambda b, t: (0, 0)
    const3 = lambda b, t: (0, 0, 0)
    hist = (CONV_W - 1) * SUBLANES
    return pl.pallas_call(
        _lru_kernel,
        grid=(batch, nt),
        in_specs=[
            pl.BlockSpec((LRU_TM, d), lambda b, t: (b * nt + t, 0)),
            pl.BlockSpec((2, LRU_TM, LRU_TM), const3),
            pl.BlockSpec((d, 2 * d_rnn), const, pipeline_mode=pl.Buffered(1)),
            pl.BlockSpec((CONV_W, d_rnn), const),
            pl.BlockSpec((1, d_rnn), const),
            pl.BlockSpec((LRU_BLOCKS, LRU_BLOCK_W, 2 * LRU_BLOCK_W), const3),
            pl.BlockSpec((LRU_BLOCKS, 1, 2 * LRU_BLOCK_W), const3),
            pl.BlockSpec((1, d_rnn), const),
        ],
        out_specs=pl.BlockSpec((LRU_TM, d_rnn), lambda b, t: (b * nt + t, 0)),
        out_shape=jax.ShapeDtypeStruct((m, d_rnn), BF16),
        scratch_shapes=[
            pltpu.VMEM((LRU_TM, d), BF16),
            pltpu.VMEM((LRU_TM, d_rnn), BF16),
            pltpu.VMEM((hist, d_rnn), F32),
            pltpu.VMEM((1, d_rnn), F32),
        ],
        compiler_params=pltpu.CompilerParams(
            dimension_semantics=("parallel", "arbitrary"),
            vmem_limit_bytes=56 * MIB),
        name="rglru",
    )(h, perms, w_in, conv_w, conv_b, wax, bax, lam)


def kernel(x, attn_w_in, attn_w_out, attn_rel_bias, lru_w_in, lru_conv_w, lru_conv_b,
           lru_wa, lru_ba, lru_wx, lru_bx, lru_lambda, lru_w_out, ln_gain, ln_bias):
    batch, seq, d_model = x.shape
    m = batch * seq
    d_att = ATT_HEADS * HEAD_DIM
    h = x.reshape(m, d_model)

    col_scale = jnp.concatenate([jnp.full((d_att,), Q_SCALE, F32),
                                 jnp.ones((3 * d_att,), F32)])[None, :]
    qkvg = _proj(h, attn_w_in[0].astype(BF16), col_scale)
    og = _attention(qkvg, _attention_bias_tiles(attn_rel_bias[0]), batch, seq)
    h = _out_ln(og, attn_w_out[0].astype(BF16), h, ln_gain[0][None, :], ln_bias[0][None, :],
                "attn_out_ln")

    wax = jnp.concatenate([lru_wa[0], lru_wx[0]], axis=-1).astype(BF16)
    bax = jnp.concatenate([lru_ba[0], lru_bx[0]], axis=-1)[:, None, :]
    y = _lru(h, lru_w_in[0].astype(BF16), lru_conv_w[0], lru_conv_b[0][None, :], wax, bax,
             lru_lambda[0][None, :], batch, seq)
    h = _out_ln(y, lru_w_out[0].astype(BF16), h, ln_gain[1][None, :], ln_bias[1][None, :],
                "lru_out_ln")
    return h.reshape(batch, seq, d_model)
```

```python
import math

import jax
import jax.numpy as jnp
import numpy as np
from jax import lax
from jax.experimental import pallas as pl
from jax.experimental.pallas import tpu as pltpu

DEPTH = 2
CHUNK = 64
LEFT_CHUNKS = 8
LEFT = LEFT_CHUNKS * CHUNK
ATT_HEADS = 16
HEAD_DIM = 128
MAX_REL_DIST = 256
LRU_BLOCKS = 16
LRU_BLOCK_W = 128
CONV_W = 4
RG_C = 8.0
LN_EPS = 1e-5
NEG_INF = -1e30
DEEPNORM_ALPHA = (2.0 * DEPTH) ** 0.25
LOG2_E = math.log2(math.e)
Q_SCALE = HEAD_DIM ** -0.5 * LOG2_E

LANES = 128
SUBLANES = 8
MIB = 1024 * 1024

PROJ_TM = 1024
PROJ_TN = 1024
ATT_TQ = 256
ATT_WIN = LEFT + ATT_TQ
ATT_HB = 8
ATT_AHEAD = 2
ATT_ONES_ROWS = 16
ATT_PIECE = 128
OUT_TM = 512
OUT_SUB = 256
LRU_TM = 256
LRU_SUB = LRU_TM // SUBLANES
LRU_AHEAD = 2

F32 = jnp.float32
BF16 = jnp.bfloat16


def _proj_kernel(x_ref, w_ref, s_ref, o_ref):
    x = x_ref[...].astype(BF16)
    acc = jnp.dot(x, w_ref[...], preferred_element_type=F32)
    o_ref[...] = (acc * s_ref[...]).astype(o_ref.dtype)


def _proj(x, w, col_scale):
    m, k = x.shape
    n = w.shape[1]
    return pl.pallas_call(
        _proj_kernel,
        grid=(m // PROJ_TM, n // PROJ_TN),
        in_specs=[
            pl.BlockSpec((PROJ_TM, k), lambda i, j: (i, 0)),
            pl.BlockSpec((k, PROJ_TN), lambda i, j: (0, j)),
            pl.BlockSpec((1, PROJ_TN), lambda i, j: (0, j)),
        ],
        out_specs=pl.BlockSpec((PROJ_TM, PROJ_TN), lambda i, j: (i, j)),
        out_shape=jax.ShapeDtypeStruct((m, n), BF16),
        compiler_params=pltpu.CompilerParams(
            dimension_semantics=("parallel", "arbitrary"),
            vmem_limit_bytes=48 * MIB),
        name="attn_in_proj",
    )(x, w, col_scale)


def _band_allowed(i, j):
    dchunk = i // CHUNK - j // CHUNK
    return (dchunk >= 0) & (dchunk <= LEFT_CHUNKS)


def _live_pieces():
    i = np.arange(ATT_TQ)[None, :]
    live = []
    for c in range(ATT_WIN // ATT_TQ):
        j = np.arange(ATT_TQ)[:, None] + c * ATT_TQ - LEFT
        ok = _band_allowed(i, j)
        n_half = ATT_TQ // ATT_PIECE
        live.append([(r, l) for r in range(n_half) for l in range(n_half)
                     if ok[r * ATT_PIECE:(r + 1) * ATT_PIECE,
                           l * ATT_PIECE:(l + 1) * ATT_PIECE].any()])
    return live


_LIVE_PIECES = _live_pieces()


def _attention_kernel(q_ref, k_ref, v_ref, g_ref, bias_ref, o_ref, vt_ref):
    qi = pl.program_id(2)
    n_kblk = ATT_WIN // ATT_TQ
    key_blk, key_row, bias_row = [], [], []
    for c in range(n_kblk):
        rel = qi - LEFT // ATT_TQ + c
        blk = jnp.maximum(rel, 0)
        key_blk.append(blk)
        key_row.append(pl.multiple_of(blk * ATT_TQ, ATT_TQ))
        bias_row.append(pl.multiple_of(
            jnp.where(rel >= 0, c * ATT_TQ, LEFT + ATT_WIN - ATT_TQ), ATT_TQ))

    @pl.when(qi == 0)
    def _():
        ones = jnp.ones((ATT_ONES_ROWS, ATT_TQ), BF16)
        for hh in range(ATT_HB):
            cols = slice(hh * HEAD_DIM, (hh + 1) * HEAD_DIM)
            for c in range(v_ref.shape[0] // ATT_TQ):
                blk = v_ref[c * ATT_TQ:(c + 1) * ATT_TQ, cols].astype(F32)
                vt_ref[hh, c, :HEAD_DIM, :] = blk.T.astype(BF16)
                vt_ref[hh, c, HEAD_DIM:, :] = ones

    def head_cols(hh):
        return slice(hh * HEAD_DIM, (hh + 1) * HEAD_DIM)

    def scores(hh):
        q = q_ref[:, head_cols(hh)]
        pieces = {}
        for c in range(n_kblk):
            k = k_ref[pl.ds(key_row[c], ATT_TQ), head_cols(hh)]
            st = lax.dot_general(k, q, (((1,), (1,)), ((), ())), preferred_element_type=F32)
            for r, l in _LIVE_PIECES[c]:
                rows = slice(r * ATT_PIECE, (r + 1) * ATT_PIECE)
                lanes = slice(l * ATT_PIECE, (l + 1) * ATT_PIECE)
                bias = bias_ref[hh, pl.ds(bias_row[c] + r * ATT_PIECE, ATT_PIECE), lanes]
                pieces[c, r, l] = st[rows, lanes] + bias
        return pieces

    def attend(hh, pieces):
        n_half = ATT_TQ // ATT_PIECE
        probs = {}
        for l in range(n_half):
            live = [key for key in pieces if key[2] == l]
            m = jnp.max(pieces[live[0]], axis=0, keepdims=True)
            for key in live[1:]:
                m = jnp.maximum(m, jnp.max(pieces[key], axis=0, keepdims=True))
            for key in live:
                probs[key] = jnp.exp2(pieces[key] - m).astype(BF16)
        zero = jnp.zeros((ATT_PIECE, ATT_PIECE), BF16)
        ot = None
        for c in range(n_kblk):
            p = jnp.concatenate(
                [jnp.concatenate([probs.get((c, r, l), zero) for l in range(n_half)], axis=1)
                 for r in range(n_half)], axis=0)
            oc = jnp.dot(vt_ref[hh, key_blk[c]], p, preferred_element_type=F32)
            ot = oc if ot is None else ot + oc
        return ot

    def finish(hh, ot):
        o = (ot[:HEAD_DIM] / ot[HEAD_DIM:HEAD_DIM + 1]).T
        g = g_ref[:, head_cols(hh)].astype(F32)
        o_ref[:, head_cols(hh)] = (o * (g * jax.nn.sigmoid(g))).astype(o_ref.dtype)

    pending = [scores(hh) for hh in range(ATT_AHEAD)]
    for hh in range(ATT_HB):
        sts = pending.pop(0)
        if hh + ATT_AHEAD < ATT_HB:
            pending.append(scores(hh + ATT_AHEAD))
        finish(hh, attend(hh, sts))


def _attention_bias_tiles(rel_table):
    n_heads = rel_table.shape[0]
    n_rel = LEFT + ATT_WIN
    i = np.arange(ATT_TQ)[:, None]
    j = np.arange(n_rel)[None, :] - LEFT
    span = ATT_TQ + n_rel - 1
    edge = (span - (2 * MAX_REL_DIST + 1)) // 2
    r = jnp.pad(rel_table.astype(F32), ((0, 0), (edge, edge)), mode="edge")[:, ::-1]
    sq = math.isqrt(ATT_TQ)
    wide = n_rel + ATT_TQ - sq
    fine = jnp.stack([r[:, sq - 1 - t:sq - 1 - t + wide] for t in range(sq)], axis=1)
    b = jnp.stack([fine[:, :, ATT_TQ - sq * (p + 1):ATT_TQ - sq * (p + 1) + n_rel]
                   for p in range(ATT_TQ // sq)], axis=1)
    b = b.reshape(n_heads, ATT_TQ, n_rel)
    b = jnp.where(jnp.asarray(_band_allowed(i, j))[None], b * LOG2_E, NEG_INF)
    return b.transpose(0, 2, 1)


def _attention(qkvg, bias_tiles, batch, seq):
    m = qkvg.shape[0]
    d_att = ATT_HEADS * HEAD_DIM
    hw = ATT_HB * HEAD_DIM
    ncol = d_att // hw
    nq = seq // ATT_TQ
    return pl.pallas_call(
        _attention_kernel,
        grid=(ATT_HEADS // ATT_HB, batch, nq),
        in_specs=[
            pl.BlockSpec((ATT_TQ, hw), lambda h, b, q: (b * nq + q, h)),
            pl.BlockSpec((seq, hw), lambda h, b, q: (b, ncol + h)),
            pl.BlockSpec((seq, hw), lambda h, b, q: (b, 2 * ncol + h)),
            pl.BlockSpec((ATT_TQ, hw), lambda h, b, q: (b * nq + q, 3 * ncol + h)),
            pl.BlockSpec((ATT_HB, LEFT + ATT_WIN, ATT_TQ), lambda h, b, q: (h, 0, 0)),
        ],
        out_specs=pl.BlockSpec((ATT_TQ, hw), lambda h, b, q: (b * nq + q, h)),
        out_shape=jax.ShapeDtypeStruct((m, d_att), BF16),
        scratch_shapes=[
            pltpu.VMEM((ATT_HB, seq // ATT_TQ, HEAD_DIM + ATT_ONES_ROWS, ATT_TQ), BF16),
        ],
        compiler_params=pltpu.CompilerParams(
            dimension_semantics=("parallel", "parallel", "arbitrary"),
            vmem_limit_bytes=56 * MIB),
        name="chunk_attention",
    )(qkvg, qkvg, qkvg, qkvg, bias_tiles)


def _out_ln_kernel(y_ref, w_ref, r_ref, gain_ref, bias_ref, o_ref):
    for i in range(OUT_TM // OUT_SUB):
        rows = slice(i * OUT_SUB, (i + 1) * OUT_SUB)
        acc = jnp.dot(y_ref[rows, :], w_ref[...], preferred_element_type=F32)
        z = DEEPNORM_ALPHA * r_ref[rows, :] + acc
        mu = jnp.mean(z, axis=-1, keepdims=True)
        zc = z - mu
        var = jnp.mean(zc * zc, axis=-1, keepdims=True)
        o_ref[rows, :] = zc * lax.rsqrt(var + LN_EPS) * gain_ref[...] + bias_ref[...]


def _out_ln(y, w, resid, gain, bias, name):
    m, k = y.shape
    n = w.shape[1]
    const = lambda i: (0, 0)
    return pl.pallas_call(
        _out_ln_kernel,
        grid=(m // OUT_TM,),
        in_specs=[
            pl.BlockSpec((OUT_TM, k), lambda i: (i, 0)),
            pl.BlockSpec((k, n), const, pipeline_mode=pl.Buffered(1)),
            pl.BlockSpec((OUT_TM, n), lambda i: (i, 0)),
            pl.BlockSpec((1, n), const),
            pl.BlockSpec((1, n), const),
        ],
        out_specs=pl.BlockSpec((OUT_TM, n), lambda i: (i, 0)),
        out_shape=jax.ShapeDtypeStruct((m, n), F32),
        compiler_params=pltpu.CompilerParams(
            dimension_semantics=("parallel",),
            vmem_limit_bytes=48 * MIB),
        name=name,
    )(y, w, resid, gain, bias)


def _time_permutation():
    r = np.arange(LRU_TM)
    p = np.zeros((LRU_TM, LRU_TM), np.float32)
    p[r, LRU_SUB * (r % SUBLANES) + r // SUBLANES] = 1.0
    return p


def _sublane_scan(a, b):
    row = lax.broadcasted_iota(jnp.int32, a.shape, 0)
    d = 1
    while d < SUBLANES:
        keep = row >= d
        a_sh = jnp.where(keep, pltpu.roll(a, d, 0), 1.0)
        b_sh = jnp.where(keep, pltpu.roll(b, d, 0), 0.0)
        b = a * b_sh + b
        a = a * a_sh
        d *= 2
    return a, b


def _lru_kernel(h_ref, perm_ref, win_ref, cw_ref, cb_ref, wax_ref, bax_ref, lam_ref, y_ref,
                xbuf, ybuf, tailbuf, hcarry):
    t = pl.program_id(1)
    tm = h_ref.shape[0]
    hist = (CONV_W - 1) * SUBLANES

    @pl.when(t == 0)
    def _():
        tailbuf[...] = jnp.zeros_like(tailbuf)
        hcarry[...] = jnp.zeros_like(hcarry)

    x = h_ref[...].astype(BF16)
    xbuf[...] = jnp.dot(perm_ref[0], x, preferred_element_type=F32).astype(BF16)

    lam = lam_ref[...]
    softplus_neg_lam = jnp.maximum(-lam, 0.0) + jnp.log1p(jnp.exp(-jnp.abs(lam)))
    log2_a_scale = (-RG_C * LOG2_E) * softplus_neg_lam
    row0 = lax.broadcasted_iota(jnp.int32, (SUBLANES, LRU_BLOCK_W), 0) == 0

    d_rnn = y_ref.shape[1]

    def project(n):
        w = jnp.concatenate(
            [win_ref[:, n * LRU_BLOCK_W:(n + 1) * LRU_BLOCK_W],
             win_ref[:, d_rnn + n * LRU_BLOCK_W:d_rnn + (n + 1) * LRU_BLOCK_W]], axis=1)
        return jnp.dot(xbuf[...], w, preferred_element_type=F32)

    def recur(n, ug):
        cols = slice(n * LRU_BLOCK_W, (n + 1) * LRU_BLOCK_W)
        u_raw = ug[:, :LRU_BLOCK_W]
        g = ug[:, LRU_BLOCK_W:]
        cur = u_raw[tm - hist:, :]
        prev = tailbuf[:, cols]
        tailbuf[:, cols] = cur
        tiles = []
        for j in range(CONV_W - 1):
            rows = slice(j * SUBLANES, (j + 1) * SUBLANES)
            tiles.append(jnp.where(row0, pltpu.roll(prev[rows], 1, 0),
                                   pltpu.roll(cur[rows], 1, 0)))
        ext = jnp.concatenate(tiles + [u_raw], axis=0)
        u = cb_ref[:, cols]
        for tap in range(CONV_W):
            start = tap * SUBLANES
            u = u + ext[start:start + tm] * cw_ref[tap:tap + 1, cols]
        gates = jnp.dot(u.astype(BF16), wax_ref[n], preferred_element_type=F32)
        gates = jax.nn.sigmoid(gates + bax_ref[n])
        r = gates[:, :LRU_BLOCK_W]
        i = gates[:, LRU_BLOCK_W:]
        a = jnp.exp2(log2_a_scale[:, cols] * r)
        one_m_a2 = 1.0 - a * a
        mult = jnp.where(one_m_a2 > 0.0, one_m_a2 * lax.rsqrt(one_m_a2), 0.0)
        b = mult * (i * u)

        hs, ps = [b[:SUBLANES]], [a[:SUBLANES]]
        for k in range(1, LRU_SUB):
            rows = slice(k * SUBLANES, (k + 1) * SUBLANES)
            hs.append(a[rows] * hs[-1] + b[rows])
            ps.append(a[rows] * ps[-1])
        p_inc, h_inc = _sublane_scan(ps[-1], hs[-1])
        h0 = hcarry[:, cols]
        row = lax.broadcasted_iota(jnp.int32, p_inc.shape, 0)
        p_exc = jnp.where(row == 0, 1.0, pltpu.roll(p_inc, 1, 0))
        h_exc = jnp.where(row == 0, 0.0, pltpu.roll(h_inc, 1, 0))
        carry = p_exc * h0 + h_exc
        hcarry[:, cols] = (p_inc * h0 + h_inc)[SUBLANES - 1:SUBLANES, :]
        h = jnp.concatenate([hk + pk * carry for hk, pk in zip(hs, ps)], axis=0)
        ybuf[:, cols] = (h * (g * jax.nn.sigmoid(g))).astype(BF16)

    pending = [project(n) for n in range(LRU_AHEAD)]
    for n in range(LRU_BLOCKS):
        ug = pending.pop(0)
        if n + LRU_AHEAD < LRU_BLOCKS:
            pending.append(project(n + LRU_AHEAD))
        recur(n, ug)

    y = jnp.dot(perm_ref[1], ybuf[...], preferred_element_type=F32)
    y_ref[...] = y.astype(y_ref.dtype)


def _lru(h, w_in, conv_w, conv_b, wax, bax, lam, batch, seq):
    m, d = h.shape
    d_rnn = lam.shape[1]
    nt = seq // LRU_TM
    perm = _time_permutation()
    perms = jnp.asarray(np.stack([perm, perm.T]), BF16)
    const = lambda b, t: (0, 0)
    const3 = lambda b, t: (0, 0, 0)
    hist = (CONV_W - 1) * SUBLANES
    return pl.pallas_call(
        _lru_kernel,
        grid=(batch, nt),
        in_specs=[
            pl.BlockSpec((LRU_TM, d), lambda b, t: (b * nt + t, 0)),
            pl.BlockSpec((2, LRU_TM, LRU_TM), const3),
            pl.BlockSpec((d, 2 * d_rnn), const, pipeline_mode=pl.Buffered(1)),
            pl.BlockSpec((CONV_W, d_rnn), const),
            pl.BlockSpec((1, d_rnn), const),
            pl.BlockSpec((LRU_BLOCKS, LRU_BLOCK_W, 2 * LRU_BLOCK_W), const3),
            pl.BlockSpec((LRU_BLOCKS, 1, 2 * LRU_BLOCK_W), const3),
            pl.BlockSpec((1, d_rnn), const),
        ],
        out_specs=pl.BlockSpec((LRU_TM, d_rnn), lambda b, t: (b * nt + t, 0)),
        out_shape=jax.ShapeDtypeStruct((m, d_rnn), BF16),
        scratch_shapes=[
            pltpu.VMEM((LRU_TM, d), BF16),
            pltpu.VMEM((LRU_TM, d_rnn), BF16),
            pltpu.VMEM((hist, d_rnn), F32),
            pltpu.VMEM((1, d_rnn), F32),
        ],
        compiler_params=pltpu.CompilerParams(
            dimension_semantics=("parallel", "arbitrary"),
            vmem_limit_bytes=56 * MIB),
        name="rglru",
    )(h, perms, w_in, conv_w, conv_b, wax, bax, lam)


def kernel(x, attn_w_in, attn_w_out, attn_rel_bias, lru_w_in, lru_conv_w, lru_conv_b,
           lru_wa, lru_ba, lru_wx, lru_bx, lru_lambda, lru_w_out, ln_gain, ln_bias):
    batch, seq, d_model = x.shape
    m = batch * seq
    d_att = ATT_HEADS * HEAD_DIM
    h = x.reshape(m, d_model)

    col_scale = jnp.concatenate([jnp.full((d_att,), Q_SCALE, F32),
                                 jnp.ones((3 * d_att,), F32)])[None, :]
    qkvg = _proj(h, attn_w_in[0].astype(BF16), col_scale)
    og = _attention(qkvg, _attention_bias_tiles(attn_rel_bias[0]), batch, seq)
    h = _out_ln(og, attn_w_out[0].astype(BF16), h, ln_gain[0][None, :], ln_bias[0][None, :],
                "attn_out_ln")

    wax = jnp.concatenate([lru_wa[0], lru_wx[0]], axis=-1).astype(BF16)
    bax = jnp.concatenate([lru_ba[0], lru_bx[0]], axis=-1)[:, None, :]
    y = _lru(h, lru_w_in[0].astype(BF16), lru_conv_w[0], lru_conv_b[0][None, :], wax, bax,
             lru_lambda[0][None, :], batch, seq)
    h = _out_ln(y, lru_w_out[0].astype(BF16), h, ln_gain[1][None, :], ln_bias[1][None, :],
                "lru_out_ln")
    return h.reshape(batch, seq, d_model)
```

```python
import math

import jax
import jax.numpy as jnp
import numpy as np
from jax import lax
from jax.experimental import pallas as pl
from jax.experimental.pallas import tpu as pltpu

DEPTH = 2
CHUNK = 64
LEFT_CHUNKS = 8
LEFT = LEFT_CHUNKS * CHUNK
ATT_HEADS = 16
HEAD_DIM = 128
MAX_REL_DIST = 256
LRU_BLOCKS = 16
LRU_BLOCK_W = 128
CONV_W = 4
RG_C = 8.0
LN_EPS = 1e-5
NEG_INF = -1e30
DEEPNORM_ALPHA = (2.0 * DEPTH) ** 0.25
LOG2_E = math.log2(math.e)
Q_SCALE = HEAD_DIM ** -0.5 * LOG2_E

LANES = 128
SUBLANES = 8
MIB = 1024 * 1024

PROJ_TM = 1024
PROJ_TN = 1024
ATT_TQ = 256
ATT_WIN = LEFT + ATT_TQ
ATT_HB = 8
ATT_AHEAD = 2
ATT_ONES_ROWS = 16
ATT_PIECE = 128
OUT_TM = 512
OUT_SUB = 256
LRU_TM = 256
LRU_SUB = LRU_TM // SUBLANES
LRU_AHEAD = 2

F32 = jnp.float32
BF16 = jnp.bfloat16


def _proj_kernel(x_ref, w_ref, s_ref, o_ref):
    x = x_ref[...].astype(BF16)
    acc = jnp.dot(x, w_ref[...].astype(BF16), preferred_element_type=F32)
    o_ref[...] = (acc * s_ref[...]).astype(o_ref.dtype)


def _proj(x, w, col_scale):
    m, k = x.shape
    n = w.shape[1]
    return pl.pallas_call(
        _proj_kernel,
        grid=(m // PROJ_TM, n // PROJ_TN),
        in_specs=[
            pl.BlockSpec((PROJ_TM, k), lambda i, j: (i, 0)),
            pl.BlockSpec((k, PROJ_TN), lambda i, j: (0, j)),
            pl.BlockSpec((1, PROJ_TN), lambda i, j: (0, j)),
        ],
        out_specs=pl.BlockSpec((PROJ_TM, PROJ_TN), lambda i, j: (i, j)),
        out_shape=jax.ShapeDtypeStruct((m, n), BF16),
        compiler_params=pltpu.CompilerParams(
            dimension_semantics=("parallel", "arbitrary"),
            vmem_limit_bytes=56 * MIB),
        name="attn_in_proj",
    )(x, w, col_scale)


def _band_allowed(i, j):
    dchunk = i // CHUNK - j // CHUNK
    return (dchunk >= 0) & (dchunk <= LEFT_CHUNKS)


def _live_pieces():
    i = np.arange(ATT_TQ)[None, :]
    live = []
    for c in range(ATT_WIN // ATT_TQ):
        j = np.arange(ATT_TQ)[:, None] + c * ATT_TQ - LEFT
        ok = _band_allowed(i, j)
        n_half = ATT_TQ // ATT_PIECE
        live.append([(r, l) for r in range(n_half) for l in range(n_half)
                     if ok[r * ATT_PIECE:(r + 1) * ATT_PIECE,
                           l * ATT_PIECE:(l + 1) * ATT_PIECE].any()])
    return live


_LIVE_PIECES = _live_pieces()


def _attention_kernel(q_ref, k_ref, v_ref, g_ref, bias_ref, o_ref, vt_ref):
    qi = pl.program_id(2)
    n_kblk = ATT_WIN // ATT_TQ
    key_blk, key_row, bias_row = [], [], []
    for c in range(n_kblk):
        rel = qi - LEFT // ATT_TQ + c
        blk = jnp.maximum(rel, 0)
        key_blk.append(blk)
        key_row.append(pl.multiple_of(blk * ATT_TQ, ATT_TQ))
        bias_row.append(pl.multiple_of(
            jnp.where(rel >= 0, c * ATT_TQ, LEFT + ATT_WIN - ATT_TQ), ATT_TQ))

    @pl.when(qi == 0)
    def _():
        ones = jnp.ones((ATT_ONES_ROWS, ATT_TQ), BF16)
        for hh in range(ATT_HB):
            cols = slice(hh * HEAD_DIM, (hh + 1) * HEAD_DIM)
            for c in range(v_ref.shape[0] // ATT_TQ):
                vt_ref[hh, c, :HEAD_DIM, :] = v_ref[c * ATT_TQ:(c + 1) * ATT_TQ, cols].T
                vt_ref[hh, c, HEAD_DIM:, :] = ones

    def head_cols(hh):
        return slice(hh * HEAD_DIM, (hh + 1) * HEAD_DIM)

    def scores(hh):
        q = q_ref[:, head_cols(hh)]
        pieces = {}
        for c in range(n_kblk):
            k = k_ref[pl.ds(key_row[c], ATT_TQ), head_cols(hh)]
            st = lax.dot_general(k, q, (((1,), (1,)), ((), ())), preferred_element_type=F32)
            for r, l in _LIVE_PIECES[c]:
                rows = slice(r * ATT_PIECE, (r + 1) * ATT_PIECE)
                lanes = slice(l * ATT_PIECE, (l + 1) * ATT_PIECE)
                bias = bias_ref[hh, pl.ds(bias_row[c] + r * ATT_PIECE, ATT_PIECE), lanes]
                pieces[c, r, l] = st[rows, lanes] + bias
        return pieces

    def attend(hh, pieces):
        n_half = ATT_TQ // ATT_PIECE
        probs = {}
        for l in range(n_half):
            live = [key for key in pieces if key[2] == l]
            m = jnp.max(pieces[live[0]], axis=0, keepdims=True)
            for key in live[1:]:
                m = jnp.maximum(m, jnp.max(pieces[key], axis=0, keepdims=True))
            for key in live:
                probs[key] = jnp.exp2(pieces[key] - m).astype(BF16)
        zero = jnp.zeros((ATT_PIECE, ATT_PIECE), BF16)
        ot = None
        for c in range(n_kblk):
            p = jnp.concatenate(
                [jnp.concatenate([probs.get((c, r, l), zero) for l in range(n_half)], axis=1)
                 for r in range(n_half)], axis=0)
            oc = jnp.dot(vt_ref[hh, key_blk[c]], p, preferred_element_type=F32)
            ot = oc if ot is None else ot + oc
        return ot

    def finish(hh, ot):
        o = (ot[:HEAD_DIM] / ot[HEAD_DIM:HEAD_DIM + 1]).T
        g = g_ref[:, head_cols(hh)].astype(F32)
        o_ref[:, head_cols(hh)] = (o * (g * jax.nn.sigmoid(g))).astype(o_ref.dtype)

    pending = [scores(hh) for hh in range(ATT_AHEAD)]
    for hh in range(ATT_HB):
        sts = pending.pop(0)
        if hh + ATT_AHEAD < ATT_HB:
            pending.append(scores(hh + ATT_AHEAD))
        finish(hh, attend(hh, sts))


def _attention_bias_tiles(rel_table):
    n_heads = rel_table.shape[0]
    n_rel = LEFT + ATT_WIN
    i = np.arange(ATT_TQ)[:, None]
    j = np.arange(n_rel)[None, :] - LEFT
    span = ATT_TQ + n_rel - 1
    edge = (span - (2 * MAX_REL_DIST + 1)) // 2
    r = jnp.pad(rel_table.astype(F32), ((0, 0), (edge, edge)), mode="edge")[:, ::-1]
    sq = math.isqrt(ATT_TQ)
    wide = n_rel + ATT_TQ - sq
    fine = jnp.stack([r[:, sq - 1 - t:sq - 1 - t + wide] for t in range(sq)], axis=1)
    b = jnp.stack([fine[:, :, ATT_TQ - sq * (p + 1):ATT_TQ - sq * (p + 1) + n_rel]
                   for p in range(ATT_TQ // sq)], axis=1)
    b = b.reshape(n_heads, ATT_TQ, n_rel)
    b = jnp.where(jnp.asarray(_band_allowed(i, j))[None], b * LOG2_E, NEG_INF)
    return b.transpose(0, 2, 1)


def _attention(qkvg, bias_tiles, batch, seq):
    m = qkvg.shape[0]
    d_att = ATT_HEADS * HEAD_DIM
    hw = ATT_HB * HEAD_DIM
    ncol = d_att // hw
    nq = seq // ATT_TQ
    return pl.pallas_call(
        _attention_kernel,
        grid=(ATT_HEADS // ATT_HB, batch, nq),
        in_specs=[
            pl.BlockSpec((ATT_TQ, hw), lambda h, b, q: (b * nq + q, h)),
            pl.BlockSpec((seq, hw), lambda h, b, q: (b, ncol + h)),
            pl.BlockSpec((seq, hw), lambda h, b, q: (b, 2 * ncol + h)),
            pl.BlockSpec((ATT_TQ, hw), lambda h, b, q: (b * nq + q, 3 * ncol + h)),
            pl.BlockSpec((ATT_HB, LEFT + ATT_WIN, ATT_TQ), lambda h, b, q: (h, 0, 0)),
        ],
        out_specs=pl.BlockSpec((ATT_TQ, hw), lambda h, b, q: (b * nq + q, h)),
        out_shape=jax.ShapeDtypeStruct((m, d_att), BF16),
        scratch_shapes=[
            pltpu.VMEM((ATT_HB, seq // ATT_TQ, HEAD_DIM + ATT_ONES_ROWS, ATT_TQ), BF16),
        ],
        compiler_params=pltpu.CompilerParams(
            dimension_semantics=("parallel", "parallel", "arbitrary"),
            vmem_limit_bytes=56 * MIB),
        name="chunk_attention",
    )(qkvg, qkvg, qkvg, qkvg, bias_tiles)


def _out_ln_kernel(y_ref, w_ref, r_ref, gain_ref, bias_ref, o_ref):
    for i in range(OUT_TM // OUT_SUB):
        rows = slice(i * OUT_SUB, (i + 1) * OUT_SUB)
        acc = jnp.dot(y_ref[rows, :], w_ref[...], preferred_element_type=F32)
        z = DEEPNORM_ALPHA * r_ref[rows, :] + acc
        mu = jnp.mean(z, axis=-1, keepdims=True)
        zc = z - mu
        var = jnp.mean(zc * zc, axis=-1, keepdims=True)
        o_ref[rows, :] = zc * lax.rsqrt(var + LN_EPS) * gain_ref[...] + bias_ref[...]


def _out_ln(y, w, resid, gain, bias, name):
    m, k = y.shape
    n = w.shape[1]
    const = lambda i: (0, 0)
    return pl.pallas_call(
        _out_ln_kernel,
        grid=(m // OUT_TM,),
        in_specs=[
            pl.BlockSpec((OUT_TM, k), lambda i: (i, 0)),
            pl.BlockSpec((k, n), const, pipeline_mode=pl.Buffered(1)),
            pl.BlockSpec((OUT_TM, n), lambda i: (i, 0)),
            pl.BlockSpec((1, n), const),
            pl.BlockSpec((1, n), const),
        ],
        out_specs=pl.BlockSpec((OUT_TM, n), lambda i: (i, 0)),
        out_shape=jax.ShapeDtypeStruct((m, n), F32),
        compiler_params=pltpu.CompilerParams(
            dimension_semantics=("parallel",),
            vmem_limit_bytes=48 * MIB),
        name=name,
    )(y, w, resid, gain, bias)


def _time_permutation():
    r = np.arange(LRU_TM)
    p = np.zeros((LRU_TM, LRU_TM), np.float32)
    p[r, LRU_SUB * (r % SUBLANES) + r // SUBLANES] = 1.0
    return p


def _sublane_scan(a, b):
    row = lax.broadcasted_iota(jnp.int32, a.shape, 0)
    d = 1
    while d < SUBLANES:
        keep = row >= d
        a_sh = jnp.where(keep, pltpu.roll(a, d, 0), 1.0)
        b_sh = jnp.where(keep, pltpu.roll(b, d, 0), 0.0)
        b = a * b_sh + b
        a = a * a_sh
        d *= 2
    return a, b


def _lru_kernel(h_ref, perm_ref, win_ref, cw_ref, cb_ref, wax_ref, bax_ref, lam_ref, y_ref,
                xbuf, ybuf, tailbuf, hcarry):
    t = pl.program_id(1)
    tm = h_ref.shape[0]
    hist = (CONV_W - 1) * SUBLANES

    @pl.when(t == 0)
    def _():
        tailbuf[...] = jnp.zeros_like(tailbuf)
        hcarry[...] = jnp.zeros_like(hcarry)

    x = h_ref[...].astype(BF16)
    xbuf[...] = jnp.dot(perm_ref[0], x, preferred_element_type=F32).astype(BF16)

    lam = lam_ref[...]
    softplus_neg_lam = jnp.maximum(-lam, 0.0) + jnp.log1p(jnp.exp(-jnp.abs(lam)))
    log2_a_scale = (-RG_C * LOG2_E) * softplus_neg_lam
    row0 = lax.broadcasted_iota(jnp.int32, (SUBLANES, LRU_BLOCK_W), 0) == 0

    d_rnn = y_ref.shape[1]

    def project(n):
        w = jnp.concatenate(
            [win_ref[:, n * LRU_BLOCK_W:(n + 1) * LRU_BLOCK_W],
             win_ref[:, d_rnn + n * LRU_BLOCK_W:d_rnn + (n + 1) * LRU_BLOCK_W]], axis=1)
        return jnp.dot(xbuf[...], w, preferred_element_type=F32)

    def recur(n, ug):
        cols = slice(n * LRU_BLOCK_W, (n + 1) * LRU_BLOCK_W)
        u_raw = ug[:, :LRU_BLOCK_W]
        g = ug[:, LRU_BLOCK_W:]
        cur = u_raw[tm - hist:, :]
        prev = tailbuf[:, cols]
        tailbuf[:, cols] = cur
        tiles = []
        for j in range(CONV_W - 1):
            rows = slice(j * SUBLANES, (j + 1) * SUBLANES)
            tiles.append(jnp.where(row0, pltpu.roll(prev[rows], 1, 0),
                                   pltpu.roll(cur[rows], 1, 0)))
        ext = jnp.concatenate(tiles + [u_raw], axis=0)
        u = cb_ref[:, cols]
        for tap in range(CONV_W):
            start = tap * SUBLANES
            u = u + ext[start:start + tm] * cw_ref[tap:tap + 1, cols]
        gates = jnp.dot(u.astype(BF16), wax_ref[n], preferred_element_type=F32)
        gates = jax.nn.sigmoid(gates + bax_ref[n])
        r = gates[:, :LRU_BLOCK_W]
        i = gates[:, LRU_BLOCK_W:]
        a = jnp.exp2(log2_a_scale[:, cols] * r)
        one_m_a2 = 1.0 - a * a
        mult = jnp.where(one_m_a2 > 0.0, one_m_a2 * lax.rsqrt(one_m_a2), 0.0)
        b = mult * (i * u)

        hs, ps = [b[:SUBLANES]], [a[:SUBLANES]]
        for k in range(1, LRU_SUB):
            rows = slice(k * SUBLANES, (k + 1) * SUBLANES)
            hs.append(a[rows] * hs[-1] + b[rows])
            ps.append(a[rows] * ps[-1])
        p_inc, h_inc = _sublane_scan(ps[-1], hs[-1])
        h0 = hcarry[:, cols]
        row = lax.broadcasted_iota(jnp.int32, p_inc.shape, 0)
        p_exc = jnp.where(row == 0, 1.0, pltpu.roll(p_inc, 1, 0))
        h_exc = jnp.where(row == 0, 0.0, pltpu.roll(h_inc, 1, 0))
        carry = p_exc * h0 + h_exc
        hcarry[:, cols] = (p_inc * h0 + h_inc)[SUBLANES - 1:SUBLANES, :]
        h = jnp.concatenate([hk + pk * carry for hk, pk in zip(hs, ps)], axis=0)
        ybuf[:, cols] = (h * (g * jax.nn.sigmoid(g))).astype(BF16)

    pending = [project(n) for n in range(LRU_AHEAD)]
    for n in range(LRU_BLOCKS):
        ug = pending.pop(0)
        if n + LRU_AHEAD < LRU_BLOCKS:
            pending.append(project(n + LRU_AHEAD))
        recur(n, ug)

    y = jnp.dot(perm_ref[1], ybuf[...], preferred_element_type=F32)
    y_ref[...] = y.astype(y_ref.dtype)


def _lru(h, w_in, conv_w, conv_b, wax, bax, lam, batch, seq):
    m, d = h.shape
    d_rnn = lam.shape[1]
    nt = seq // LRU_TM
    perm = _time_permutation()
    perms = jnp.asarray(np.stack([perm, perm.T]), BF16)
    const = lambda b, t: (0, 0)
    const3 = lambda b, t: (0, 0, 0)
    hist = (CONV_W - 1) * SUBLANES
    return pl.pallas_call(
        _lru_kernel,
        grid=(batch, nt),
        in_specs=[
            pl.BlockSpec((LRU_TM, d), lambda b, t: (b * nt + t, 0)),
            pl.BlockSpec((2, LRU_TM, LRU_TM), const3),
            pl.BlockSpec((d, 2 * d_rnn), const, pipeline_mode=pl.Buffered(1)),
            pl.BlockSpec((CONV_W, d_rnn), const),
            pl.BlockSpec((1, d_rnn), const),
            pl.BlockSpec((LRU_BLOCKS, LRU_BLOCK_W, 2 * LRU_BLOCK_W), const3),
            pl.BlockSpec((LRU_BLOCKS, 1, 2 * LRU_BLOCK_W), const3),
            pl.BlockSpec((1, d_rnn), const),
        ],
        out_specs=pl.BlockSpec((LRU_TM, d_rnn), lambda b, t: (b * nt + t, 0)),
        out_shape=jax.ShapeDtypeStruct((m, d_rnn), BF16),
        scratch_shapes=[
            pltpu.VMEM((LRU_TM, d), BF16),
            pltpu.VMEM((LRU_TM, d_rnn), BF16),
            pltpu.VMEM((hist, d_rnn), F32),
            pltpu.VMEM((1, d_rnn), F32),
        ],
        compiler_params=pltpu.CompilerParams(
            dimension_semantics=("parallel", "arbitrary"),
            vmem_limit_bytes=56 * MIB),
        name="rglru",
    )(h, perms, w_in, conv_w, conv_b, wax, bax, lam)


def kernel(x, attn_w_in, attn_w_out, attn_rel_bias, lru_w_in, lru_conv_w, lru_conv_b,
           lru_wa, lru_ba, lru_wx, lru_bx, lru_lambda, lru_w_out, ln_gain, ln_bias):
    batch, seq, d_model = x.shape
    m = batch * seq
    d_att = ATT_HEADS * HEAD_DIM
    h = x.reshape(m, d_model)

    col_scale = jnp.concatenate([jnp.full((d_att,), Q_SCALE, F32),
                                 jnp.ones((3 * d_att,), F32)])[None, :]
    qkvg = _proj(h, attn_w_in[0], col_scale)
    og = _attention(qkvg, _attention_bias_tiles(attn_rel_bias[0]), batch, seq)
    h = _out_ln(og, attn_w_out[0].astype(BF16), h, ln_gain[0][None, :], ln_bias[0][None, :],
                "attn_out_ln")

    wax = jnp.concatenate([lru_wa[0], lru_wx[0]], axis=-1).astype(BF16)
    bax = jnp.concatenate([lru_ba[0], lru_bx[0]], axis=-1)[:, None, :]
    y = _lru(h, lru_w_in[0].astype(BF16), lru_conv_w[0], lru_conv_b[0][None, :], wax, bax,
             lru_lambda[0][None, :], batch, seq)
    h = _out_ln(y, lru_w_out[0].astype(BF16), h, ln_gain[1][None, :], ln_bias[1][None, :],
                "lru_out_ln")
    return h.reshape(batch, seq, d_model)
```

```python
import math

import jax
import jax.numpy as jnp
import numpy as np
from jax import lax
from jax.experimental import pallas as pl
from jax.experimental.pallas import tpu as pltpu

DEPTH = 2
CHUNK = 64
LEFT_CHUNKS = 8
LEFT = LEFT_CHUNKS * CHUNK
ATT_HEADS = 16
HEAD_DIM = 128
MAX_REL_DIST = 256
LRU_BLOCKS = 16
LRU_BLOCK_W = 128
CONV_W = 4
RG_C = 8.0
LN_EPS = 1e-5
NEG_INF = -1e30
DEEPNORM_ALPHA = (2.0 * DEPTH) ** 0.25
LOG2_E = math.log2(math.e)
Q_SCALE = HEAD_DIM ** -0.5 * LOG2_E

LANES = 128
SUBLANES = 8
MIB = 1024 * 1024

PROJ_TM = 1024
PROJ_TN = 2048
ATT_TQ = 256
ATT_WIN = LEFT + ATT_TQ
ATT_HB = 8
ATT_AHEAD = 2
ATT_ONES_ROWS = 16
ATT_PIECE = 128
OUT_TM = 512
OUT_SUB = 256
LRU_TM = 256
LRU_SUB = LRU_TM // SUBLANES
LRU_AHEAD = 2

F32 = jnp.float32
BF16 = jnp.bfloat16


def _proj_kernel(x_ref, w_ref, s_ref, o_ref):
    x = x_ref[...].astype(BF16)
    acc = jnp.dot(x, w_ref[...], preferred_element_type=F32)
    o_ref[...] = (acc * s_ref[...]).astype(o_ref.dtype)


def _proj(x, w, col_scale):
    m, k = x.shape
    n = w.shape[1]
    return pl.pallas_call(
        _proj_kernel,
        grid=(m // PROJ_TM, n // PROJ_TN),
        in_specs=[
            pl.BlockSpec((PROJ_TM, k), lambda i, j: (i, 0)),
            pl.BlockSpec((k, PROJ_TN), lambda i, j: (0, j)),
            pl.BlockSpec((1, PROJ_TN), lambda i, j: (0, j)),
        ],
        out_specs=pl.BlockSpec((PROJ_TM, PROJ_TN), lambda i, j: (i, j)),
        out_shape=jax.ShapeDtypeStruct((m, n), BF16),
        compiler_params=pltpu.CompilerParams(
            dimension_semantics=("parallel", "arbitrary"),
            vmem_limit_bytes=56 * MIB),
        name="attn_in_proj",
    )(x, w, col_scale)


def _band_allowed(i, j):
    dchunk = i // CHUNK - j // CHUNK
    return (dchunk >= 0) & (dchunk <= LEFT_CHUNKS)


def _live_pieces():
    i = np.arange(ATT_TQ)[None, :]
    live = []
    for c in range(ATT_WIN // ATT_TQ):
        j = np.arange(ATT_TQ)[:, None] + c * ATT_TQ - LEFT
        ok = _band_allowed(i, j)
        n_half = ATT_TQ // ATT_PIECE
        live.append([(r, l) for r in range(n_half) for l in range(n_half)
                     if ok[r * ATT_PIECE:(r + 1) * ATT_PIECE,
                           l * ATT_PIECE:(l + 1) * ATT_PIECE].any()])
    return live


_LIVE_PIECES = _live_pieces()


def _attention_kernel(q_ref, k_ref, v_ref, g_ref, bias_ref, o_ref, vt_ref):
    qi = pl.program_id(2)
    n_kblk = ATT_WIN // ATT_TQ
    key_blk, key_row, bias_row = [], [], []
    for c in range(n_kblk):
        rel = qi - LEFT // ATT_TQ + c
        blk = jnp.maximum(rel, 0)
        key_blk.append(blk)
        key_row.append(pl.multiple_of(blk * ATT_TQ, ATT_TQ))
        bias_row.append(pl.multiple_of(
            jnp.where(rel >= 0, c * ATT_TQ, LEFT + ATT_WIN - ATT_TQ), ATT_TQ))

    def head_cols(hh):
        return slice(hh * HEAD_DIM, (hh + 1) * HEAD_DIM)

    q_row = pl.multiple_of(qi * ATT_TQ, ATT_TQ)
    for hh in range(ATT_HB):
        vt_ref[hh, qi, :HEAD_DIM, :] = v_ref[pl.ds(q_row, ATT_TQ), head_cols(hh)].T
        vt_ref[hh, qi, HEAD_DIM:, :] = jnp.ones((ATT_ONES_ROWS, ATT_TQ), BF16)

    def scores(hh):
        q = q_ref[:, head_cols(hh)]
        pieces = {}
        for c in range(n_kblk):
            k = k_ref[pl.ds(key_row[c], ATT_TQ), head_cols(hh)]
            st = lax.dot_general(k, q, (((1,), (1,)), ((), ())), preferred_element_type=F32)
            for r, l in _LIVE_PIECES[c]:
                rows = slice(r * ATT_PIECE, (r + 1) * ATT_PIECE)
                lanes = slice(l * ATT_PIECE, (l + 1) * ATT_PIECE)
                bias = bias_ref[hh, pl.ds(bias_row[c] + r * ATT_PIECE, ATT_PIECE), lanes]
                pieces[c, r, l] = st[rows, lanes] + bias
        return pieces

    def attend(hh, pieces):
        n_half = ATT_TQ // ATT_PIECE
        probs = {}
        for l in range(n_half):
            live = [key for key in pieces if key[2] == l]
            m = jnp.max(pieces[live[0]], axis=0, keepdims=True)
            for key in live[1:]:
                m = jnp.maximum(m, jnp.max(pieces[key], axis=0, keepdims=True))
            for key in live:
                probs[key] = jnp.exp2(pieces[key] - m).astype(BF16)
        zero = jnp.zeros((ATT_PIECE, ATT_PIECE), BF16)
        ot = None
        for c in range(n_kblk):
            p = jnp.concatenate(
                [jnp.concatenate([probs.get((c, r, l), zero) for l in range(n_half)], axis=1)
                 for r in range(n_half)], axis=0)
            oc = jnp.dot(vt_ref[hh, key_blk[c]], p, preferred_element_type=F32)
            ot = oc if ot is None else ot + oc
        return ot

    def finish(hh, ot):
        o = (ot[:HEAD_DIM] / ot[HEAD_DIM:HEAD_DIM + 1]).T
        g = g_ref[:, head_cols(hh)].astype(F32)
        o_ref[:, head_cols(hh)] = (o * (g * jax.nn.sigmoid(g))).astype(o_ref.dtype)

    pending = [scores(hh) for hh in range(ATT_AHEAD)]
    for hh in range(ATT_HB):
        sts = pending.pop(0)
        if hh + ATT_AHEAD < ATT_HB:
            pending.append(scores(hh + ATT_AHEAD))
        finish(hh, attend(hh, sts))


def _attention_bias_tiles(rel_table):
    n_heads = rel_table.shape[0]
    n_rel = LEFT + ATT_WIN
    i = np.arange(ATT_TQ)[:, None]
    j = np.arange(n_rel)[None, :] - LEFT
    span = ATT_TQ + n_rel - 1
    edge = (span - (2 * MAX_REL_DIST + 1)) // 2
    r = jnp.pad(rel_table.astype(F32), ((0, 0), (edge, edge)), mode="edge")[:, ::-1]
    sq = math.isqrt(ATT_TQ)
    wide = n_rel + ATT_TQ - sq
    fine = jnp.stack([r[:, sq - 1 - t:sq - 1 - t + wide] for t in range(sq)], axis=1)
    b = jnp.stack([fine[:, :, ATT_TQ - sq * (p + 1):ATT_TQ - sq * (p + 1) + n_rel]
                   for p in range(ATT_TQ // sq)], axis=1)
    b = b.reshape(n_heads, ATT_TQ, n_rel)
    b = jnp.where(jnp.asarray(_band_allowed(i, j))[None], b * LOG2_E, NEG_INF)
    return b.transpose(0, 2, 1)


def _attention(qkvg, bias_tiles, batch, seq):
    m = qkvg.shape[0]
    d_att = ATT_HEADS * HEAD_DIM
    hw = ATT_HB * HEAD_DIM
    ncol = d_att // hw
    nq = seq // ATT_TQ
    return pl.pallas_call(
        _attention_kernel,
        grid=(ATT_HEADS // ATT_HB, batch, nq),
        in_specs=[
            pl.BlockSpec((ATT_TQ, hw), lambda h, b, q: (b * nq + q, h)),
            pl.BlockSpec((seq, hw), lambda h, b, q: (b, ncol + h)),
            pl.BlockSpec((seq, hw), lambda h, b, q: (b, 2 * ncol + h)),
            pl.BlockSpec((ATT_TQ, hw), lambda h, b, q: (b * nq + q, 3 * ncol + h)),
            pl.BlockSpec((ATT_HB, LEFT + ATT_WIN, ATT_TQ), lambda h, b, q: (h, 0, 0)),
        ],
        out_specs=pl.BlockSpec((ATT_TQ, hw), lambda h, b, q: (b * nq + q, h)),
        out_shape=jax.ShapeDtypeStruct((m, d_att), BF16),
        scratch_shapes=[
            pltpu.VMEM((ATT_HB, seq // ATT_TQ, HEAD_DIM + ATT_ONES_ROWS, ATT_TQ), BF16),
        ],
        compiler_params=pltpu.CompilerParams(
            dimension_semantics=("parallel", "parallel", "arbitrary"),
            vmem_limit_bytes=56 * MIB),
        name="chunk_attention",
    )(qkvg, qkvg, qkvg, qkvg, bias_tiles)


def _out_ln_kernel(y_ref, w_ref, r_ref, gain_ref, bias_ref, o_ref):
    for i in range(OUT_TM // OUT_SUB):
        rows = slice(i * OUT_SUB, (i + 1) * OUT_SUB)
        acc = jnp.dot(y_ref[rows, :], w_ref[...], preferred_element_type=F32)
        z = DEEPNORM_ALPHA * r_ref[rows, :] + acc
        mu = jnp.mean(z, axis=-1, keepdims=True)
        zc = z - mu
        var = jnp.mean(zc * zc, axis=-1, keepdims=True)
        o_ref[rows, :] = zc * lax.rsqrt(var + LN_EPS) * gain_ref[...] + bias_ref[...]


def _out_ln(y, w, resid, gain, bias, name):
    m, k = y.shape
    n = w.shape[1]
    const = lambda i: (0, 0)
    return pl.pallas_call(
        _out_ln_kernel,
        grid=(m // OUT_TM,),
        in_specs=[
            pl.BlockSpec((OUT_TM, k), lambda i: (i, 0)),
            pl.BlockSpec((k, n), const, pipeline_mode=pl.Buffered(1)),
            pl.BlockSpec((OUT_TM, n), lambda i: (i, 0)),
            pl.BlockSpec((1, n), const),
            pl.BlockSpec((1, n), const),
        ],
        out_specs=pl.BlockSpec((OUT_TM, n), lambda i: (i, 0)),
        out_shape=jax.ShapeDtypeStruct((m, n), F32),
        compiler_params=pltpu.CompilerParams(
            dimension_semantics=("parallel",),
            vmem_limit_bytes=48 * MIB),
        name=name,
    )(y, w, resid, gain, bias)


def _time_permutation():
    r = np.arange(LRU_TM)
    p = np.zeros((LRU_TM, LRU_TM), np.float32)
    p[r, LRU_SUB * (r % SUBLANES) + r // SUBLANES] = 1.0
    return p


def _sublane_scan(a, b):
    row = lax.broadcasted_iota(jnp.int32, a.shape, 0)
    d = 1
    while d < SUBLANES:
        keep = row >= d
        a_sh = jnp.where(keep, pltpu.roll(a, d, 0), 1.0)
        b_sh = jnp.where(keep, pltpu.roll(b, d, 0), 0.0)
        b = a * b_sh + b
        a = a * a_sh
        d *= 2
    return a, b


def _lru_kernel(h_ref, perm_ref, win_ref, cw_ref, cb_ref, wax_ref, bax_ref, lam_ref, y_ref,
                xbuf, ybuf, tailbuf, hcarry):
    t = pl.program_id(1)
    tm = h_ref.shape[0]
    hist = (CONV_W - 1) * SUBLANES

    @pl.when(t == 0)
    def _():
        tailbuf[...] = jnp.zeros_like(tailbuf)
        hcarry[...] = jnp.zeros_like(hcarry)

    x = h_ref[...].astype(BF16)
    xbuf[...] = jnp.dot(perm_ref[0], x, preferred_element_type=F32).astype(BF16)

    lam = lam_ref[...]
    softplus_neg_lam = jnp.maximum(-lam, 0.0) + jnp.log1p(jnp.exp(-jnp.abs(lam)))
    log2_a_scale = (-RG_C * LOG2_E) * softplus_neg_lam
    row0 = lax.broadcasted_iota(jnp.int32, (SUBLANES, LRU_BLOCK_W), 0) == 0

    d_rnn = y_ref.shape[1]

    def project(n):
        w = jnp.concatenate(
            [win_ref[:, n * LRU_BLOCK_W:(n + 1) * LRU_BLOCK_W],
             win_ref[:, d_rnn + n * LRU_BLOCK_W:d_rnn + (n + 1) * LRU_BLOCK_W]], axis=1)
        return jnp.dot(xbuf[...], w, preferred_element_type=F32)

    def recur(n, ug):
        cols = slice(n * LRU_BLOCK_W, (n + 1) * LRU_BLOCK_W)
        u_raw = ug[:, :LRU_BLOCK_W]
        g = ug[:, LRU_BLOCK_W:]
        cur = u_raw[tm - hist:, :]
        prev = tailbuf[:, cols]
        tailbuf[:, cols] = cur
        tiles = []
        for j in range(CONV_W - 1):
            rows = slice(j * SUBLANES, (j + 1) * SUBLANES)
            tiles.append(jnp.where(row0, pltpu.roll(prev[rows], 1, 0),
                                   pltpu.roll(cur[rows], 1, 0)))
        ext = jnp.concatenate(tiles + [u_raw], axis=0)
        u = cb_ref[:, cols]
        for tap in range(CONV_W):
            start = tap * SUBLANES
            u = u + ext[start:start + tm] * cw_ref[tap:tap + 1, cols]
        gates = jnp.dot(u.astype(BF16), wax_ref[n], preferred_element_type=F32)
        gates = jax.nn.sigmoid(gates + bax_ref[n])
        r = gates[:, :LRU_BLOCK_W]
        i = gates[:, LRU_BLOCK_W:]
        a = jnp.exp2(log2_a_scale[:, cols] * r)
        one_m_a2 = 1.0 - a * a
        mult = jnp.where(one_m_a2 > 0.0, one_m_a2 * lax.rsqrt(one_m_a2), 0.0)
        b = mult * (i * u)

        hs, ps = [b[:SUBLANES]], [a[:SUBLANES]]
        for k in range(1, LRU_SUB):
            rows = slice(k * SUBLANES, (k + 1) * SUBLANES)
            hs.append(a[rows] * hs[-1] + b[rows])
            ps.append(a[rows] * ps[-1])
        p_inc, h_inc = _sublane_scan(ps[-1], hs[-1])
        h0 = hcarry[:, cols]
        row = lax.broadcasted_iota(jnp.int32, p_inc.shape, 0)
        p_exc = jnp.where(row == 0, 1.0, pltpu.roll(p_inc, 1, 0))
        h_exc = jnp.where(row == 0, 0.0, pltpu.roll(h_inc, 1, 0))
        carry = p_exc * h0 + h_exc
        hcarry[:, cols] = (p_inc * h0 + h_inc)[SUBLANES - 1:SUBLANES, :]
        h = jnp.concatenate([hk + pk * carry for hk, pk in zip(hs, ps)], axis=0)
        ybuf[:, cols] = (h * (g * jax.nn.sigmoid(g))).astype(BF16)

    pending = [project(n) for n in range(LRU_AHEAD)]
    for n in range(LRU_BLOCKS):
        ug = pending.pop(0)
        if n + LRU_AHEAD < LRU_BLOCKS:
            pending.append(project(n + LRU_AHEAD))
        recur(n, ug)

    y = jnp.dot(perm_ref[1], ybuf[...], preferred_element_type=F32)
    y_ref[...] = y.astype(y_ref.dtype)


def _lru(h, w_in, conv_w, conv_b, wax, bax, lam, batch, seq):
    m, d = h.shape
    d_rnn = lam.shape[1]
    nt = seq // LRU_TM
    perm = _time_permutation()
    perms = jnp.asarray(np.stack([perm, perm.T]), BF16)
    const = lambda b, t: (0, 0)
    const3 = lambda b, t: (0, 0, 0)
    hist = (CONV_W - 1) * SUBLANES
    return pl.pallas_call(
        _lru_kernel,
        grid=(batch, nt),
        in_specs=[
            pl.BlockSpec((LRU_TM, d), lambda b, t: (b * nt + t, 0)),
            pl.BlockSpec((2, LRU_TM, LRU_TM), const3),
            pl.BlockSpec((d, 2 * d_rnn), const, pipeline_mode=pl.Buffered(1)),
            pl.BlockSpec((CONV_W, d_rnn), const),
            pl.BlockSpec((1, d_rnn), const),
            pl.BlockSpec((LRU_BLOCKS, LRU_BLOCK_W, 2 * LRU_BLOCK_W), const3),
            pl.BlockSpec((LRU_BLOCKS, 1, 2 * LRU_BLOCK_W), const3),
            pl.BlockSpec((1, d_rnn), const),
        ],
        out_specs=pl.BlockSpec((LRU_TM, d_rnn), lambda b, t: (b * nt + t, 0)),
        out_shape=jax.ShapeDtypeStruct((m, d_rnn), BF16),
        scratch_shapes=[
            pltpu.VMEM((LRU_TM, d), BF16),
            pltpu.VMEM((LRU_TM, d_rnn), BF16),
            pltpu.VMEM((hist, d_rnn), F32),
            pltpu.VMEM((1, d_rnn), F32),
        ],
        compiler_params=pltpu.CompilerParams(
            dimension_semantics=("parallel", "arbitrary"),
            vmem_limit_bytes=56 * MIB),
        name="rglru",
    )(h, perms, w_in, conv_w, conv_b, wax, bax, lam)


def kernel(x, attn_w_in, attn_w_out, attn_rel_bias, lru_w_in, lru_conv_w, lru_conv_b,
           lru_wa, lru_ba, lru_wx, lru_bx, lru_lambda, lru_w_out, ln_gain, ln_bias):
    batch, seq, d_model = x.shape
    m = batch * seq
    d_att = ATT_HEADS * HEAD_DIM
    h = x.reshape(m, d_model)

    col_scale = jnp.concatenate([jnp.full((d_att,), Q_SCALE, F32),
                                 jnp.ones((3 * d_att,), F32)])[None, :]
    qkvg = _proj(h, attn_w_in[0].astype(BF16), col_scale)
    og = _attention(qkvg, _attention_bias_tiles(attn_rel_bias[0]), batch, seq)
    h = _out_ln(og, attn_w_out[0].astype(BF16), h, ln_gain[0][None, :], ln_bias[0][None, :],
                "attn_out_ln")

    wax = jnp.concatenate([lru_wa[0], lru_wx[0]], axis=-1).astype(BF16)
    bax = jnp.concatenate([lru_ba[0], lru_bx[0]], axis=-1)[:, None, :]
    y = _lru(h, lru_w_in[0].astype(BF16), lru_conv_w[0], lru_conv_b[0][None, :], wax, bax,
             lru_lambda[0][None, :], batch, seq)
    h = _out_ln(y, lru_w_out[0].astype(BF16), h, ln_gain[1][None, :], ln_bias[1][None, :],
                "lru_out_ln")
    return h.reshape(batch, seq, d_model)
```

```python
import math

import jax
import jax.numpy as jnp
import numpy as np
from jax import lax
from jax.experimental import pallas as pl
from jax.experimental.pallas import tpu as pltpu

DEPTH = 2
CHUNK = 64
LEFT_CHUNKS = 8
LEFT = LEFT_CHUNKS * CHUNK
ATT_HEADS = 16
HEAD_DIM = 128
MAX_REL_DIST = 256
LRU_BLOCKS = 16
LRU_BLOCK_W = 128
CONV_W = 4
RG_C = 8.0
LN_EPS = 1e-5
NEG_INF = -1e30
DEEPNORM_ALPHA = (2.0 * DEPTH) ** 0.25
LOG2_E = math.log2(math.e)
Q_SCALE = HEAD_DIM ** -0.5 * LOG2_E

LANES = 128
SUBLANES = 8
MIB = 1024 * 1024

PROJ_TM = 1024
PROJ_TN = 2048
ATT_TQ = 256
ATT_WIN = LEFT + ATT_TQ
ATT_HB = 8
ATT_AHEAD = 3
ATT_ONES_ROWS = 16
ATT_PIECE = 128
OUT_TM = 512
OUT_SUB = 256
LRU_TM = 256
LRU_SUB = LRU_TM // SUBLANES
LRU_AHEAD = 2

F32 = jnp.float32
BF16 = jnp.bfloat16


def _proj_kernel(x_ref, w_ref, s_ref, o_ref):
    x = x_ref[...].astype(BF16)
    acc = jnp.dot(x, w_ref[...], preferred_element_type=F32)
    o_ref[...] = (acc * s_ref[...]).astype(o_ref.dtype)


def _proj(x, w, col_scale):
    m, k = x.shape
    n = w.shape[1]
    return pl.pallas_call(
        _proj_kernel,
        grid=(m // PROJ_TM, n // PROJ_TN),
        in_specs=[
            pl.BlockSpec((PROJ_TM, k), lambda i, j: (i, 0)),
            pl.BlockSpec((k, PROJ_TN), lambda i, j: (0, j)),
            pl.BlockSpec((1, PROJ_TN), lambda i, j: (0, j)),
        ],
        out_specs=pl.BlockSpec((PROJ_TM, PROJ_TN), lambda i, j: (i, j)),
        out_shape=jax.ShapeDtypeStruct((m, n), BF16),
        compiler_params=pltpu.CompilerParams(
            dimension_semantics=("parallel", "arbitrary"),
            vmem_limit_bytes=56 * MIB),
        name="attn_in_proj",
    )(x, w, col_scale)


def _band_allowed(i, j):
    dchunk = i // CHUNK - j // CHUNK
    return (dchunk >= 0) & (dchunk <= LEFT_CHUNKS)


def _live_pieces():
    i = np.arange(ATT_TQ)[None, :]
    live = []
    for c in range(ATT_WIN // ATT_TQ):
        j = np.arange(ATT_TQ)[:, None] + c * ATT_TQ - LEFT
        ok = _band_allowed(i, j)
        n_half = ATT_TQ // ATT_PIECE
        live.append([(r, l) for r in range(n_half) for l in range(n_half)
                     if ok[r * ATT_PIECE:(r + 1) * ATT_PIECE,
                           l * ATT_PIECE:(l + 1) * ATT_PIECE].any()])
    return live


_LIVE_PIECES = _live_pieces()


def _attention_kernel(q_ref, k_ref, v_ref, g_ref, bias_ref, o_ref, vt_ref):
    qi = pl.program_id(2)
    n_kblk = ATT_WIN // ATT_TQ
    key_blk, key_row, bias_row = [], [], []
    for c in range(n_kblk):
        rel = qi - LEFT // ATT_TQ + c
        blk = jnp.maximum(rel, 0)
        key_blk.append(blk)
        key_row.append(pl.multiple_of(blk * ATT_TQ, ATT_TQ))
        bias_row.append(pl.multiple_of(
            jnp.where(rel >= 0, c * ATT_TQ, LEFT + ATT_WIN - ATT_TQ), ATT_TQ))

    def head_cols(hh):
        return slice(hh * HEAD_DIM, (hh + 1) * HEAD_DIM)

    q_row = pl.multiple_of(qi * ATT_TQ, ATT_TQ)
    for hh in range(ATT_HB):
        vt_ref[hh, qi, :HEAD_DIM, :] = v_ref[pl.ds(q_row, ATT_TQ), head_cols(hh)].T
        vt_ref[hh, qi, HEAD_DIM:, :] = jnp.ones((ATT_ONES_ROWS, ATT_TQ), BF16)

    def scores(hh):
        q = q_ref[:, head_cols(hh)]
        pieces = {}
        for c in range(n_kblk):
            k = k_ref[pl.ds(key_row[c], ATT_TQ), head_cols(hh)]
            st = lax.dot_general(k, q, (((1,), (1,)), ((), ())), preferred_element_type=F32)
            for r, l in _LIVE_PIECES[c]:
                rows = slice(r * ATT_PIECE, (r + 1) * ATT_PIECE)
                lanes = slice(l * ATT_PIECE, (l + 1) * ATT_PIECE)
                bias = bias_ref[hh, pl.ds(bias_row[c] + r * ATT_PIECE, ATT_PIECE), lanes]
                pieces[c, r, l] = st[rows, lanes] + bias
        return pieces

    def attend(hh, pieces):
        n_half = ATT_TQ // ATT_PIECE
        probs = {}
        for l in range(n_half):
            live = [key for key in pieces if key[2] == l]
            m = jnp.max(pieces[live[0]], axis=0, keepdims=True)
            for key in live[1:]:
                m = jnp.maximum(m, jnp.max(pieces[key], axis=0, keepdims=True))
            for key in live:
                probs[key] = jnp.exp2(pieces[key] - m).astype(BF16)
        zero = jnp.zeros((ATT_PIECE, ATT_PIECE), BF16)
        ot = None
        for c in range(n_kblk):
            p = jnp.concatenate(
                [jnp.concatenate([probs.get((c, r, l), zero) for l in range(n_half)], axis=1)
                 for r in range(n_half)], axis=0)
            oc = jnp.dot(vt_ref[hh, key_blk[c]], p, preferred_element_type=F32)
            ot = oc if ot is None else ot + oc
        return ot

    def finish(hh, ot):
        o = (ot[:HEAD_DIM] / ot[HEAD_DIM:HEAD_DIM + 1]).T
        g = g_ref[:, head_cols(hh)].astype(F32)
        o_ref[:, head_cols(hh)] = (o * (g * jax.nn.sigmoid(g))).astype(o_ref.dtype)

    pending = [scores(hh) for hh in range(ATT_AHEAD)]
    for hh in range(ATT_HB):
        sts = pending.pop(0)
        if hh + ATT_AHEAD < ATT_HB:
            pending.append(scores(hh + ATT_AHEAD))
        finish(hh, attend(hh, sts))


def _attention_bias_tiles(rel_table):
    n_heads = rel_table.shape[0]
    n_rel = LEFT + ATT_WIN
    i = np.arange(ATT_TQ)[:, None]
    j = np.arange(n_rel)[None, :] - LEFT
    span = ATT_TQ + n_rel - 1
    edge = (span - (2 * MAX_REL_DIST + 1)) // 2
    r = jnp.pad(rel_table.astype(F32), ((0, 0), (edge, edge)), mode="edge")[:, ::-1]
    sq = math.isqrt(ATT_TQ)
    wide = n_rel + ATT_TQ - sq
    fine = jnp.stack([r[:, sq - 1 - t:sq - 1 - t + wide] for t in range(sq)], axis=1)
    b = jnp.stack([fine[:, :, ATT_TQ - sq * (p + 1):ATT_TQ - sq * (p + 1) + n_rel]
                   for p in range(ATT_TQ // sq)], axis=1)
    b = b.reshape(n_heads, ATT_TQ, n_rel)
    b = jnp.where(jnp.asarray(_band_allowed(i, j))[None], b * LOG2_E, NEG_INF)
    return b.transpose(0, 2, 1)


def _attention(qkvg, bias_tiles, batch, seq):
    m = qkvg.shape[0]
    d_att = ATT_HEADS * HEAD_DIM
    hw = ATT_HB * HEAD_DIM
    ncol = d_att // hw
    nq = seq // ATT_TQ
    return pl.pallas_call(
        _attention_kernel,
        grid=(ATT_HEADS // ATT_HB, batch, nq),
        in_specs=[
            pl.BlockSpec((ATT_TQ, hw), lambda h, b, q: (b * nq + q, h)),
            pl.BlockSpec((seq, hw), lambda h, b, q: (b, ncol + h)),
            pl.BlockSpec((seq, hw), lambda h, b, q: (b, 2 * ncol + h)),
            pl.BlockSpec((ATT_TQ, hw), lambda h, b, q: (b * nq + q, 3 * ncol + h)),
            pl.BlockSpec((ATT_HB, LEFT + ATT_WIN, ATT_TQ), lambda h, b, q: (h, 0, 0)),
        ],
        out_specs=pl.BlockSpec((ATT_TQ, hw), lambda h, b, q: (b * nq + q, h)),
        out_shape=jax.ShapeDtypeStruct((m, d_att), BF16),
        scratch_shapes=[
            pltpu.VMEM((ATT_HB, seq // ATT_TQ, HEAD_DIM + ATT_ONES_ROWS, ATT_TQ), BF16),
        ],
        compiler_params=pltpu.CompilerParams(
            dimension_semantics=("parallel", "parallel", "arbitrary"),
            vmem_limit_bytes=56 * MIB),
        name="chunk_attention",
    )(qkvg, qkvg, qkvg, qkvg, bias_tiles)


def _out_ln_kernel(y_ref, w_ref, r_ref, gain_ref, bias_ref, o_ref):
    for i in range(OUT_TM // OUT_SUB):
        rows = slice(i * OUT_SUB, (i + 1) * OUT_SUB)
        acc = jnp.dot(y_ref[rows, :], w_ref[...], preferred_element_type=F32)
        z = DEEPNORM_ALPHA * r_ref[rows, :] + acc
        mu = jnp.mean(z, axis=-1, keepdims=True)
        zc = z - mu
        var = jnp.mean(zc * zc, axis=-1, keepdims=True)
        o_ref[rows, :] = zc * lax.rsqrt(var + LN_EPS) * gain_ref[...] + bias_ref[...]


def _out_ln(y, w, resid, gain, bias, name):
    m, k = y.shape
    n = w.shape[1]
    const = lambda i: (0, 0)
    return pl.pallas_call(
        _out_ln_kernel,
        grid=(m // OUT_TM,),
        in_specs=[
            pl.BlockSpec((OUT_TM, k), lambda i: (i, 0)),
            pl.BlockSpec((k, n), const, pipeline_mode=pl.Buffered(1)),
            pl.BlockSpec((OUT_TM, n), lambda i: (i, 0)),
            pl.BlockSpec((1, n), const),
            pl.BlockSpec((1, n), const),
        ],
        out_specs=pl.BlockSpec((OUT_TM, n), lambda i: (i, 0)),
        out_shape=jax.ShapeDtypeStruct((m, n), F32),
        compiler_params=pltpu.CompilerParams(
            dimension_semantics=("parallel",),
            vmem_limit_bytes=48 * MIB),
        name=name,
    )(y, w, resid, gain, bias)


def _time_permutation():
    r = np.arange(LRU_TM)
    p = np.zeros((LRU_TM, LRU_TM), np.float32)
    p[r, LRU_SUB * (r % SUBLANES) + r // SUBLANES] = 1.0
    return p


def _sublane_scan(a, b):
    row = lax.broadcasted_iota(jnp.int32, a.shape, 0)
    d = 1
    while d < SUBLANES:
        keep = row >= d
        a_sh = jnp.where(keep, pltpu.roll(a, d, 0), 1.0)
        b_sh = jnp.where(keep, pltpu.roll(b, d, 0), 0.0)
        b = a * b_sh + b
        a = a * a_sh
        d *= 2
    return a, b


def _lru_kernel(h_ref, perm_ref, win_ref, cw_ref, cb_ref, wax_ref, bax_ref, lam_ref, y_ref,
                xbuf, ybuf, tailbuf, hcarry):
    t = pl.program_id(1)
    tm = h_ref.shape[0]
    hist = (CONV_W - 1) * SUBLANES

    @pl.when(t == 0)
    def _():
        tailbuf[...] = jnp.zeros_like(tailbuf)
        hcarry[...] = jnp.zeros_like(hcarry)

    x = h_ref[...].astype(BF16)
    xbuf[...] = jnp.dot(perm_ref[0], x, preferred_element_type=F32).astype(BF16)

    lam = lam_ref[...]
    softplus_neg_lam = jnp.maximum(-lam, 0.0) + jnp.log1p(jnp.exp(-jnp.abs(lam)))
    log2_a_scale = (-RG_C * LOG2_E) * softplus_neg_lam
    row0 = lax.broadcasted_iota(jnp.int32, (SUBLANES, LRU_BLOCK_W), 0) == 0

    d_rnn = y_ref.shape[1]

    def project(n):
        w = jnp.concatenate(
            [win_ref[:, n * LRU_BLOCK_W:(n + 1) * LRU_BLOCK_W],
             win_ref[:, d_rnn + n * LRU_BLOCK_W:d_rnn + (n + 1) * LRU_BLOCK_W]], axis=1)
        return jnp.dot(xbuf[...], w, preferred_element_type=F32)

    def recur(n, ug):
        cols = slice(n * LRU_BLOCK_W, (n + 1) * LRU_BLOCK_W)
        u_raw = ug[:, :LRU_BLOCK_W]
        g = ug[:, LRU_BLOCK_W:]
        cur = u_raw[tm - hist:, :]
        prev = tailbuf[:, cols]
        tailbuf[:, cols] = cur
        tiles = []
        for j in range(CONV_W - 1):
            rows = slice(j * SUBLANES, (j + 1) * SUBLANES)
            tiles.append(jnp.where(row0, pltpu.roll(prev[rows], 1, 0),
                                   pltpu.roll(cur[rows], 1, 0)))
        ext = jnp.concatenate(tiles + [u_raw], axis=0)
        u = cb_ref[:, cols]
        for tap in range(CONV_W):
            start = tap * SUBLANES
            u = u + ext[start:start + tm] * cw_ref[tap:tap + 1, cols]
        gates = jnp.dot(u.astype(BF16), wax_ref[n], preferred_element_type=F32)
        gates = jax.nn.sigmoid(gates + bax_ref[n])
        r = gates[:, :LRU_BLOCK_W]
        i = gates[:, LRU_BLOCK_W:]
        a = jnp.exp2(log2_a_scale[:, cols] * r)
        one_m_a2 = 1.0 - a * a
        mult = jnp.where(one_m_a2 > 0.0, one_m_a2 * lax.rsqrt(one_m_a2), 0.0)
        b = mult * (i * u)

        hs, ps = [b[:SUBLANES]], [a[:SUBLANES]]
        for k in range(1, LRU_SUB):
            rows = slice(k * SUBLANES, (k + 1) * SUBLANES)
            hs.append(a[rows] * hs[-1] + b[rows])
            ps.append(a[rows] * ps[-1])
        p_inc, h_inc = _sublane_scan(ps[-1], hs[-1])
        h0 = hcarry[:, cols]
        row = lax.broadcasted_iota(jnp.int32, p_inc.shape, 0)
        p_exc = jnp.where(row == 0, 1.0, pltpu.roll(p_inc, 1, 0))
        h_exc = jnp.where(row == 0, 0.0, pltpu.roll(h_inc, 1, 0))
        carry = p_exc * h0 + h_exc
        hcarry[:, cols] = (p_inc * h0 + h_inc)[SUBLANES - 1:SUBLANES, :]
        h = jnp.concatenate([hk + pk * carry for hk, pk in zip(hs, ps)], axis=0)
        ybuf[:, cols] = (h * (g * jax.nn.sigmoid(g))).astype(BF16)

    pending = [project(n) for n in range(LRU_AHEAD)]
    for n in range(LRU_BLOCKS):
        ug = pending.pop(0)
        if n + LRU_AHEAD < LRU_BLOCKS:
            pending.append(project(n + LRU_AHEAD))
        recur(n, ug)

    y = jnp.dot(perm_ref[1], ybuf[...], preferred_element_type=F32)
    y_ref[...] = y.astype(y_ref.dtype)


def _lru(h, w_in, conv_w, conv_b, wax, bax, lam, batch, seq):
    m, d = h.shape
    d_rnn = lam.shape[1]
    nt = seq // LRU_TM
    perm = _time_permutation()
    perms = jnp.asarray(np.stack([perm, perm.T]), BF16)
    const = lambda b, t: (0, 0)
    const3 = lambda b, t: (0, 0, 0)
    hist = (CONV_W - 1) * SUBLANES
    return pl.pallas_call(
        _lru_kernel,
        grid=(batch, nt),
        in_specs=[
            pl.BlockSpec((LRU_TM, d), lambda b, t: (b * nt + t, 0)),
            pl.BlockSpec((2, LRU_TM, LRU_TM), const3),
            pl.BlockSpec((d, 2 * d_rnn), const, pipeline_mode=pl.Buffered(1)),
            pl.BlockSpec((CONV_W, d_rnn), const),
            pl.BlockSpec((1, d_rnn), const),
            pl.BlockSpec((LRU_BLOCKS, LRU_BLOCK_W, 2 * LRU_BLOCK_W), const3),
            pl.BlockSpec((LRU_BLOCKS, 1, 2 * LRU_BLOCK_W), const3),
            pl.BlockSpec((1, d_rnn), const),
        ],
        out_specs=pl.BlockSpec((LRU_TM, d_rnn), lambda b, t: (b * nt + t, 0)),
        out_shape=jax.ShapeDtypeStruct((m, d_rnn), BF16),
        scratch_shapes=[
            pltpu.VMEM((LRU_TM, d), BF16),
            pltpu.VMEM((LRU_TM, d_rnn), BF16),
            pltpu.VMEM((hist, d_rnn), F32),
            pltpu.VMEM((1, d_rnn), F32),
        ],
        compiler_params=pltpu.CompilerParams(
            dimension_semantics=("parallel", "arbitrary"),
            vmem_limit_bytes=56 * MIB),
        name="rglru",
    )(h, perms, w_in, conv_w, conv_b, wax, bax, lam)


def kernel(x, attn_w_in, attn_w_out, attn_rel_bias, lru_w_in, lru_conv_w, lru_conv_b,
           lru_wa, lru_ba, lru_wx, lru_bx, lru_lambda, lru_w_out, ln_gain, ln_bias):
    batch, seq, d_model = x.shape
    m = batch * seq
    d_att = ATT_HEADS * HEAD_DIM
    h = x.reshape(m, d_model)

    col_scale = jnp.concatenate([jnp.full((d_att,), Q_SCALE, F32),
                                 jnp.ones((3 * d_att,), F32)])[None, :]
    qkvg = _proj(h, attn_w_in[0].astype(BF16), col_scale)
    og = _attention(qkvg, _attention_bias_tiles(attn_rel_bias[0]), batch, seq)
    h = _out_ln(og, attn_w_out[0].astype(BF16), h, ln_gain[0][None, :], ln_bias[0][None, :],
                "attn_out_ln")

    wax = jnp.concatenate([lru_wa[0], lru_wx[0]], axis=-1).astype(BF16)
    bax = jnp.concatenate([lru_ba[0], lru_bx[0]], axis=-1)[:, None, :]
    y = _lru(h, lru_w_in[0].astype(BF16), lru_conv_w[0], lru_conv_b[0][None, :], wax, bax,
             lru_lambda[0][None, :], batch, seq)
    h = _out_ln(y, lru_w_out[0].astype(BF16), h, ln_gain[1][None, :], ln_bias[1][None, :],
                "lru_out_ln")
    return h.reshape(batch, seq, d_model)
```

```python
import math

import jax
import jax.numpy as jnp
import numpy as np
from jax import lax
from jax.experimental import pallas as pl
from jax.experimental.pallas import tpu as pltpu

DEPTH = 2
CHUNK = 64
LEFT_CHUNKS = 8
LEFT = LEFT_CHUNKS * CHUNK
ATT_HEADS = 16
HEAD_DIM = 128
MAX_REL_DIST = 256
LRU_BLOCKS = 16
LRU_BLOCK_W = 128
CONV_W = 4
RG_C = 8.0
LN_EPS = 1e-5
NEG_INF = -1e30
DEEPNORM_ALPHA = (2.0 * DEPTH) ** 0.25
LOG2_E = math.log2(math.e)
Q_SCALE = HEAD_DIM ** -0.5 * LOG2_E

SUBLANES = 8
MIB = 1024 * 1024
V7X_VMEM_BYTES = 64 * MIB

PROJ_TM = 1024
PROJ_TN = 2048
ATT_TQ = 256
ATT_WIN = LEFT + ATT_TQ
ATT_HB = 8
ATT_AHEAD = 2
ATT_ONES_ROWS = 16
ATT_PIECE = 128
OUT_TM = 512
OUT_SUB = 256
LRU_TM = 256
LRU_SUB = LRU_TM // SUBLANES
LRU_AHEAD = 2

F32 = jnp.float32
BF16 = jnp.bfloat16


def _nbytes(shape, dtype):
    return math.prod(shape) * jnp.dtype(dtype).itemsize


def _vmem_limit(pipelined, resident=(), temporaries=()):
    need = 2 * sum(pipelined) + sum(resident) + sum(temporaries)
    assert need <= V7X_VMEM_BYTES, f"{need} bytes of VMEM requested"
    return need


def _proj_kernel(x_ref, w_ref, s_ref, o_ref):
    x = x_ref[...].astype(BF16)
    acc = jnp.dot(x, w_ref[...], preferred_element_type=F32)
    o_ref[...] = (acc * s_ref[...]).astype(o_ref.dtype)


def _proj(x, w, col_scale):
    m, k = x.shape
    n = w.shape[1]
    vmem = _vmem_limit(
        pipelined=[_nbytes((PROJ_TM, k), F32), _nbytes((k, PROJ_TN), BF16),
                   _nbytes((SUBLANES, PROJ_TN), F32), _nbytes((PROJ_TM, PROJ_TN), BF16)],
        temporaries=[_nbytes((PROJ_TM, k), BF16), _nbytes((PROJ_TM, PROJ_TN), F32)])
    return pl.pallas_call(
        _proj_kernel,
        grid=(m // PROJ_TM, n // PROJ_TN),
        in_specs=[
            pl.BlockSpec((PROJ_TM, k), lambda i, j: (i, 0)),
            pl.BlockSpec((k, PROJ_TN), lambda i, j: (0, j)),
            pl.BlockSpec((1, PROJ_TN), lambda i, j: (0, j)),
        ],
        out_specs=pl.BlockSpec((PROJ_TM, PROJ_TN), lambda i, j: (i, j)),
        out_shape=jax.ShapeDtypeStruct((m, n), BF16),
        compiler_params=pltpu.CompilerParams(
            dimension_semantics=("parallel", "arbitrary"),
            vmem_limit_bytes=vmem),
        name="attn_in_proj",
    )(x, w, col_scale)


def _band_allowed(i, j):
    dchunk = i // CHUNK - j // CHUNK
    return (dchunk >= 0) & (dchunk <= LEFT_CHUNKS)


def _live_pieces():
    i = np.arange(ATT_TQ)[None, :]
    live = []
    for c in range(ATT_WIN // ATT_TQ):
        j = np.arange(ATT_TQ)[:, None] + c * ATT_TQ - LEFT
        ok = _band_allowed(i, j)
        n_half = ATT_TQ // ATT_PIECE
        live.append([(r, l) for r in range(n_half) for l in range(n_half)
                     if ok[r * ATT_PIECE:(r + 1) * ATT_PIECE,
                           l * ATT_PIECE:(l + 1) * ATT_PIECE].any()])
    return live


_LIVE_PIECES = _live_pieces()


def _attention_kernel(q_ref, k_ref, v_ref, g_ref, bias_ref, o_ref, vt_ref):
    qi = pl.program_id(2)
    n_kblk = ATT_WIN // ATT_TQ
    key_blk, key_row, bias_row = [], [], []
    for c in range(n_kblk):
        rel = qi - LEFT // ATT_TQ + c
        blk = jnp.maximum(rel, 0)
        key_blk.append(blk)
        key_row.append(pl.multiple_of(blk * ATT_TQ, ATT_TQ))
        bias_row.append(pl.multiple_of(
            jnp.where(rel >= 0, c * ATT_TQ, LEFT + ATT_WIN - ATT_TQ), ATT_TQ))

    def head_cols(hh):
        return slice(hh * HEAD_DIM, (hh + 1) * HEAD_DIM)

    q_row = pl.multiple_of(qi * ATT_TQ, ATT_TQ)
    for hh in range(ATT_HB):
        vt_ref[hh, qi, :HEAD_DIM, :] = v_ref[pl.ds(q_row, ATT_TQ), head_cols(hh)].T
        vt_ref[hh, qi, HEAD_DIM:, :] = jnp.ones((ATT_ONES_ROWS, ATT_TQ), BF16)

    def scores(hh):
        q = q_ref[:, head_cols(hh)]
        pieces = {}
        for c in range(n_kblk):
            k = k_ref[pl.ds(key_row[c], ATT_TQ), head_cols(hh)]
            st = lax.dot_general(k, q, (((1,), (1,)), ((), ())), preferred_element_type=F32)
            for r, l in _LIVE_PIECES[c]:
                rows = slice(r * ATT_PIECE, (r + 1) * ATT_PIECE)
                lanes = slice(l * ATT_PIECE, (l + 1) * ATT_PIECE)
                bias = bias_ref[hh, pl.ds(bias_row[c] + r * ATT_PIECE, ATT_PIECE), lanes]
                pieces[c, r, l] = st[rows, lanes] + bias
        return pieces

    def attend(hh, pieces):
        n_half = ATT_TQ // ATT_PIECE
        probs = {}
        for l in range(n_half):
            live = [key for key in pieces if key[2] == l]
            m = jnp.max(pieces[live[0]], axis=0, keepdims=True)
            for key in live[1:]:
                m = jnp.maximum(m, jnp.max(pieces[key], axis=0, keepdims=True))
            for key in live:
                probs[key] = jnp.exp2(pieces[key] - m).astype(BF16)
        zero = jnp.zeros((ATT_PIECE, ATT_PIECE), BF16)
        ot = None
        for c in range(n_kblk):
            p = jnp.concatenate(
                [jnp.concatenate([probs.get((c, r, l), zero) for l in range(n_half)], axis=1)
                 for r in range(n_half)], axis=0)
            oc = jnp.dot(vt_ref[hh, key_blk[c]], p, preferred_element_type=F32)
            ot = oc if ot is None else ot + oc
        return ot

    def finish(hh, ot):
        o = (ot[:HEAD_DIM] / ot[HEAD_DIM:HEAD_DIM + 1]).T
        g = g_ref[:, head_cols(hh)].astype(F32)
        o_ref[:, head_cols(hh)] = (o * (g * jax.nn.sigmoid(g))).astype(o_ref.dtype)

    pending = [scores(hh) for hh in range(ATT_AHEAD)]
    for hh in range(ATT_HB):
        sts = pending.pop(0)
        if hh + ATT_AHEAD < ATT_HB:
            pending.append(scores(hh + ATT_AHEAD))
        finish(hh, attend(hh, sts))


def _attention_bias_tiles(rel_table):
    n_heads = rel_table.shape[0]
    n_rel = LEFT + ATT_WIN
    i = np.arange(ATT_TQ)[:, None]
    j = np.arange(n_rel)[None, :] - LEFT
    span = ATT_TQ + n_rel - 1
    edge = (span - (2 * MAX_REL_DIST + 1)) // 2
    r = jnp.pad(rel_table.astype(F32), ((0, 0), (edge, edge)), mode="edge")[:, ::-1]
    sq = math.isqrt(ATT_TQ)
    wide = n_rel + ATT_TQ - sq
    fine = jnp.stack([r[:, sq - 1 - t:sq - 1 - t + wide] for t in range(sq)], axis=1)
    b = jnp.stack([fine[:, :, ATT_TQ - sq * (p + 1):ATT_TQ - sq * (p + 1) + n_rel]
                   for p in range(ATT_TQ // sq)], axis=1)
    b = b.reshape(n_heads, ATT_TQ, n_rel)
    b = jnp.where(jnp.asarray(_band_allowed(i, j))[None], b * LOG2_E, NEG_INF)
    return b.transpose(0, 2, 1)


def _attention(qkvg, bias_tiles, batch, seq):
    m = qkvg.shape[0]
    d_att = ATT_HEADS * HEAD_DIM
    hw = ATT_HB * HEAD_DIM
    ncol = d_att // hw
    nq = seq // ATT_TQ
    vt_shape = (ATT_HB, nq, HEAD_DIM + ATT_ONES_ROWS, ATT_TQ)
    vmem = _vmem_limit(
        pipelined=[3 * _nbytes((ATT_TQ, hw), BF16), 2 * _nbytes((seq, hw), BF16),
                   _nbytes((ATT_HB, LEFT + ATT_WIN, ATT_TQ), F32)],
        resident=[_nbytes(vt_shape, BF16)],
        temporaries=[(ATT_AHEAD + 1) * _nbytes((ATT_WIN, ATT_TQ), F32),
                     _nbytes((ATT_WIN, ATT_TQ), BF16)])
    return pl.pallas_call(
        _attention_kernel,
        grid=(ATT_HEADS // ATT_HB, batch, nq),
        in_specs=[
            pl.BlockSpec((ATT_TQ, hw), lambda h, b, q: (b * nq + q, h)),
            pl.BlockSpec((seq, hw), lambda h, b, q: (b, ncol + h)),
            pl.BlockSpec((seq, hw), lambda h, b, q: (b, 2 * ncol + h)),
            pl.BlockSpec((ATT_TQ, hw), lambda h, b, q: (b * nq + q, 3 * ncol + h)),
            pl.BlockSpec((ATT_HB, LEFT + ATT_WIN, ATT_TQ), lambda h, b, q: (h, 0, 0)),
        ],
        out_specs=pl.BlockSpec((ATT_TQ, hw), lambda h, b, q: (b * nq + q, h)),
        out_shape=jax.ShapeDtypeStruct((m, d_att), BF16),
        scratch_shapes=[pltpu.VMEM(vt_shape, BF16)],
        compiler_params=pltpu.CompilerParams(
            dimension_semantics=("parallel", "parallel", "arbitrary"),
            vmem_limit_bytes=vmem),
        name="chunk_attention",
    )(qkvg, qkvg, qkvg, qkvg, bias_tiles)


def _out_ln_kernel(y_ref, w_ref, r_ref, gain_ref, bias_ref, o_ref):
    for i in range(OUT_TM // OUT_SUB):
        rows = slice(i * OUT_SUB, (i + 1) * OUT_SUB)
        acc = jnp.dot(y_ref[rows, :], w_ref[...], preferred_element_type=F32)
        z = DEEPNORM_ALPHA * r_ref[rows, :] + acc
        mu = jnp.mean(z, axis=-1, keepdims=True)
        zc = z - mu
        var = jnp.mean(zc * zc, axis=-1, keepdims=True)
        o_ref[rows, :] = zc * lax.rsqrt(var + LN_EPS) * gain_ref[...] + bias_ref[...]


def _out_ln(y, w, resid, gain, bias, name):
    m, k = y.shape
    n = w.shape[1]
    const = lambda i: (0, 0)
    vmem = _vmem_limit(
        pipelined=[_nbytes((OUT_TM, k), BF16), 2 * _nbytes((OUT_TM, n), F32),
                   2 * _nbytes((SUBLANES, n), F32)],
        resident=[_nbytes((k, n), BF16)],
        temporaries=[2 * 2 * _nbytes((OUT_SUB, n), F32)])
    return pl.pallas_call(
        _out_ln_kernel,
        grid=(m // OUT_TM,),
        in_specs=[
            pl.BlockSpec((OUT_TM, k), lambda i: (i, 0)),
            pl.BlockSpec((k, n), const, pipeline_mode=pl.Buffered(1)),
            pl.BlockSpec((OUT_TM, n), lambda i: (i, 0)),
            pl.BlockSpec((1, n), const),
            pl.BlockSpec((1, n), const),
        ],
        out_specs=pl.BlockSpec((OUT_TM, n), lambda i: (i, 0)),
        out_shape=jax.ShapeDtypeStruct((m, n), F32),
        compiler_params=pltpu.CompilerParams(
            dimension_semantics=("parallel",),
            vmem_limit_bytes=vmem),
        name=name,
    )(y, w, resid, gain, bias)


def _time_permutation():
    r = np.arange(LRU_TM)
    p = np.zeros((LRU_TM, LRU_TM), np.float32)
    p[r, LRU_SUB * (r % SUBLANES) + r // SUBLANES] = 1.0
    return p


def _sublane_scan(a, b):
    row = lax.broadcasted_iota(jnp.int32, a.shape, 0)
    d = 1
    while d < SUBLANES:
        keep = row >= d
        a_sh = jnp.where(keep, pltpu.roll(a, d, 0), 1.0)
        b_sh = jnp.where(keep, pltpu.roll(b, d, 0), 0.0)
        b = a * b_sh + b
        a = a * a_sh
        d *= 2
    return a, b


def _lru_kernel(h_ref, perm_ref, win_ref, cw_ref, cb_ref, wax_ref, bax_ref, lam_ref, y_ref,
                xbuf, ybuf, tailbuf, hcarry):
    t = pl.program_id(1)
    tm = h_ref.shape[0]
    hist = (CONV_W - 1) * SUBLANES

    @pl.when(t == 0)
    def _():
        tailbuf[...] = jnp.zeros_like(tailbuf)
        hcarry[...] = jnp.zeros_like(hcarry)

    x = h_ref[...].astype(BF16)
    xbuf[...] = jnp.dot(perm_ref[0], x, preferred_element_type=F32).astype(BF16)

    lam = lam_ref[...]
    softplus_neg_lam = jnp.maximum(-lam, 0.0) + jnp.log1p(jnp.exp(-jnp.abs(lam)))
    log2_a_scale = (-RG_C * LOG2_E) * softplus_neg_lam
    row0 = lax.broadcasted_iota(jnp.int32, (SUBLANES, LRU_BLOCK_W), 0) == 0

    d_rnn = y_ref.shape[1]

    def project(n):
        w = jnp.concatenate(
            [win_ref[:, n * LRU_BLOCK_W:(n + 1) * LRU_BLOCK_W],
             win_ref[:, d_rnn + n * LRU_BLOCK_W:d_rnn + (n + 1) * LRU_BLOCK_W]], axis=1)
        return jnp.dot(xbuf[...], w, preferred_element_type=F32)

    def recur(n, ug):
        cols = slice(n * LRU_BLOCK_W, (n + 1) * LRU_BLOCK_W)
        u_raw = ug[:, :LRU_BLOCK_W]
        g = ug[:, LRU_BLOCK_W:]
        cur = u_raw[tm - hist:, :]
        prev = tailbuf[:, cols]
        tailbuf[:, cols] = cur
        tiles = []
        for j in range(CONV_W - 1):
            rows = slice(j * SUBLANES, (j + 1) * SUBLANES)
            tiles.append(jnp.where(row0, pltpu.roll(prev[rows], 1, 0),
                                   pltpu.roll(cur[rows], 1, 0)))
        ext = jnp.concatenate(tiles + [u_raw], axis=0)
        u = cb_ref[:, cols]
        for tap in range(CONV_W):
            start = tap * SUBLANES
            u = u + ext[start:start + tm] * cw_ref[tap:tap + 1, cols]
        gates = jnp.dot(u.astype(BF16), wax_ref[n], preferred_element_type=F32)
        gates = jax.nn.sigmoid(gates + bax_ref[n])
        r = gates[:, :LRU_BLOCK_W]
        i = gates[:, LRU_BLOCK_W:]
        a = jnp.exp2(log2_a_scale[:, cols] * r)
        one_m_a2 = 1.0 - a * a
        mult = jnp.where(one_m_a2 > 0.0, one_m_a2 * lax.rsqrt(one_m_a2), 0.0)
        b = mult * (i * u)

        hs, ps = [b[:SUBLANES]], [a[:SUBLANES]]
        for k in range(1, LRU_SUB):
            rows = slice(k * SUBLANES, (k + 1) * SUBLANES)
            hs.append(a[rows] * hs[-1] + b[rows])
            ps.append(a[rows] * ps[-1])
        p_inc, h_inc = _sublane_scan(ps[-1], hs[-1])
        h0 = hcarry[:, cols]
        row = lax.broadcasted_iota(jnp.int32, p_inc.shape, 0)
        p_exc = jnp.where(row == 0, 1.0, pltpu.roll(p_inc, 1, 0))
        h_exc = jnp.where(row == 0, 0.0, pltpu.roll(h_inc, 1, 0))
        carry = p_exc * h0 + h_exc
        hcarry[:, cols] = (p_inc * h0 + h_inc)[SUBLANES - 1:SUBLANES, :]
        h = jnp.concatenate([hk + pk * carry for hk, pk in zip(hs, ps)], axis=0)
        ybuf[:, cols] = (h * (g * jax.nn.sigmoid(g))).astype(BF16)

    pending = [project(n) for n in range(LRU_AHEAD)]
    for n in range(LRU_BLOCKS):
        ug = pending.pop(0)
        if n + LRU_AHEAD < LRU_BLOCKS:
            pending.append(project(n + LRU_AHEAD))
        recur(n, ug)

    y = jnp.dot(perm_ref[1], ybuf[...], preferred_element_type=F32)
    y_ref[...] = y.astype(y_ref.dtype)


def _lru(h, w_in, conv_w, conv_b, wax, bax, lam, batch, seq):
    m, d = h.shape
    d_rnn = lam.shape[1]
    nt = seq // LRU_TM
    perm = _time_permutation()
    perms = jnp.asarray(np.stack([perm, perm.T]), BF16)
    const = lambda b, t: (0, 0)
    const3 = lambda b, t: (0, 0, 0)
    hist = (CONV_W - 1) * SUBLANES
    block_f32 = _nbytes((LRU_TM, LRU_BLOCK_W), F32)
    vmem = _vmem_limit(
        pipelined=[_nbytes((LRU_TM, d), F32), _nbytes((LRU_TM, d_rnn), BF16),
                   _nbytes((2, LRU_TM, LRU_TM), BF16),
                   _nbytes((LRU_BLOCKS, LRU_BLOCK_W, 2 * LRU_BLOCK_W), BF16),
                   _nbytes((LRU_BLOCKS, SUBLANES, 2 * LRU_BLOCK_W), F32),
                   (CONV_W + 3 * SUBLANES) * d_rnn * 4],
        resident=[_nbytes((d, 2 * d_rnn), BF16), _nbytes((LRU_TM, d), BF16),
                  _nbytes((LRU_TM, d_rnn), BF16), _nbytes((hist + SUBLANES, d_rnn), F32)],
        temporaries=[_nbytes((LRU_TM, d), F32), (LRU_AHEAD + 1) * 2 * block_f32,
                     12 * block_f32])
    return pl.pallas_call(
        _lru_kernel,
        grid=(batch, nt),
        in_specs=[
            pl.BlockSpec((LRU_TM, d), lambda b, t: (b * nt + t, 0)),
            pl.BlockSpec((2, LRU_TM, LRU_TM), const3),
            pl.BlockSpec((d, 2 * d_rnn), const, pipeline_mode=pl.Buffered(1)),
            pl.BlockSpec((CONV_W, d_rnn), const),
            pl.BlockSpec((1, d_rnn), const),
            pl.BlockSpec((LRU_BLOCKS, LRU_BLOCK_W, 2 * LRU_BLOCK_W), const3),
            pl.BlockSpec((LRU_BLOCKS, 1, 2 * LRU_BLOCK_W), const3),
            pl.BlockSpec((1, d_rnn), const),
        ],
        out_specs=pl.BlockSpec((LRU_TM, d_rnn), lambda b, t: (b * nt + t, 0)),
        out_shape=jax.ShapeDtypeStruct((m, d_rnn), BF16),
        scratch_shapes=[
            pltpu.VMEM((LRU_TM, d), BF16),
            pltpu.VMEM((LRU_TM, d_rnn), BF16),
            pltpu.VMEM((hist, d_rnn), F32),
            pltpu.VMEM((1, d_rnn), F32),
        ],
        compiler_params=pltpu.CompilerParams(
            dimension_semantics=("parallel", "arbitrary"),
            vmem_limit_bytes=vmem),
        name="rglru",
    )(h, perms, w_in, conv_w, conv_b, wax, bax, lam)


def kernel(x, attn_w_in, attn_w_out, attn_rel_bias, lru_w_in, lru_conv_w, lru_conv_b,
           lru_wa, lru_ba, lru_wx, lru_bx, lru_lambda, lru_w_out, ln_gain, ln_bias):
    batch, seq, d_model = x.shape
    m = batch * seq
    d_att = ATT_HEADS * HEAD_DIM
    h = x.reshape(m, d_model)

    col_scale = jnp.concatenate([jnp.full((d_att,), Q_SCALE, F32),
                                 jnp.ones((3 * d_att,), F32)])[None, :]
    qkvg = _proj(h, attn_w_in[0].astype(BF16), col_scale)
    og = _attention(qkvg, _attention_bias_tiles(attn_rel_bias[0]), batch, seq)
    h = _out_ln(og, attn_w_out[0].astype(BF16), h, ln_gain[0][None, :], ln_bias[0][None, :],
                "attn_out_ln")

    wax = jnp.concatenate([lru_wa[0], lru_wx[0]], axis=-1).astype(BF16)
    bax = jnp.concatenate([lru_ba[0], lru_bx[0]], axis=-1)[:, None, :]
    y = _lru(h, lru_w_in[0].astype(BF16), lru_conv_w[0], lru_conv_b[0][None, :], wax, bax,
             lru_lambda[0][None, :], batch, seq)
    h = _out_ln(y, lru_w_out[0].astype(BF16), h, ln_gain[1][None, :], ln_bias[1][None, :],
                "lru_out_ln")
    return h.reshape(batch, seq, d_model)
```

```python
import math

import jax
import jax.numpy as jnp
import numpy as np
from jax import lax
from jax.experimental import pallas as pl
from jax.experimental.pallas import tpu as pltpu

DEPTH = 2
CHUNK = 64
LEFT_CHUNKS = 8
LEFT = LEFT_CHUNKS * CHUNK
ATT_HEADS = 16
HEAD_DIM = 128
MAX_REL_DIST = 256
LRU_BLOCKS = 16
LRU_BLOCK_W = 128
CONV_W = 4
RG_C = 8.0
LN_EPS = 1e-5
NEG_INF = -1e30
DEEPNORM_ALPHA = (2.0 * DEPTH) ** 0.25
LOG2_E = math.log2(math.e)
Q_SCALE = HEAD_DIM ** -0.5 * LOG2_E

SUBLANES = 8
MIB = 1024 * 1024
V7X_VMEM_BYTES = 64 * MIB

PROJ_TM = 1024
PROJ_TN = 2048
ATT_TQ = 256
ATT_WIN = LEFT + ATT_TQ
ATT_HB = 8
ATT_QB = 2
ATT_AHEAD = 2
ATT_ONES_ROWS = 16
ATT_PIECE = 128
OUT_TM = 512
OUT_SUB = 256
LRU_TM = 256
LRU_SUB = LRU_TM // SUBLANES
LRU_AHEAD = 2

F32 = jnp.float32
BF16 = jnp.bfloat16


def _nbytes(shape, dtype):
    return math.prod(shape) * jnp.dtype(dtype).itemsize


def _vmem_limit(pipelined, resident=(), temporaries=()):
    need = 2 * sum(pipelined) + sum(resident) + sum(temporaries)
    assert need <= V7X_VMEM_BYTES, f"{need} bytes of VMEM requested"
    return need


def _proj_kernel(x_ref, w_ref, s_ref, o_ref):
    x = x_ref[...].astype(BF16)
    acc = jnp.dot(x, w_ref[...], preferred_element_type=F32)
    o_ref[...] = (acc * s_ref[...]).astype(o_ref.dtype)


def _proj(x, w, col_scale):
    m, k = x.shape
    n = w.shape[1]
    vmem = _vmem_limit(
        pipelined=[_nbytes((PROJ_TM, k), F32), _nbytes((k, PROJ_TN), BF16),
                   _nbytes((SUBLANES, PROJ_TN), F32), _nbytes((PROJ_TM, PROJ_TN), BF16)],
        temporaries=[_nbytes((PROJ_TM, k), BF16), _nbytes((PROJ_TM, PROJ_TN), F32)])
    return pl.pallas_call(
        _proj_kernel,
        grid=(m // PROJ_TM, n // PROJ_TN),
        in_specs=[
            pl.BlockSpec((PROJ_TM, k), lambda i, j: (i, 0)),
            pl.BlockSpec((k, PROJ_TN), lambda i, j: (0, j)),
            pl.BlockSpec((1, PROJ_TN), lambda i, j: (0, j)),
        ],
        out_specs=pl.BlockSpec((PROJ_TM, PROJ_TN), lambda i, j: (i, j)),
        out_shape=jax.ShapeDtypeStruct((m, n), BF16),
        compiler_params=pltpu.CompilerParams(
            dimension_semantics=("parallel", "arbitrary"),
            vmem_limit_bytes=vmem),
        name="attn_in_proj",
    )(x, w, col_scale)


def _band_allowed(i, j):
    dchunk = i // CHUNK - j // CHUNK
    return (dchunk >= 0) & (dchunk <= LEFT_CHUNKS)


def _live_pieces():
    i = np.arange(ATT_TQ)[None, :]
    live = []
    for c in range(ATT_WIN // ATT_TQ):
        j = np.arange(ATT_TQ)[:, None] + c * ATT_TQ - LEFT
        ok = _band_allowed(i, j)
        n_half = ATT_TQ // ATT_PIECE
        live.append([(r, l) for r in range(n_half) for l in range(n_half)
                     if ok[r * ATT_PIECE:(r + 1) * ATT_PIECE,
                           l * ATT_PIECE:(l + 1) * ATT_PIECE].any()])
    return live


_LIVE_PIECES = _live_pieces()


def _attention_kernel(q_ref, k_ref, v_ref, g_ref, bias_ref, o_ref, vt_ref):
    n_kblk = ATT_WIN // ATT_TQ

    def head_cols(hh):
        return slice(hh * HEAD_DIM, (hh + 1) * HEAD_DIM)

    def window(qblk):
        key_blk, key_row, bias_row = [], [], []
        for c in range(n_kblk):
            rel = qblk - LEFT // ATT_TQ + c
            blk = jnp.maximum(rel, 0)
            key_blk.append(blk)
            key_row.append(pl.multiple_of(blk * ATT_TQ, ATT_TQ))
            bias_row.append(pl.multiple_of(
                jnp.where(rel >= 0, c * ATT_TQ, LEFT + ATT_WIN - ATT_TQ), ATT_TQ))
        return key_blk, key_row, bias_row

    qblks = [pl.program_id(2) * ATT_QB + j for j in range(ATT_QB)]
    windows = [window(qblk) for qblk in qblks]

    for qblk in qblks:
        q_row = pl.multiple_of(qblk * ATT_TQ, ATT_TQ)
        for hh in range(ATT_HB):
            vt_ref[hh, qblk, :HEAD_DIM, :] = v_ref[pl.ds(q_row, ATT_TQ), head_cols(hh)].T
            vt_ref[hh, qblk, HEAD_DIM:, :] = jnp.ones((ATT_ONES_ROWS, ATT_TQ), BF16)

    def scores(j, hh):
        _, key_row, bias_row = windows[j]
        q = q_ref[j * ATT_TQ:(j + 1) * ATT_TQ, head_cols(hh)]
        pieces = {}
        for c in range(n_kblk):
            k = k_ref[pl.ds(key_row[c], ATT_TQ), head_cols(hh)]
            st = lax.dot_general(k, q, (((1,), (1,)), ((), ())), preferred_element_type=F32)
            for r, l in _LIVE_PIECES[c]:
                rows = slice(r * ATT_PIECE, (r + 1) * ATT_PIECE)
                lanes = slice(l * ATT_PIECE, (l + 1) * ATT_PIECE)
                bias = bias_ref[hh, pl.ds(bias_row[c] + r * ATT_PIECE, ATT_PIECE), lanes]
                pieces[c, r, l] = st[rows, lanes] + bias
        return pieces

    def attend(j, hh, pieces):
        key_blk = windows[j][0]
        n_half = ATT_TQ // ATT_PIECE
        probs = {}
        for l in range(n_half):
            live = [key for key in pieces if key[2] == l]
            m = jnp.max(pieces[live[0]], axis=0, keepdims=True)
            for key in live[1:]:
                m = jnp.maximum(m, jnp.max(pieces[key], axis=0, keepdims=True))
            for key in live:
                probs[key] = jnp.exp2(pieces[key] - m).astype(BF16)
        zero = jnp.zeros((ATT_PIECE, ATT_PIECE), BF16)
        ot = None
        for c in range(n_kblk):
            p = jnp.concatenate(
                [jnp.concatenate([probs.get((c, r, l), zero) for l in range(n_half)], axis=1)
                 for r in range(n_half)], axis=0)
            oc = jnp.dot(vt_ref[hh, key_blk[c]], p, preferred_element_type=F32)
            ot = oc if ot is None else ot + oc
        return ot

    def finish(j, hh, ot):
        rows = slice(j * ATT_TQ, (j + 1) * ATT_TQ)
        o = (ot[:HEAD_DIM] / ot[HEAD_DIM:HEAD_DIM + 1]).T
        g = g_ref[rows, head_cols(hh)].astype(F32)
        o_ref[rows, head_cols(hh)] = (o * (g * jax.nn.sigmoid(g))).astype(o_ref.dtype)

    units = [(j, hh) for j in range(ATT_QB) for hh in range(ATT_HB)]
    pending = [scores(*u) for u in units[:ATT_AHEAD]]
    for n, unit in enumerate(units):
        pieces = pending.pop(0)
        if n + ATT_AHEAD < len(units):
            pending.append(scores(*units[n + ATT_AHEAD]))
        finish(*unit, attend(*unit, pieces))


def _attention_bias_tiles(rel_table):
    n_heads = rel_table.shape[0]
    n_rel = LEFT + ATT_WIN
    i = np.arange(ATT_TQ)[:, None]
    j = np.arange(n_rel)[None, :] - LEFT
    span = ATT_TQ + n_rel - 1
    edge = (span - (2 * MAX_REL_DIST + 1)) // 2
    ext = jnp.pad(rel_table.astype(F32), ((0, 0), (edge, edge)), mode="edge")
    sq = math.isqrt(ATT_TQ)
    wide = n_rel + ATT_TQ - sq
    fine = jnp.stack([ext[:, span - (sq - 1 - t) - wide:span - (sq - 1 - t)]
                      for t in range(sq)], axis=1)[:, :, ::-1]
    b = jnp.stack([fine[:, :, ATT_TQ - sq * (p + 1):ATT_TQ - sq * (p + 1) + n_rel]
                   for p in range(ATT_TQ // sq)], axis=1)
    b = b.reshape(n_heads, ATT_TQ, n_rel)
    b = jnp.where(jnp.asarray(_band_allowed(i, j))[None], b * LOG2_E, NEG_INF)
    return b.transpose(0, 2, 1)


def _attention(qkvg, bias_tiles, batch, seq):
    m = qkvg.shape[0]
    d_att = ATT_HEADS * HEAD_DIM
    hw = ATT_HB * HEAD_DIM
    ncol = d_att // hw
    rows = ATT_QB * ATT_TQ
    nq = seq // rows
    vt_shape = (ATT_HB, seq // ATT_TQ, HEAD_DIM + ATT_ONES_ROWS, ATT_TQ)
    vmem = _vmem_limit(
        pipelined=[3 * _nbytes((rows, hw), BF16), 2 * _nbytes((seq, hw), BF16),
                   _nbytes((ATT_HB, LEFT + ATT_WIN, ATT_TQ), F32)],
        resident=[_nbytes(vt_shape, BF16)],
        temporaries=[(ATT_AHEAD + 1) * _nbytes((ATT_WIN, ATT_TQ), F32),
                     _nbytes((ATT_WIN, ATT_TQ), BF16)])
    return pl.pallas_call(
        _attention_kernel,
        grid=(ATT_HEADS // ATT_HB, batch, nq),
        in_specs=[
            pl.BlockSpec((rows, hw), lambda h, b, q: (b * nq + q, h)),
            pl.BlockSpec((seq, hw), lambda h, b, q: (b, ncol + h)),
            pl.BlockSpec((seq, hw), lambda h, b, q: (b, 2 * ncol + h)),
            pl.BlockSpec((rows, hw), lambda h, b, q: (b * nq + q, 3 * ncol + h)),
            pl.BlockSpec((ATT_HB, LEFT + ATT_WIN, ATT_TQ), lambda h, b, q: (h, 0, 0)),
        ],
        out_specs=pl.BlockSpec((rows, hw), lambda h, b, q: (b * nq + q, h)),
        out_shape=jax.ShapeDtypeStruct((m, d_att), BF16),
        scratch_shapes=[pltpu.VMEM(vt_shape, BF16)],
        compiler_params=pltpu.CompilerParams(
            dimension_semantics=("parallel", "parallel", "arbitrary"),
            vmem_limit_bytes=vmem),
        name="chunk_attention",
    )(qkvg, qkvg, qkvg, qkvg, bias_tiles)


def _out_ln_kernel(y_ref, w_ref, r_ref, gain_ref, bias_ref, o_ref):
    for i in range(OUT_TM // OUT_SUB):
        rows = slice(i * OUT_SUB, (i + 1) * OUT_SUB)
        acc = jnp.dot(y_ref[rows, :], w_ref[...], preferred_element_type=F32)
        z = DEEPNORM_ALPHA * r_ref[rows, :] + acc
        mu = jnp.mean(z, axis=-1, keepdims=True)
        zc = z - mu
        var = jnp.mean(zc * zc, axis=-1, keepdims=True)
        o_ref[rows, :] = zc * lax.rsqrt(var + LN_EPS) * gain_ref[...] + bias_ref[...]


def _out_ln(y, w, resid, gain, bias, name):
    m, k = y.shape
    n = w.shape[1]
    const = lambda i: (0, 0)
    vmem = _vmem_limit(
        pipelined=[_nbytes((OUT_TM, k), BF16), 2 * _nbytes((OUT_TM, n), F32),
                   2 * _nbytes((SUBLANES, n), F32)],
        resident=[_nbytes((k, n), BF16)],
        temporaries=[2 * 2 * _nbytes((OUT_SUB, n), F32)])
    return pl.pallas_call(
        _out_ln_kernel,
        grid=(m // OUT_TM,),
        in_specs=[
            pl.BlockSpec((OUT_TM, k), lambda i: (i, 0)),
            pl.BlockSpec((k, n), const, pipeline_mode=pl.Buffered(1)),
            pl.BlockSpec((OUT_TM, n), lambda i: (i, 0)),
            pl.BlockSpec((1, n), const),
            pl.BlockSpec((1, n), const),
        ],
        out_specs=pl.BlockSpec((OUT_TM, n), lambda i: (i, 0)),
        out_shape=jax.ShapeDtypeStruct((m, n), F32),
        compiler_params=pltpu.CompilerParams(
            dimension_semantics=("parallel",),
            vmem_limit_bytes=vmem),
        name=name,
    )(y, w, resid, gain, bias)


def _time_permutation():
    r = np.arange(LRU_TM)
    p = np.zeros((LRU_TM, LRU_TM), np.float32)
    p[r, LRU_SUB * (r % SUBLANES) + r // SUBLANES] = 1.0
    return p


def _sublane_scan(a, b):
    row = lax.broadcasted_iota(jnp.int32, a.shape, 0)
    d = 1
    while d < SUBLANES:
        keep = row >= d
        a_sh = jnp.where(keep, pltpu.roll(a, d, 0), 1.0)
        b_sh = jnp.where(keep, pltpu.roll(b, d, 0), 0.0)
        b = a * b_sh + b
        a = a * a_sh
        d *= 2
    return a, b


def _lru_kernel(h_ref, perm_ref, win_ref, cw_ref, cb_ref, wax_ref, bax_ref, lam_ref, y_ref,
                xbuf, ybuf, tailbuf, hcarry):
    t = pl.program_id(1)
    tm = h_ref.shape[0]
    hist = (CONV_W - 1) * SUBLANES

    @pl.when(t == 0)
    def _():
        tailbuf[...] = jnp.zeros_like(tailbuf)
        hcarry[...] = jnp.zeros_like(hcarry)

    x = h_ref[...].astype(BF16)
    xbuf[...] = jnp.dot(perm_ref[0], x, preferred_element_type=F32).astype(BF16)

    lam = lam_ref[...]
    softplus_neg_lam = jnp.maximum(-lam, 0.0) + jnp.log1p(jnp.exp(-jnp.abs(lam)))
    log2_a_scale = (-RG_C * LOG2_E) * softplus_neg_lam
    row0 = lax.broadcasted_iota(jnp.int32, (SUBLANES, LRU_BLOCK_W), 0) == 0

    d_rnn = y_ref.shape[1]

    def project(n):
        w = jnp.concatenate(
            [win_ref[:, n * LRU_BLOCK_W:(n + 1) * LRU_BLOCK_W],
             win_ref[:, d_rnn + n * LRU_BLOCK_W:d_rnn + (n + 1) * LRU_BLOCK_W]], axis=1)
        return jnp.dot(xbuf[...], w, preferred_element_type=F32)

    def recur(n, ug):
        cols = slice(n * LRU_BLOCK_W, (n + 1) * LRU_BLOCK_W)
        u_raw = ug[:, :LRU_BLOCK_W]
        g = ug[:, LRU_BLOCK_W:]
        cur = u_raw[tm - hist:, :]
        prev = tailbuf[:, cols]
        tailbuf[:, cols] = cur
        tiles = []
        for j in range(CONV_W - 1):
            rows = slice(j * SUBLANES, (j + 1) * SUBLANES)
            tiles.append(jnp.where(row0, pltpu.roll(prev[rows], 1, 0),
                                   pltpu.roll(cur[rows], 1, 0)))
        ext = jnp.concatenate(tiles + [u_raw], axis=0)
        u = cb_ref[:, cols]
        for tap in range(CONV_W):
            start = tap * SUBLANES
            u = u + ext[start:start + tm] * cw_ref[tap:tap + 1, cols]
        gates = jnp.dot(u.astype(BF16), wax_ref[n], preferred_element_type=F32)
        gates = jax.nn.sigmoid(gates + bax_ref[n])
        r = gates[:, :LRU_BLOCK_W]
        i = gates[:, LRU_BLOCK_W:]
        a = jnp.exp2(log2_a_scale[:, cols] * r)
        one_m_a2 = 1.0 - a * a
        mult = jnp.where(one_m_a2 > 0.0, one_m_a2 * lax.rsqrt(one_m_a2), 0.0)
        b = mult * (i * u)

        hs, ps = [b[:SUBLANES]], [a[:SUBLANES]]
        for k in range(1, LRU_SUB):
            rows = slice(k * SUBLANES, (k + 1) * SUBLANES)
            hs.append(a[rows] * hs[-1] + b[rows])
            ps.append(a[rows] * ps[-1])
        p_inc, h_inc = _sublane_scan(ps[-1], hs[-1])
        h0 = hcarry[:, cols]
        row = lax.broadcasted_iota(jnp.int32, p_inc.shape, 0)
        p_exc = jnp.where(row == 0, 1.0, pltpu.roll(p_inc, 1, 0))
        h_exc = jnp.where(row == 0, 0.0, pltpu.roll(h_inc, 1, 0))
        carry = p_exc * h0 + h_exc
        hcarry[:, cols] = (p_inc * h0 + h_inc)[SUBLANES - 1:SUBLANES, :]
        h = jnp.concatenate([hk + pk * carry for hk, pk in zip(hs, ps)], axis=0)
        ybuf[:, cols] = (h * (g * jax.nn.sigmoid(g))).astype(BF16)

    pending = [project(n) for n in range(LRU_AHEAD)]
    for n in range(LRU_BLOCKS):
        ug = pending.pop(0)
        if n + LRU_AHEAD < LRU_BLOCKS:
            pending.append(project(n + LRU_AHEAD))
        recur(n, ug)

    y = jnp.dot(perm_ref[1], ybuf[...], preferred_element_type=F32)
    y_ref[...] = y.astype(y_ref.dtype)


def _lru(h, w_in, conv_w, conv_b, wax, bax, lam, batch, seq):
    m, d = h.shape
    d_rnn = lam.shape[1]
    nt = seq // LRU_TM
    perm = _time_permutation()
    perms = jnp.asarray(np.stack([perm, perm.T]), BF16)
    const = lambda b, t: (0, 0)
    const3 = lambda b, t: (0, 0, 0)
    hist = (CONV_W - 1) * SUBLANES
    block_f32 = _nbytes((LRU_TM, LRU_BLOCK_W), F32)
    vmem = _vmem_limit(
        pipelined=[_nbytes((LRU_TM, d), F32), _nbytes((LRU_TM, d_rnn), BF16),
                   _nbytes((2, LRU_TM, LRU_TM), BF16),
                   _nbytes((LRU_BLOCKS, LRU_BLOCK_W, 2 * LRU_BLOCK_W), BF16),
                   _nbytes((LRU_BLOCKS, SUBLANES, 2 * LRU_BLOCK_W), F32),
                   (CONV_W + 3 * SUBLANES) * d_rnn * 4],
        resident=[_nbytes((d, 2 * d_rnn), BF16), _nbytes((LRU_TM, d), BF16),
                  _nbytes((LRU_TM, d_rnn), BF16), _nbytes((hist + SUBLANES, d_rnn), F32)],
        temporaries=[_nbytes((LRU_TM, d), F32), (LRU_AHEAD + 1) * 2 * block_f32,
                     12 * block_f32])
    return pl.pallas_call(
        _lru_kernel,
        grid=(batch, nt),
        in_specs=[
            pl.BlockSpec((LRU_TM, d), lambda b, t: (b * nt + t, 0)),
            pl.BlockSpec((2, LRU_TM, LRU_TM), const3),
            pl.BlockSpec((d, 2 * d_rnn), const, pipeline_mode=pl.Buffered(1)),
            pl.BlockSpec((CONV_W, d_rnn), const),
            pl.BlockSpec((1, d_rnn), const),
            pl.BlockSpec((LRU_BLOCKS, LRU_BLOCK_W, 2 * LRU_BLOCK_W), const3),
            pl.BlockSpec((LRU_BLOCKS, 1, 2 * LRU_BLOCK_W), const3),
            pl.BlockSpec((1, d_rnn), const),
        ],
        out_specs=pl.BlockSpec((LRU_TM, d_rnn), lambda b, t: (b * nt + t, 0)),
        out_shape=jax.ShapeDtypeStruct((m, d_rnn), BF16),
        scratch_shapes=[
            pltpu.VMEM((LRU_TM, d), BF16),
            pltpu.VMEM((LRU_TM, d_rnn), BF16),
            pltpu.VMEM((hist, d_rnn), F32),
            pltpu.VMEM((1, d_rnn), F32),
        ],
        compiler_params=pltpu.CompilerParams(
            dimension_semantics=("parallel", "arbitrary"),
            vmem_limit_bytes=vmem),
        name="rglru",
    )(h, perms, w_in, conv_w, conv_b, wax, bax, lam)


def kernel(x, attn_w_in, attn_w_out, attn_rel_bias, lru_w_in, lru_conv_w, lru_conv_b,
           lru_wa, lru_ba, lru_wx, lru_bx, lru_lambda, lru_w_out, ln_gain, ln_bias):
    batch, seq, d_model = x.shape
    m = batch * seq
    d_att = ATT_HEADS * HEAD_DIM
    h = x.reshape(m, d_model)

    col_scale = jnp.concatenate([jnp.full((d_att,), Q_SCALE, F32),
                                 jnp.ones((3 * d_att,), F32)])[None, :]
    qkvg = _proj(h, attn_w_in[0].astype(BF16), col_scale)
    og = _attention(qkvg, _attention_bias_tiles(attn_rel_bias[0]), batch, seq)
    h = _out_ln(og, attn_w_out[0].astype(BF16), h, ln_gain[0][None, :], ln_bias[0][None, :],
                "attn_out_ln")

    wax = jnp.concatenate([lru_wa[0], lru_wx[0]], axis=-1).astype(BF16)
    bax = jnp.concatenate([lru_ba[0], lru_bx[0]], axis=-1)[:, None, :]
    y = _lru(h, lru_w_in[0].astype(BF16), lru_conv_w[0], lru_conv_b[0][None, :], wax, bax,
             lru_lambda[0][None, :], batch, seq)
    h = _out_ln(y, lru_w_out[0].astype(BF16), h, ln_gain[1][None, :], ln_bias[1][None, :],
                "lru_out_ln")
    return h.reshape(batch, seq, d_model)
```

```python
import math

import jax
import jax.numpy as jnp
import numpy as np
from jax import lax
from jax.experimental import pallas as pl
from jax.experimental.pallas import tpu as pltpu

DEPTH = 2
CHUNK = 64
LEFT_CHUNKS = 8
LEFT = LEFT_CHUNKS * CHUNK
ATT_HEADS = 16
HEAD_DIM = 128
MAX_REL_DIST = 256
LRU_BLOCKS = 16
LRU_BLOCK_W = 128
CONV_W = 4
RG_C = 8.0
LN_EPS = 1e-5
NEG_INF = -1e30
DEEPNORM_ALPHA = (2.0 * DEPTH) ** 0.25
LOG2_E = math.log2(math.e)
Q_SCALE = HEAD_DIM ** -0.5 * LOG2_E

SUBLANES = 8
MIB = 1024 * 1024
V7X_VMEM_BYTES = 64 * MIB

PROJ_TM = 1024
PROJ_TN = 2048
ATT_TQ = 256
ATT_WIN = LEFT + ATT_TQ
ATT_HB = 8
ATT_QB = 4
ATT_AHEAD = 2
ATT_ONES_ROWS = 16
ATT_PIECE = 128
OUT_TM = 512
OUT_SUB = 256
LRU_TM = 256
LRU_SUB = LRU_TM // SUBLANES
LRU_AHEAD = 2

F32 = jnp.float32
BF16 = jnp.bfloat16


def _nbytes(shape, dtype):
    return math.prod(shape) * jnp.dtype(dtype).itemsize


def _vmem_limit(pipelined, resident=(), temporaries=()):
    need = 2 * sum(pipelined) + sum(resident) + sum(temporaries)
    assert need <= V7X_VMEM_BYTES, f"{need} bytes of VMEM requested"
    return need


def _proj_kernel(x_ref, w_ref, s_ref, o_ref):
    x = x_ref[...].astype(BF16)
    acc = jnp.dot(x, w_ref[...], preferred_element_type=F32)
    o_ref[...] = (acc * s_ref[...]).astype(o_ref.dtype)


def _proj(x, w, col_scale):
    m, k = x.shape
    n = w.shape[1]
    vmem = _vmem_limit(
        pipelined=[_nbytes((PROJ_TM, k), F32), _nbytes((k, PROJ_TN), BF16),
                   _nbytes((SUBLANES, PROJ_TN), F32), _nbytes((PROJ_TM, PROJ_TN), BF16)],
        temporaries=[_nbytes((PROJ_TM, k), BF16), _nbytes((PROJ_TM, PROJ_TN), F32)])
    return pl.pallas_call(
        _proj_kernel,
        grid=(m // PROJ_TM, n // PROJ_TN),
        in_specs=[
            pl.BlockSpec((PROJ_TM, k), lambda i, j: (i, 0)),
            pl.BlockSpec((k, PROJ_TN), lambda i, j: (0, j)),
            pl.BlockSpec((1, PROJ_TN), lambda i, j: (0, j)),
        ],
        out_specs=pl.BlockSpec((PROJ_TM, PROJ_TN), lambda i, j: (i, j)),
        out_shape=jax.ShapeDtypeStruct((m, n), BF16),
        compiler_params=pltpu.CompilerParams(
            dimension_semantics=("parallel", "arbitrary"),
            vmem_limit_bytes=vmem),
        name="attn_in_proj",
    )(x, w, col_scale)


def _band_allowed(i, j):
    dchunk = i // CHUNK - j // CHUNK
    return (dchunk >= 0) & (dchunk <= LEFT_CHUNKS)


def _live_pieces():
    i = np.arange(ATT_TQ)[None, :]
    live = []
    for c in range(ATT_WIN // ATT_TQ):
        j = np.arange(ATT_TQ)[:, None] + c * ATT_TQ - LEFT
        ok = _band_allowed(i, j)
        n_half = ATT_TQ // ATT_PIECE
        live.append([(r, l) for r in range(n_half) for l in range(n_half)
                     if ok[r * ATT_PIECE:(r + 1) * ATT_PIECE,
                           l * ATT_PIECE:(l + 1) * ATT_PIECE].any()])
    return live


_LIVE_PIECES = _live_pieces()


def _attention_kernel(q_ref, k_ref, v_ref, g_ref, bias_ref, o_ref, vt_ref):
    n_kblk = ATT_WIN // ATT_TQ

    def head_cols(hh):
        return slice(hh * HEAD_DIM, (hh + 1) * HEAD_DIM)

    def window(qblk):
        key_blk, key_row, bias_row = [], [], []
        for c in range(n_kblk):
            rel = qblk - LEFT // ATT_TQ + c
            blk = jnp.maximum(rel, 0)
            key_blk.append(blk)
            key_row.append(pl.multiple_of(blk * ATT_TQ, ATT_TQ))
            bias_row.append(pl.multiple_of(
                jnp.where(rel >= 0, c * ATT_TQ, LEFT + ATT_WIN - ATT_TQ), ATT_TQ))
        return key_blk, key_row, bias_row

    qblks = [pl.program_id(2) * ATT_QB + j for j in range(ATT_QB)]
    windows = [window(qblk) for qblk in qblks]

    for qblk in qblks:
        q_row = pl.multiple_of(qblk * ATT_TQ, ATT_TQ)
        for hh in range(ATT_HB):
            vt_ref[hh, qblk, :HEAD_DIM, :] = v_ref[pl.ds(q_row, ATT_TQ), head_cols(hh)].T
            vt_ref[hh, qblk, HEAD_DIM:, :] = jnp.ones((ATT_ONES_ROWS, ATT_TQ), BF16)

    def scores(j, hh):
        _, key_row, bias_row = windows[j]
        q = q_ref[j * ATT_TQ:(j + 1) * ATT_TQ, head_cols(hh)]
        pieces = {}
        for c in range(n_kblk):
            k = k_ref[pl.ds(key_row[c], ATT_TQ), head_cols(hh)]
            st = lax.dot_general(k, q, (((1,), (1,)), ((), ())), preferred_element_type=F32)
            for r, l in _LIVE_PIECES[c]:
                rows = slice(r * ATT_PIECE, (r + 1) * ATT_PIECE)
                lanes = slice(l * ATT_PIECE, (l + 1) * ATT_PIECE)
                bias = bias_ref[hh, pl.ds(bias_row[c] + r * ATT_PIECE, ATT_PIECE), lanes]
                pieces[c, r, l] = st[rows, lanes] + bias
        return pieces

    def attend(j, hh, pieces):
        key_blk = windows[j][0]
        n_half = ATT_TQ // ATT_PIECE
        probs = {}
        for l in range(n_half):
            live = [key for key in pieces if key[2] == l]
            m = jnp.max(pieces[live[0]], axis=0, keepdims=True)
            for key in live[1:]:
                m = jnp.maximum(m, jnp.max(pieces[key], axis=0, keepdims=True))
            for key in live:
                probs[key] = jnp.exp2(pieces[key] - m).astype(BF16)
        zero = jnp.zeros((ATT_PIECE, ATT_PIECE), BF16)
        ot = None
        for c in range(n_kblk):
            p = jnp.concatenate(
                [jnp.concatenate([probs.get((c, r, l), zero) for l in range(n_half)], axis=1)
                 for r in range(n_half)], axis=0)
            oc = jnp.dot(vt_ref[hh, key_blk[c]], p, preferred_element_type=F32)
            ot = oc if ot is None else ot + oc
        return ot

    def finish(j, hh, ot):
        rows = slice(j * ATT_TQ, (j + 1) * ATT_TQ)
        o = (ot[:HEAD_DIM] / ot[HEAD_DIM:HEAD_DIM + 1]).T
        g = g_ref[rows, head_cols(hh)].astype(F32)
        o_ref[rows, head_cols(hh)] = (o * (g * jax.nn.sigmoid(g))).astype(o_ref.dtype)

    units = [(j, hh) for j in range(ATT_QB) for hh in range(ATT_HB)]
    pending = [scores(*u) for u in units[:ATT_AHEAD]]
    for n, unit in enumerate(units):
        pieces = pending.pop(0)
        if n + ATT_AHEAD < len(units):
            pending.append(scores(*units[n + ATT_AHEAD]))
        finish(*unit, attend(*unit, pieces))


def _attention_bias_tiles(rel_table):
    n_heads = rel_table.shape[0]
    n_rel = LEFT + ATT_WIN
    i = np.arange(ATT_TQ)[:, None]
    j = np.arange(n_rel)[None, :] - LEFT
    span = ATT_TQ + n_rel - 1
    edge = (span - (2 * MAX_REL_DIST + 1)) // 2
    r = jnp.pad(rel_table.astype(F32), ((0, 0), (edge, edge)), mode="edge")[:, ::-1]
    sq = math.isqrt(ATT_TQ)
    wide = n_rel + ATT_TQ - sq
    fine = jnp.stack([r[:, sq - 1 - t:sq - 1 - t + wide] for t in range(sq)], axis=1)
    b = jnp.stack([fine[:, :, ATT_TQ - sq * (p + 1):ATT_TQ - sq * (p + 1) + n_rel]
                   for p in range(ATT_TQ // sq)], axis=1)
    b = b.reshape(n_heads, ATT_TQ, n_rel)
    b = jnp.where(jnp.asarray(_band_allowed(i, j))[None], b * LOG2_E, NEG_INF)
    return b.transpose(0, 2, 1)


def _attention(qkvg, bias_tiles, batch, seq):
    m = qkvg.shape[0]
    d_att = ATT_HEADS * HEAD_DIM
    hw = ATT_HB * HEAD_DIM
    ncol = d_att // hw
    rows = ATT_QB * ATT_TQ
    nq = seq // rows
    vt_shape = (ATT_HB, seq // ATT_TQ, HEAD_DIM + ATT_ONES_ROWS, ATT_TQ)
    vmem = _vmem_limit(
        pipelined=[3 * _nbytes((rows, hw), BF16), 2 * _nbytes((seq, hw), BF16),
                   _nbytes((ATT_HB, LEFT + ATT_WIN, ATT_TQ), F32)],
        resident=[_nbytes(vt_shape, BF16)],
        temporaries=[(ATT_AHEAD + 1) * _nbytes((ATT_WIN, ATT_TQ), F32),
                     _nbytes((ATT_WIN, ATT_TQ), BF16)])
    return pl.pallas_call(
        _attention_kernel,
        grid=(ATT_HEADS // ATT_HB, batch, nq),
        in_specs=[
            pl.BlockSpec((rows, hw), lambda h, b, q: (b * nq + q, h)),
            pl.BlockSpec((seq, hw), lambda h, b, q: (b, ncol + h)),
            pl.BlockSpec((seq, hw), lambda h, b, q: (b, 2 * ncol + h)),
            pl.BlockSpec((rows, hw), lambda h, b, q: (b * nq + q, 3 * ncol + h)),
            pl.BlockSpec((ATT_HB, LEFT + ATT_WIN, ATT_TQ), lambda h, b, q: (h, 0, 0)),
        ],
        out_specs=pl.BlockSpec((rows, hw), lambda h, b, q: (b * nq + q, h)),
        out_shape=jax.ShapeDtypeStruct((m, d_att), BF16),
        scratch_shapes=[pltpu.VMEM(vt_shape, BF16)],
        compiler_params=pltpu.CompilerParams(
            dimension_semantics=("parallel", "parallel", "arbitrary"),
            vmem_limit_bytes=vmem),
        name="chunk_attention",
    )(qkvg, qkvg, qkvg, qkvg, bias_tiles)


def _out_ln_kernel(y_ref, w_ref, r_ref, gain_ref, bias_ref, o_ref):
    for i in range(OUT_TM // OUT_SUB):
        rows = slice(i * OUT_SUB, (i + 1) * OUT_SUB)
        acc = jnp.dot(y_ref[rows, :], w_ref[...], preferred_element_type=F32)
        z = DEEPNORM_ALPHA * r_ref[rows, :] + acc
        mu = jnp.mean(z, axis=-1, keepdims=True)
        zc = z - mu
        var = jnp.mean(zc * zc, axis=-1, keepdims=True)
        o_ref[rows, :] = zc * lax.rsqrt(var + LN_EPS) * gain_ref[...] + bias_ref[...]


def _out_ln(y, w, resid, gain, bias, name):
    m, k = y.shape
    n = w.shape[1]
    const = lambda i: (0, 0)
    vmem = _vmem_limit(
        pipelined=[_nbytes((OUT_TM, k), BF16), 2 * _nbytes((OUT_TM, n), F32),
                   2 * _nbytes((SUBLANES, n), F32)],
        resident=[_nbytes((k, n), BF16)],
        temporaries=[2 * 2 * _nbytes((OUT_SUB, n), F32)])
    return pl.pallas_call(
        _out_ln_kernel,
        grid=(m // OUT_TM,),
        in_specs=[
            pl.BlockSpec((OUT_TM, k), lambda i: (i, 0)),
            pl.BlockSpec((k, n), const, pipeline_mode=pl.Buffered(1)),
            pl.BlockSpec((OUT_TM, n), lambda i: (i, 0)),
            pl.BlockSpec((1, n), const),
            pl.BlockSpec((1, n), const),
        ],
        out_specs=pl.BlockSpec((OUT_TM, n), lambda i: (i, 0)),
        out_shape=jax.ShapeDtypeStruct((m, n), F32),
        compiler_params=pltpu.CompilerParams(
            dimension_semantics=("parallel",),
            vmem_limit_bytes=vmem),
        name=name,
    )(y, w, resid, gain, bias)


def _time_permutation():
    r = np.arange(LRU_TM)
    p = np.zeros((LRU_TM, LRU_TM), np.float32)
    p[r, LRU_SUB * (r % SUBLANES) + r // SUBLANES] = 1.0
    return p


def _sublane_scan(a, b):
    row = lax.broadcasted_iota(jnp.int32, a.shape, 0)
    d = 1
    while d < SUBLANES:
        keep = row >= d
        a_sh = jnp.where(keep, pltpu.roll(a, d, 0), 1.0)
        b_sh = jnp.where(keep, pltpu.roll(b, d, 0), 0.0)
        b = a * b_sh + b
        a = a * a_sh
        d *= 2
    return a, b


def _lru_kernel(h_ref, perm_ref, win_ref, cw_ref, cb_ref, wax_ref, bax_ref, lam_ref, y_ref,
                xbuf, ybuf, tailbuf, hcarry):
    t = pl.program_id(1)
    tm = h_ref.shape[0]
    hist = (CONV_W - 1) * SUBLANES

    @pl.when(t == 0)
    def _():
        tailbuf[...] = jnp.zeros_like(tailbuf)
        hcarry[...] = jnp.zeros_like(hcarry)

    x = h_ref[...].astype(BF16)
    xbuf[...] = jnp.dot(perm_ref[0], x, preferred_element_type=F32).astype(BF16)

    lam = lam_ref[...]
    softplus_neg_lam = jnp.maximum(-lam, 0.0) + jnp.log1p(jnp.exp(-jnp.abs(lam)))
    log2_a_scale = (-RG_C * LOG2_E) * softplus_neg_lam
    row0 = lax.broadcasted_iota(jnp.int32, (SUBLANES, LRU_BLOCK_W), 0) == 0

    d_rnn = y_ref.shape[1]

    def project(n):
        w = jnp.concatenate(
            [win_ref[:, n * LRU_BLOCK_W:(n + 1) * LRU_BLOCK_W],
             win_ref[:, d_rnn + n * LRU_BLOCK_W:d_rnn + (n + 1) * LRU_BLOCK_W]], axis=1)
        return jnp.dot(xbuf[...], w, preferred_element_type=F32)

    def recur(n, ug):
        cols = slice(n * LRU_BLOCK_W, (n + 1) * LRU_BLOCK_W)
        u_raw = ug[:, :LRU_BLOCK_W]
        g = ug[:, LRU_BLOCK_W:]
        cur = u_raw[tm - hist:, :]
        prev = tailbuf[:, cols]
        tailbuf[:, cols] = cur
        tiles = []
        for j in range(CONV_W - 1):
            rows = slice(j * SUBLANES, (j + 1) * SUBLANES)
            tiles.append(jnp.where(row0, pltpu.roll(prev[rows], 1, 0),
                                   pltpu.roll(cur[rows], 1, 0)))
        ext = jnp.concatenate(tiles + [u_raw], axis=0)
        u = cb_ref[:, cols]
        for tap in range(CONV_W):
            start = tap * SUBLANES
            u = u + ext[start:start + tm] * cw_ref[tap:tap + 1, cols]
        gates = jnp.dot(u.astype(BF16), wax_ref[n], preferred_element_type=F32)
        gates = jax.nn.sigmoid(gates + bax_ref[n])
        r = gates[:, :LRU_BLOCK_W]
        i = gates[:, LRU_BLOCK_W:]
        a = jnp.exp2(log2_a_scale[:, cols] * r)
        one_m_a2 = 1.0 - a * a
        mult = jnp.where(one_m_a2 > 0.0, one_m_a2 * lax.rsqrt(one_m_a2), 0.0)
        b = mult * (i * u)

        hs, ps = [b[:SUBLANES]], [a[:SUBLANES]]
        for k in range(1, LRU_SUB):
            rows = slice(k * SUBLANES, (k + 1) * SUBLANES)
            hs.append(a[rows] * hs[-1] + b[rows])
            ps.append(a[rows] * ps[-1])
        p_inc, h_inc = _sublane_scan(ps[-1], hs[-1])
        h0 = hcarry[:, cols]
        row = lax.broadcasted_iota(jnp.int32, p_inc.shape, 0)
        p_exc = jnp.where(row == 0, 1.0, pltpu.roll(p_inc, 1, 0))
        h_exc = jnp.where(row == 0, 0.0, pltpu.roll(h_inc, 1, 0))
        carry = p_exc * h0 + h_exc
        hcarry[:, cols] = (p_inc * h0 + h_inc)[SUBLANES - 1:SUBLANES, :]
        h = jnp.concatenate([hk + pk * carry for hk, pk in zip(hs, ps)], axis=0)
        ybuf[:, cols] = (h * (g * jax.nn.sigmoid(g))).astype(BF16)

    pending = [project(n) for n in range(LRU_AHEAD)]
    for n in range(LRU_BLOCKS):
        ug = pending.pop(0)
        if n + LRU_AHEAD < LRU_BLOCKS:
            pending.append(project(n + LRU_AHEAD))
        recur(n, ug)

    y = jnp.dot(perm_ref[1], ybuf[...], preferred_element_type=F32)
    y_ref[...] = y.astype(y_ref.dtype)


def _lru(h, w_in, conv_w, conv_b, wax, bax, lam, batch, seq):
    m, d = h.shape
    d_rnn = lam.shape[1]
    nt = seq // LRU_TM
    perm = _time_permutation()
    perms = jnp.asarray(np.stack([perm, perm.T]), BF16)
    const = lambda b, t: (0, 0)
    const3 = lambda b, t: (0, 0, 0)
    hist = (CONV_W - 1) * SUBLANES
    block_f32 = _nbytes((LRU_TM, LRU_BLOCK_W), F32)
    vmem = _vmem_limit(
        pipelined=[_nbytes((LRU_TM, d), F32), _nbytes((LRU_TM, d_rnn), BF16),
                   _nbytes((2, LRU_TM, LRU_TM), BF16),
                   _nbytes((LRU_BLOCKS, LRU_BLOCK_W, 2 * LRU_BLOCK_W), BF16),
                   _nbytes((LRU_BLOCKS, SUBLANES, 2 * LRU_BLOCK_W), F32),
                   (CONV_W + 3 * SUBLANES) * d_rnn * 4],
        resident=[_nbytes((d, 2 * d_rnn), BF16), _nbytes((LRU_TM, d), BF16),
                  _nbytes((LRU_TM, d_rnn), BF16), _nbytes((hist + SUBLANES, d_rnn), F32)],
        temporaries=[_nbytes((LRU_TM, d), F32), (LRU_AHEAD + 1) * 2 * block_f32,
                     12 * block_f32])
    return pl.pallas_call(
        _lru_kernel,
        grid=(batch, nt),
        in_specs=[
            pl.BlockSpec((LRU_TM, d), lambda b, t: (b * nt + t, 0)),
            pl.BlockSpec((2, LRU_TM, LRU_TM), const3),
            pl.BlockSpec((d, 2 * d_rnn), const, pipeline_mode=pl.Buffered(1)),
            pl.BlockSpec((CONV_W, d_rnn), const),
            pl.BlockSpec((1, d_rnn), const),
            pl.BlockSpec((LRU_BLOCKS, LRU_BLOCK_W, 2 * LRU_BLOCK_W), const3),
            pl.BlockSpec((LRU_BLOCKS, 1, 2 * LRU_BLOCK_W), const3),
            pl.BlockSpec((1, d_rnn), const),
        ],
        out_specs=pl.BlockSpec((LRU_TM, d_rnn), lambda b, t: (b * nt + t, 0)),
        out_shape=jax.ShapeDtypeStruct((m, d_rnn), BF16),
        scratch_shapes=[
            pltpu.VMEM((LRU_TM, d), BF16),
            pltpu.VMEM((LRU_TM, d_rnn), BF16),
            pltpu.VMEM((hist, d_rnn), F32),
            pltpu.VMEM((1, d_rnn), F32),
        ],
        compiler_params=pltpu.CompilerParams(
            dimension_semantics=("parallel", "arbitrary"),
            vmem_limit_bytes=vmem),
        name="rglru",
    )(h, perms, w_in, conv_w, conv_b, wax, bax, lam)


def kernel(x, attn_w_in, attn_w_out, attn_rel_bias, lru_w_in, lru_conv_w, lru_conv_b,
           lru_wa, lru_ba, lru_wx, lru_bx, lru_lambda, lru_w_out, ln_gain, ln_bias):
    batch, seq, d_model = x.shape
    m = batch * seq
    d_att = ATT_HEADS * HEAD_DIM
    h = x.reshape(m, d_model)

    col_scale = jnp.concatenate([jnp.full((d_att,), Q_SCALE, F32),
                                 jnp.ones((3 * d_att,), F32)])[None, :]
    qkvg = _proj(h, attn_w_in[0].astype(BF16), col_scale)
    og = _attention(qkvg, _attention_bias_tiles(attn_rel_bias[0]), batch, seq)
    h = _out_ln(og, attn_w_out[0].astype(BF16), h, ln_gain[0][None, :], ln_bias[0][None, :],
                "attn_out_ln")

    wax = jnp.concatenate([lru_wa[0], lru_wx[0]], axis=-1).astype(BF16)
    bax = jnp.concatenate([lru_ba[0], lru_bx[0]], axis=-1)[:, None, :]
    y = _lru(h, lru_w_in[0].astype(BF16), lru_conv_w[0], lru_conv_b[0][None, :], wax, bax,
             lru_lambda[0][None, :], batch, seq)
    h = _out_ln(y, lru_w_out[0].astype(BF16), h, ln_gain[1][None, :], ln_bias[1][None, :],
                "lru_out_ln")
    return h.reshape(batch, seq, d_model)
```

```python
import math

import jax
import jax.numpy as jnp
import numpy as np
from jax import lax
from jax.experimental import pallas as pl
from jax.experimental.pallas import tpu as pltpu

DEPTH = 2
CHUNK = 64
LEFT_CHUNKS = 8
LEFT = LEFT_CHUNKS * CHUNK
ATT_HEADS = 16
HEAD_DIM = 128
MAX_REL_DIST = 256
LRU_BLOCKS = 16
LRU_BLOCK_W = 128
CONV_W = 4
RG_C = 8.0
LN_EPS = 1e-5
NEG_INF = -1e30
DEEPNORM_ALPHA = (2.0 * DEPTH) ** 0.25
LOG2_E = math.log2(math.e)
Q_SCALE = HEAD_DIM ** -0.5 * LOG2_E

SUBLANES = 8
MIB = 1024 * 1024
V7X_VMEM_BYTES = 64 * MIB

PROJ_TM = 1024
PROJ_TN = 2048
ATT_TQ = 256
ATT_WIN = LEFT + ATT_TQ
ATT_HB = 8
ATT_QB = 4
ATT_AHEAD = 2
ATT_ONES_ROWS = 16
ATT_PIECE = 128
OUT_TM = 512
OUT_SUB = 256
LRU_TM = 256
LRU_SB = 2
LRU_SUB = LRU_TM // SUBLANES
LRU_AHEAD = 2

F32 = jnp.float32
BF16 = jnp.bfloat16


def _nbytes(shape, dtype):
    return math.prod(shape) * jnp.dtype(dtype).itemsize


def _vmem_limit(pipelined, resident=(), temporaries=()):
    need = 2 * sum(pipelined) + sum(resident) + sum(temporaries)
    assert need <= V7X_VMEM_BYTES, f"{need} bytes of VMEM requested"
    return need


def _proj_kernel(x_ref, w_ref, s_ref, o_ref):
    x = x_ref[...].astype(BF16)
    acc = jnp.dot(x, w_ref[...], preferred_element_type=F32)
    o_ref[...] = (acc * s_ref[...]).astype(o_ref.dtype)


def _proj(x, w, col_scale):
    m, k = x.shape
    n = w.shape[1]
    vmem = _vmem_limit(
        pipelined=[_nbytes((PROJ_TM, k), F32), _nbytes((k, PROJ_TN), BF16),
                   _nbytes((SUBLANES, PROJ_TN), F32), _nbytes((PROJ_TM, PROJ_TN), BF16)],
        temporaries=[_nbytes((PROJ_TM, k), BF16), _nbytes((PROJ_TM, PROJ_TN), F32)])
    return pl.pallas_call(
        _proj_kernel,
        grid=(m // PROJ_TM, n // PROJ_TN),
        in_specs=[
            pl.BlockSpec((PROJ_TM, k), lambda i, j: (i, 0)),
            pl.BlockSpec((k, PROJ_TN), lambda i, j: (0, j)),
            pl.BlockSpec((1, PROJ_TN), lambda i, j: (0, j)),
        ],
        out_specs=pl.BlockSpec((PROJ_TM, PROJ_TN), lambda i, j: (i, j)),
        out_shape=jax.ShapeDtypeStruct((m, n), BF16),
        compiler_params=pltpu.CompilerParams(
            dimension_semantics=("parallel", "arbitrary"),
            vmem_limit_bytes=vmem),
        name="attn_in_proj",
    )(x, w, col_scale)


def _band_allowed(i, j):
    dchunk = i // CHUNK - j // CHUNK
    return (dchunk >= 0) & (dchunk <= LEFT_CHUNKS)


def _live_pieces():
    i = np.arange(ATT_TQ)[None, :]
    live = []
    for c in range(ATT_WIN // ATT_TQ):
        j = np.arange(ATT_TQ)[:, None] + c * ATT_TQ - LEFT
        ok = _band_allowed(i, j)
        n_half = ATT_TQ // ATT_PIECE
        live.append([(r, l) for r in range(n_half) for l in range(n_half)
                     if ok[r * ATT_PIECE:(r + 1) * ATT_PIECE,
                           l * ATT_PIECE:(l + 1) * ATT_PIECE].any()])
    return live


_LIVE_PIECES = _live_pieces()


def _attention_kernel(q_ref, k_ref, v_ref, g_ref, bias_ref, o_ref, vt_ref):
    n_kblk = ATT_WIN // ATT_TQ

    def head_cols(hh):
        return slice(hh * HEAD_DIM, (hh + 1) * HEAD_DIM)

    def window(qblk):
        key_blk, key_row, bias_row = [], [], []
        for c in range(n_kblk):
            rel = qblk - LEFT // ATT_TQ + c
            blk = jnp.maximum(rel, 0)
            key_blk.append(blk)
            key_row.append(pl.multiple_of(blk * ATT_TQ, ATT_TQ))
            bias_row.append(pl.multiple_of(
                jnp.where(rel >= 0, c * ATT_TQ, LEFT + ATT_WIN - ATT_TQ), ATT_TQ))
        return key_blk, key_row, bias_row

    qblks = [pl.program_id(2) * ATT_QB + j for j in range(ATT_QB)]
    windows = [window(qblk) for qblk in qblks]

    for qblk in qblks:
        q_row = pl.multiple_of(qblk * ATT_TQ, ATT_TQ)
        for hh in range(ATT_HB):
            vt_ref[hh, qblk, :HEAD_DIM, :] = v_ref[pl.ds(q_row, ATT_TQ), head_cols(hh)].T
            vt_ref[hh, qblk, HEAD_DIM:, :] = jnp.ones((ATT_ONES_ROWS, ATT_TQ), BF16)

    def scores(j, hh):
        _, key_row, bias_row = windows[j]
        q = q_ref[j * ATT_TQ:(j + 1) * ATT_TQ, head_cols(hh)]
        pieces = {}
        for c in range(n_kblk):
            k = k_ref[pl.ds(key_row[c], ATT_TQ), head_cols(hh)]
            st = lax.dot_general(k, q, (((1,), (1,)), ((), ())), preferred_element_type=F32)
            for r, l in _LIVE_PIECES[c]:
                rows = slice(r * ATT_PIECE, (r + 1) * ATT_PIECE)
                lanes = slice(l * ATT_PIECE, (l + 1) * ATT_PIECE)
                bias = bias_ref[hh, pl.ds(bias_row[c] + r * ATT_PIECE, ATT_PIECE), lanes]
                pieces[c, r, l] = st[rows, lanes] + bias
        return pieces

    def attend(j, hh, pieces):
        key_blk = windows[j][0]
        n_half = ATT_TQ // ATT_PIECE
        probs = {}
        for l in range(n_half):
            live = [key for key in pieces if key[2] == l]
            m = jnp.max(pieces[live[0]], axis=0, keepdims=True)
            for key in live[1:]:
                m = jnp.maximum(m, jnp.max(pieces[key], axis=0, keepdims=True))
            for key in live:
                probs[key] = jnp.exp2(pieces[key] - m).astype(BF16)
        zero = jnp.zeros((ATT_PIECE, ATT_PIECE), BF16)
        ot = None
        for c in range(n_kblk):
            p = jnp.concatenate(
                [jnp.concatenate([probs.get((c, r, l), zero) for l in range(n_half)], axis=1)
                 for r in range(n_half)], axis=0)
            oc = jnp.dot(vt_ref[hh, key_blk[c]], p, preferred_element_type=F32)
            ot = oc if ot is None else ot + oc
        return ot

    def finish(j, hh, ot):
        rows = slice(j * ATT_TQ, (j + 1) * ATT_TQ)
        o = (ot[:HEAD_DIM] / ot[HEAD_DIM:HEAD_DIM + 1]).T
        g = g_ref[rows, head_cols(hh)].astype(F32)
        o_ref[rows, head_cols(hh)] = (o * (g * jax.nn.sigmoid(g))).astype(o_ref.dtype)

    units = [(j, hh) for j in range(ATT_QB) for hh in range(ATT_HB)]
    pending = [scores(*u) for u in units[:ATT_AHEAD]]
    for n, unit in enumerate(units):
        pieces = pending.pop(0)
        if n + ATT_AHEAD < len(units):
            pending.append(scores(*units[n + ATT_AHEAD]))
        finish(*unit, attend(*unit, pieces))


def _attention_bias_tiles(rel_table):
    n_heads = rel_table.shape[0]
    n_rel = LEFT + ATT_WIN
    i = np.arange(ATT_TQ)[:, None]
    j = np.arange(n_rel)[None, :] - LEFT
    span = ATT_TQ + n_rel - 1
    edge = (span - (2 * MAX_REL_DIST + 1)) // 2
    r = jnp.pad(rel_table.astype(F32), ((0, 0), (edge, edge)), mode="edge")[:, ::-1]
    sq = math.isqrt(ATT_TQ)
    wide = n_rel + ATT_TQ - sq
    fine = jnp.stack([r[:, sq - 1 - t:sq - 1 - t + wide] for t in range(sq)], axis=1)
    b = jnp.stack([fine[:, :, ATT_TQ - sq * (p + 1):ATT_TQ - sq * (p + 1) + n_rel]
                   for p in range(ATT_TQ // sq)], axis=1)
    b = b.reshape(n_heads, ATT_TQ, n_rel)
    b = jnp.where(jnp.asarray(_band_allowed(i, j))[None], b * LOG2_E, NEG_INF)
    return b.transpose(0, 2, 1)


def _attention(qkvg, bias_tiles, batch, seq):
    m = qkvg.shape[0]
    d_att = ATT_HEADS * HEAD_DIM
    hw = ATT_HB * HEAD_DIM
    ncol = d_att // hw
    rows = ATT_QB * ATT_TQ
    nq = seq // rows
    vt_shape = (ATT_HB, seq // ATT_TQ, HEAD_DIM + ATT_ONES_ROWS, ATT_TQ)
    vmem = _vmem_limit(
        pipelined=[3 * _nbytes((rows, hw), BF16), 2 * _nbytes((seq, hw), BF16),
                   _nbytes((ATT_HB, LEFT + ATT_WIN, ATT_TQ), F32)],
        resident=[_nbytes(vt_shape, BF16)],
        temporaries=[(ATT_AHEAD + 1) * _nbytes((ATT_WIN, ATT_TQ), F32),
                     _nbytes((ATT_WIN, ATT_TQ), BF16)])
    return pl.pallas_call(
        _attention_kernel,
        grid=(ATT_HEADS // ATT_HB, batch, nq),
        in_specs=[
            pl.BlockSpec((rows, hw), lambda h, b, q: (b * nq + q, h)),
            pl.BlockSpec((seq, hw), lambda h, b, q: (b, ncol + h)),
            pl.BlockSpec((seq, hw), lambda h, b, q: (b, 2 * ncol + h)),
            pl.BlockSpec((rows, hw), lambda h, b, q: (b * nq + q, 3 * ncol + h)),
            pl.BlockSpec((ATT_HB, LEFT + ATT_WIN, ATT_TQ), lambda h, b, q: (h, 0, 0)),
        ],
        out_specs=pl.BlockSpec((rows, hw), lambda h, b, q: (b * nq + q, h)),
        out_shape=jax.ShapeDtypeStruct((m, d_att), BF16),
        scratch_shapes=[pltpu.VMEM(vt_shape, BF16)],
        compiler_params=pltpu.CompilerParams(
            dimension_semantics=("parallel", "parallel", "arbitrary"),
            vmem_limit_bytes=vmem),
        name="chunk_attention",
    )(qkvg, qkvg, qkvg, qkvg, bias_tiles)


def _out_ln_kernel(y_ref, w_ref, r_ref, gain_ref, bias_ref, o_ref):
    for i in range(OUT_TM // OUT_SUB):
        rows = slice(i * OUT_SUB, (i + 1) * OUT_SUB)
        acc = jnp.dot(y_ref[rows, :], w_ref[...], preferred_element_type=F32)
        z = DEEPNORM_ALPHA * r_ref[rows, :] + acc
        mu = jnp.mean(z, axis=-1, keepdims=True)
        zc = z - mu
        var = jnp.mean(zc * zc, axis=-1, keepdims=True)
        o_ref[rows, :] = zc * lax.rsqrt(var + LN_EPS) * gain_ref[...] + bias_ref[...]


def _out_ln(y, w, resid, gain, bias, name):
    m, k = y.shape
    n = w.shape[1]
    const = lambda i: (0, 0)
    vmem = _vmem_limit(
        pipelined=[_nbytes((OUT_TM, k), BF16), 2 * _nbytes((OUT_TM, n), F32),
                   2 * _nbytes((SUBLANES, n), F32)],
        resident=[_nbytes((k, n), BF16)],
        temporaries=[2 * 2 * _nbytes((OUT_SUB, n), F32)])
    return pl.pallas_call(
        _out_ln_kernel,
        grid=(m // OUT_TM,),
        in_specs=[
            pl.BlockSpec((OUT_TM, k), lambda i: (i, 0)),
            pl.BlockSpec((k, n), const, pipeline_mode=pl.Buffered(1)),
            pl.BlockSpec((OUT_TM, n), lambda i: (i, 0)),
            pl.BlockSpec((1, n), const),
            pl.BlockSpec((1, n), const),
        ],
        out_specs=pl.BlockSpec((OUT_TM, n), lambda i: (i, 0)),
        out_shape=jax.ShapeDtypeStruct((m, n), F32),
        compiler_params=pltpu.CompilerParams(
            dimension_semantics=("parallel",),
            vmem_limit_bytes=vmem),
        name=name,
    )(y, w, resid, gain, bias)


def _time_permutation():
    r = np.arange(LRU_TM)
    p = np.zeros((LRU_TM, LRU_TM), np.float32)
    p[r, LRU_SUB * (r % SUBLANES) + r // SUBLANES] = 1.0
    return p


def _sublane_scan(a, b):
    row = lax.broadcasted_iota(jnp.int32, a.shape, 0)
    d = 1
    while d < SUBLANES:
        keep = row >= d
        a_sh = jnp.where(keep, pltpu.roll(a, d, 0), 1.0)
        b_sh = jnp.where(keep, pltpu.roll(b, d, 0), 0.0)
        b = a * b_sh + b
        a = a * a_sh
        d *= 2
    return a, b


def _lru_kernel(h_ref, perm_ref, win_ref, cw_ref, cb_ref, wax_ref, bax_ref, lam_ref, y_ref,
                xbuf, ybuf, tailbuf, hcarry):
    t = pl.program_id(1)
    tm = LRU_TM
    hist = (CONV_W - 1) * SUBLANES

    @pl.when(t == 0)
    def _():
        tailbuf[...] = jnp.zeros_like(tailbuf)
        hcarry[...] = jnp.zeros_like(hcarry)

    for sb in range(LRU_SB):
        x = h_ref[sb * tm:(sb + 1) * tm, :].astype(BF16)
        xbuf[sb] = jnp.dot(perm_ref[0], x, preferred_element_type=F32).astype(BF16)

    lam = lam_ref[...]
    softplus_neg_lam = jnp.maximum(-lam, 0.0) + jnp.log1p(jnp.exp(-jnp.abs(lam)))
    log2_a_scale = (-RG_C * LOG2_E) * softplus_neg_lam
    row0 = lax.broadcasted_iota(jnp.int32, (SUBLANES, LRU_BLOCK_W), 0) == 0

    d_rnn = y_ref.shape[1]

    def project(sb, n):
        w = jnp.concatenate(
            [win_ref[:, n * LRU_BLOCK_W:(n + 1) * LRU_BLOCK_W],
             win_ref[:, d_rnn + n * LRU_BLOCK_W:d_rnn + (n + 1) * LRU_BLOCK_W]], axis=1)
        return jnp.dot(xbuf[sb], w, preferred_element_type=F32)

    def recur(sb, n, ug):
        cols = slice(n * LRU_BLOCK_W, (n + 1) * LRU_BLOCK_W)
        u_raw = ug[:, :LRU_BLOCK_W]
        g = ug[:, LRU_BLOCK_W:]
        cur = u_raw[tm - hist:, :]
        prev = tailbuf[:, cols]
        tailbuf[:, cols] = cur
        tiles = []
        for j in range(CONV_W - 1):
            rows = slice(j * SUBLANES, (j + 1) * SUBLANES)
            tiles.append(jnp.where(row0, pltpu.roll(prev[rows], 1, 0),
                                   pltpu.roll(cur[rows], 1, 0)))
        ext = jnp.concatenate(tiles + [u_raw], axis=0)
        u = cb_ref[:, cols]
        for tap in range(CONV_W):
            start = tap * SUBLANES
            u = u + ext[start:start + tm] * cw_ref[tap:tap + 1, cols]
        gates = jnp.dot(u.astype(BF16), wax_ref[n], preferred_element_type=F32)
        gates = jax.nn.sigmoid(gates + bax_ref[n])
        r = gates[:, :LRU_BLOCK_W]
        i = gates[:, LRU_BLOCK_W:]
        a = jnp.exp2(log2_a_scale[:, cols] * r)
        one_m_a2 = 1.0 - a * a
        mult = jnp.where(one_m_a2 > 0.0, one_m_a2 * lax.rsqrt(one_m_a2), 0.0)
        b = mult * (i * u)

        hs, ps = [b[:SUBLANES]], [a[:SUBLANES]]
        for k in range(1, LRU_SUB):
            rows = slice(k * SUBLANES, (k + 1) * SUBLANES)
            hs.append(a[rows] * hs[-1] + b[rows])
            ps.append(a[rows] * ps[-1])
        p_inc, h_inc = _sublane_scan(ps[-1], hs[-1])
        h0 = hcarry[:, cols]
        row = lax.broadcasted_iota(jnp.int32, p_inc.shape, 0)
        p_exc = jnp.where(row == 0, 1.0, pltpu.roll(p_inc, 1, 0))
        h_exc = jnp.where(row == 0, 0.0, pltpu.roll(h_inc, 1, 0))
        carry = p_exc * h0 + h_exc
        hcarry[:, cols] = (p_inc * h0 + h_inc)[SUBLANES - 1:SUBLANES, :]
        h = jnp.concatenate([hk + pk * carry for hk, pk in zip(hs, ps)], axis=0)
        ybuf[sb, :, cols] = (h * (g * jax.nn.sigmoid(g))).astype(BF16)

    units = [(sb, n) for sb in range(LRU_SB) for n in range(LRU_BLOCKS)]
    pending = [project(*u) for u in units[:LRU_AHEAD]]
    for i, unit in enumerate(units):
        ug = pending.pop(0)
        if i + LRU_AHEAD < len(units):
            pending.append(project(*units[i + LRU_AHEAD]))
        recur(*unit, ug)

    for sb in range(LRU_SB):
        y = jnp.dot(perm_ref[1], ybuf[sb], preferred_element_type=F32)
        y_ref[sb * tm:(sb + 1) * tm, :] = y.astype(y_ref.dtype)


def _lru(h, w_in, conv_w, conv_b, wax, bax, lam, batch, seq):
    m, d = h.shape
    d_rnn = lam.shape[1]
    rows = LRU_SB * LRU_TM
    nt = seq // rows
    perm = _time_permutation()
    perms = jnp.asarray(np.stack([perm, perm.T]), BF16)
    const = lambda b, t: (0, 0)
    const3 = lambda b, t: (0, 0, 0)
    hist = (CONV_W - 1) * SUBLANES
    block_f32 = _nbytes((LRU_TM, LRU_BLOCK_W), F32)
    vmem = _vmem_limit(
        pipelined=[_nbytes((rows, d), F32), _nbytes((rows, d_rnn), BF16),
                   _nbytes((2, LRU_TM, LRU_TM), BF16),
                   _nbytes((LRU_BLOCKS, LRU_BLOCK_W, 2 * LRU_BLOCK_W), BF16),
                   _nbytes((LRU_BLOCKS, SUBLANES, 2 * LRU_BLOCK_W), F32),
                   (CONV_W + 3 * SUBLANES) * d_rnn * 4],
        resident=[_nbytes((d, 2 * d_rnn), BF16), _nbytes((rows, d), BF16),
                  _nbytes((rows, d_rnn), BF16), _nbytes((hist + SUBLANES, d_rnn), F32)],
        temporaries=[_nbytes((LRU_TM, d), F32), (LRU_AHEAD + 1) * 2 * block_f32,
                     12 * block_f32])
    return pl.pallas_call(
        _lru_kernel,
        grid=(batch, nt),
        in_specs=[
            pl.BlockSpec((rows, d), lambda b, t: (b * nt + t, 0)),
            pl.BlockSpec((2, LRU_TM, LRU_TM), const3),
            pl.BlockSpec((d, 2 * d_rnn), const, pipeline_mode=pl.Buffered(1)),
            pl.BlockSpec((CONV_W, d_rnn), const),
            pl.BlockSpec((1, d_rnn), const),
            pl.BlockSpec((LRU_BLOCKS, LRU_BLOCK_W, 2 * LRU_BLOCK_W), const3),
            pl.BlockSpec((LRU_BLOCKS, 1, 2 * LRU_BLOCK_W), const3),
            pl.BlockSpec((1, d_rnn), const),
        ],
        out_specs=pl.BlockSpec((rows, d_rnn), lambda b, t: (b * nt + t, 0)),
        out_shape=jax.ShapeDtypeStruct((m, d_rnn), BF16),
        scratch_shapes=[
            pltpu.VMEM((LRU_SB, LRU_TM, d), BF16),
            pltpu.VMEM((LRU_SB, LRU_TM, d_rnn), BF16),
            pltpu.VMEM((hist, d_rnn), F32),
            pltpu.VMEM((1, d_rnn), F32),
        ],
        compiler_params=pltpu.CompilerParams(
            dimension_semantics=("parallel", "arbitrary"),
            vmem_limit_bytes=vmem),
        name="rglru",
    )(h, perms, w_in, conv_w, conv_b, wax, bax, lam)


def kernel(x, attn_w_in, attn_w_out, attn_rel_bias, lru_w_in, lru_conv_w, lru_conv_b,
           lru_wa, lru_ba, lru_wx, lru_bx, lru_lambda, lru_w_out, ln_gain, ln_bias):
    batch, seq, d_model = x.shape
    m = batch * seq
    d_att = ATT_HEADS * HEAD_DIM
    h = x.reshape(m, d_model)

    col_scale = jnp.concatenate([jnp.full((d_att,), Q_SCALE, F32),
                                 jnp.ones((3 * d_att,), F32)])[None, :]
    qkvg = _proj(h, attn_w_in[0].astype(BF16), col_scale)
    og = _attention(qkvg, _attention_bias_tiles(attn_rel_bias[0]), batch, seq)
    h = _out_ln(og, attn_w_out[0].astype(BF16), h, ln_gain[0][None, :], ln_bias[0][None, :],
                "attn_out_ln")

    wax = jnp.concatenate([lru_wa[0], lru_wx[0]], axis=-1).astype(BF16)
    bax = jnp.concatenate([lru_ba[0], lru_bx[0]], axis=-1)[:, None, :]
    y = _lru(h, lru_w_in[0].astype(BF16), lru_conv_w[0], lru_conv_b[0][None, :], wax, bax,
             lru_lambda[0][None, :], batch, seq)
    h = _out_ln(y, lru_w_out[0].astype(BF16), h, ln_gain[1][None, :], ln_bias[1][None, :],
                "lru_out_ln")
    return h.reshape(batch, seq, d_model)
```

```python
import math

import jax
import jax.numpy as jnp
import numpy as np
from jax import lax
from jax.experimental import pallas as pl
from jax.experimental.pallas import tpu as pltpu

DEPTH = 2
CHUNK = 64
LEFT_CHUNKS = 8
LEFT = LEFT_CHUNKS * CHUNK
ATT_HEADS = 16
HEAD_DIM = 128
MAX_REL_DIST = 256
LRU_BLOCKS = 16
LRU_BLOCK_W = 128
CONV_W = 4
RG_C = 8.0
LN_EPS = 1e-5
NEG_INF = -1e30
DEEPNORM_ALPHA = (2.0 * DEPTH) ** 0.25
LOG2_E = math.log2(math.e)
Q_SCALE = HEAD_DIM ** -0.5 * LOG2_E

SUBLANES = 8
MIB = 1024 * 1024
V7X_VMEM_BYTES = 64 * MIB

PROJ_TM = 1024
PROJ_TN = 2048
ATT_TQ = 256
ATT_WIN = LEFT + ATT_TQ
ATT_HB = 8
ATT_QB = 4
ATT_AHEAD = 2
ATT_ONES_ROWS = 16
ATT_PIECE = 128
OUT_TM = 512
OUT_SUB = 256
LRU_TM = 256
LRU_SB = 2
LRU_SUB = LRU_TM // SUBLANES
LRU_AHEAD = 2

F32 = jnp.float32
BF16 = jnp.bfloat16


def _nbytes(shape, dtype):
    return math.prod(shape) * jnp.dtype(dtype).itemsize


def _vmem_limit(pipelined, resident=(), temporaries=()):
    need = 2 * sum(pipelined) + sum(resident) + sum(temporaries)
    assert need <= V7X_VMEM_BYTES, f"{need} bytes of VMEM requested"
    return need


def _proj_kernel(x_ref, w_ref, s_ref, o_ref):
    x = x_ref[...].astype(BF16)
    acc = jnp.dot(x, w_ref[...], preferred_element_type=F32)
    o_ref[...] = (acc * s_ref[...]).astype(o_ref.dtype)


def _proj(x, w, col_scale):
    m, k = x.shape
    n = w.shape[1]
    vmem = _vmem_limit(
        pipelined=[_nbytes((PROJ_TM, k), F32), _nbytes((k, PROJ_TN), BF16),
                   _nbytes((SUBLANES, PROJ_TN), F32), _nbytes((PROJ_TM, PROJ_TN), BF16)],
        temporaries=[_nbytes((PROJ_TM, k), BF16), _nbytes((PROJ_TM, PROJ_TN), F32)])
    return pl.pallas_call(
        _proj_kernel,
        grid=(m // PROJ_TM, n // PROJ_TN),
        in_specs=[
            pl.BlockSpec((PROJ_TM, k), lambda i, j: (i, 0)),
            pl.BlockSpec((k, PROJ_TN), lambda i, j: (0, j)),
            pl.BlockSpec((1, PROJ_TN), lambda i, j: (0, j)),
        ],
        out_specs=pl.BlockSpec((PROJ_TM, PROJ_TN), lambda i, j: (i, j)),
        out_shape=jax.ShapeDtypeStruct((m, n), BF16),
        compiler_params=pltpu.CompilerParams(
            dimension_semantics=("parallel", "arbitrary"),
            vmem_limit_bytes=vmem),
        name="attn_in_proj",
    )(x, w, col_scale)


def _band_allowed(i, j):
    dchunk = i // CHUNK - j // CHUNK
    return (dchunk >= 0) & (dchunk <= LEFT_CHUNKS)


def _live_pieces():
    i = np.arange(ATT_TQ)[None, :]
    live = []
    for c in range(ATT_WIN // ATT_TQ):
        j = np.arange(ATT_TQ)[:, None] + c * ATT_TQ - LEFT
        ok = _band_allowed(i, j)
        n_half = ATT_TQ // ATT_PIECE
        live.append([(r, l) for r in range(n_half) for l in range(n_half)
                     if ok[r * ATT_PIECE:(r + 1) * ATT_PIECE,
                           l * ATT_PIECE:(l + 1) * ATT_PIECE].any()])
    return live


_LIVE_PIECES = _live_pieces()


def _attention_kernel(q_ref, k_ref, v_ref, g_ref, bias_ref, o_ref, vt_ref):
    n_kblk = ATT_WIN // ATT_TQ

    def head_cols(hh):
        return slice(hh * HEAD_DIM, (hh + 1) * HEAD_DIM)

    def window(qblk):
        key_blk, key_row, bias_row = [], [], []
        for c in range(n_kblk):
            rel = qblk - LEFT // ATT_TQ + c
            blk = jnp.maximum(rel, 0)
            key_blk.append(blk)
            key_row.append(pl.multiple_of(blk * ATT_TQ, ATT_TQ))
            bias_row.append(pl.multiple_of(
                jnp.where(rel >= 0, c * ATT_TQ, LEFT + ATT_WIN - ATT_TQ), ATT_TQ))
        return key_blk, key_row, bias_row

    qblks = [pl.program_id(2) * ATT_QB + j for j in range(ATT_QB)]
    windows = [window(qblk) for qblk in qblks]

    for qblk in qblks:
        q_row = pl.multiple_of(qblk * ATT_TQ, ATT_TQ)
        for hh in range(ATT_HB):
            vt_ref[hh, qblk, :HEAD_DIM, :] = v_ref[pl.ds(q_row, ATT_TQ), head_cols(hh)].T
            vt_ref[hh, qblk, HEAD_DIM:, :] = jnp.ones((ATT_ONES_ROWS, ATT_TQ), BF16)

    def scores(j, hh):
        _, key_row, bias_row = windows[j]
        q = q_ref[j * ATT_TQ:(j + 1) * ATT_TQ, head_cols(hh)]
        pieces = {}
        for c in range(n_kblk):
            k = k_ref[pl.ds(key_row[c], ATT_TQ), head_cols(hh)]
            st = lax.dot_general(k, q, (((1,), (1,)), ((), ())), preferred_element_type=F32)
            for r, l in _LIVE_PIECES[c]:
                rows = slice(r * ATT_PIECE, (r + 1) * ATT_PIECE)
                lanes = slice(l * ATT_PIECE, (l + 1) * ATT_PIECE)
                bias = bias_ref[hh, pl.ds(bias_row[c] + r * ATT_PIECE, ATT_PIECE), lanes]
                pieces[c, r, l] = st[rows, lanes] + bias
        return pieces

    def attend(j, hh, pieces):
        key_blk = windows[j][0]
        n_half = ATT_TQ // ATT_PIECE
        probs = {}
        for l in range(n_half):
            live = [key for key in pieces if key[2] == l]
            m = jnp.max(pieces[live[0]], axis=0, keepdims=True)
            for key in live[1:]:
                m = jnp.maximum(m, jnp.max(pieces[key], axis=0, keepdims=True))
            for key in live:
                probs[key] = jnp.exp2(pieces[key] - m).astype(BF16)
        zero = jnp.zeros((ATT_PIECE, ATT_PIECE), BF16)
        ot = None
        for c in range(n_kblk):
            p = jnp.concatenate(
                [jnp.concatenate([probs.get((c, r, l), zero) for l in range(n_half)], axis=1)
                 for r in range(n_half)], axis=0)
            oc = jnp.dot(vt_ref[hh, key_blk[c]], p, preferred_element_type=F32)
            ot = oc if ot is None else ot + oc
        return ot

    def finish(j, hh, ot):
        rows = slice(j * ATT_TQ, (j + 1) * ATT_TQ)
        o = (ot[:HEAD_DIM] / ot[HEAD_DIM:HEAD_DIM + 1]).T
        g = g_ref[rows, head_cols(hh)].astype(F32)
        o_ref[rows, head_cols(hh)] = (o * (g * jax.nn.sigmoid(g))).astype(o_ref.dtype)

    units = [(j, hh) for j in range(ATT_QB) for hh in range(ATT_HB)]
    pending = [scores(*u) for u in units[:ATT_AHEAD]]
    for n, unit in enumerate(units):
        pieces = pending.pop(0)
        if n + ATT_AHEAD < len(units):
            pending.append(scores(*units[n + ATT_AHEAD]))
        finish(*unit, attend(*unit, pieces))


def _attention_bias_tiles(rel_table):
    n_heads = rel_table.shape[0]
    n_rel = LEFT + ATT_WIN
    i = np.arange(ATT_TQ)[:, None]
    j = np.arange(n_rel)[None, :] - LEFT
    span = ATT_TQ + n_rel - 1
    edge = (span - (2 * MAX_REL_DIST + 1)) // 2
    r = jnp.pad(rel_table.astype(F32), ((0, 0), (edge, edge)), mode="edge")[:, ::-1]
    sq = math.isqrt(ATT_TQ)
    wide = n_rel + ATT_TQ - sq
    fine = jnp.stack([r[:, sq - 1 - t:sq - 1 - t + wide] for t in range(sq)], axis=1)
    b = jnp.stack([fine[:, :, ATT_TQ - sq * (p + 1):ATT_TQ - sq * (p + 1) + n_rel]
                   for p in range(ATT_TQ // sq)], axis=1)
    b = b.reshape(n_heads, ATT_TQ, n_rel)
    b = jnp.where(jnp.asarray(_band_allowed(i, j))[None], b * LOG2_E, NEG_INF)
    return b.transpose(0, 2, 1)


def _attention(qkvg, bias_tiles, batch, seq):
    m = qkvg.shape[0]
    d_att = ATT_HEADS * HEAD_DIM
    hw = ATT_HB * HEAD_DIM
    ncol = d_att // hw
    rows = ATT_QB * ATT_TQ
    nq = seq // rows
    vt_shape = (ATT_HB, seq // ATT_TQ, HEAD_DIM + ATT_ONES_ROWS, ATT_TQ)
    vmem = _vmem_limit(
        pipelined=[3 * _nbytes((rows, hw), BF16), 2 * _nbytes((seq, hw), BF16),
                   _nbytes((ATT_HB, LEFT + ATT_WIN, ATT_TQ), F32)],
        resident=[_nbytes(vt_shape, BF16)],
        temporaries=[(ATT_AHEAD + 1) * _nbytes((ATT_WIN, ATT_TQ), F32),
                     _nbytes((ATT_WIN, ATT_TQ), BF16)])
    return pl.pallas_call(
        _attention_kernel,
        grid=(ATT_HEADS // ATT_HB, batch, nq),
        in_specs=[
            pl.BlockSpec((rows, hw), lambda h, b, q: (b * nq + q, h)),
            pl.BlockSpec((seq, hw), lambda h, b, q: (b, ncol + h)),
            pl.BlockSpec((seq, hw), lambda h, b, q: (b, 2 * ncol + h)),
            pl.BlockSpec((rows, hw), lambda h, b, q: (b * nq + q, 3 * ncol + h)),
            pl.BlockSpec((ATT_HB, LEFT + ATT_WIN, ATT_TQ), lambda h, b, q: (h, 0, 0)),
        ],
        out_specs=pl.BlockSpec((rows, hw), lambda h, b, q: (b * nq + q, h)),
        out_shape=jax.ShapeDtypeStruct((m, d_att), BF16),
        scratch_shapes=[pltpu.VMEM(vt_shape, BF16)],
        compiler_params=pltpu.CompilerParams(
            dimension_semantics=("parallel", "parallel", "arbitrary"),
            vmem_limit_bytes=vmem),
        name="chunk_attention",
    )(qkvg, qkvg, qkvg, qkvg, bias_tiles)


def _out_ln_kernel(y_ref, w_ref, r_ref, gain_ref, bias_ref, o_ref):
    for i in range(OUT_TM // OUT_SUB):
        rows = slice(i * OUT_SUB, (i + 1) * OUT_SUB)
        acc = jnp.dot(y_ref[rows, :], w_ref[...], preferred_element_type=F32)
        z = DEEPNORM_ALPHA * r_ref[rows, :] + acc
        mu = jnp.mean(z, axis=-1, keepdims=True)
        zc = z - mu
        var = jnp.mean(zc * zc, axis=-1, keepdims=True)
        o_ref[rows, :] = zc * lax.rsqrt(var + LN_EPS) * gain_ref[...] + bias_ref[...]


def _out_ln(y, w, resid, gain, bias, name):
    m, k = y.shape
    n = w.shape[1]
    const = lambda i: (0, 0)
    vmem = _vmem_limit(
        pipelined=[_nbytes((OUT_TM, k), BF16), 2 * _nbytes((OUT_TM, n), F32),
                   2 * _nbytes((SUBLANES, n), F32)],
        resident=[_nbytes((k, n), BF16)],
        temporaries=[2 * 2 * _nbytes((OUT_SUB, n), F32)])
    return pl.pallas_call(
        _out_ln_kernel,
        grid=(m // OUT_TM,),
        in_specs=[
            pl.BlockSpec((OUT_TM, k), lambda i: (i, 0)),
            pl.BlockSpec((k, n), const, pipeline_mode=pl.Buffered(1)),
            pl.BlockSpec((OUT_TM, n), lambda i: (i, 0)),
            pl.BlockSpec((1, n), const),
            pl.BlockSpec((1, n), const),
        ],
        out_specs=pl.BlockSpec((OUT_TM, n), lambda i: (i, 0)),
        out_shape=jax.ShapeDtypeStruct((m, n), F32),
        compiler_params=pltpu.CompilerParams(
            dimension_semantics=("parallel",),
            vmem_limit_bytes=vmem),
        name=name,
    )(y, w, resid, gain, bias)


def _time_permutation():
    r = np.arange(LRU_TM)
    p = np.zeros((LRU_TM, LRU_TM), np.float32)
    p[r, LRU_SUB * (r % SUBLANES) + r // SUBLANES] = 1.0
    return p


def _sublane_scan(a, b):
    row = lax.broadcasted_iota(jnp.int32, a.shape, 0)
    d = 1
    while d < SUBLANES:
        keep = row >= d
        a_sh = jnp.where(keep, pltpu.roll(a, d, 0), 1.0)
        b_sh = jnp.where(keep, pltpu.roll(b, d, 0), 0.0)
        b = a * b_sh + b
        a = a * a_sh
        d *= 2
    return a, b


def _lru_kernel(h_ref, perm_ref, win_ref, cw_ref, cb_ref, wax_ref, bax_ref, lam_ref, y_ref,
                xbuf, ybuf, tailbuf, hcarry):
    t = pl.program_id(1)
    tm = LRU_TM
    hist = (CONV_W - 1) * SUBLANES

    @pl.when(t == 0)
    def _():
        tailbuf[...] = jnp.zeros_like(tailbuf)
        hcarry[...] = jnp.zeros_like(hcarry)

    for sb in range(LRU_SB):
        x = h_ref[sb * tm:(sb + 1) * tm, :].astype(BF16)
        xbuf[sb * tm:(sb + 1) * tm, :] = jnp.dot(
            perm_ref[0], x, preferred_element_type=F32).astype(BF16)

    lam = lam_ref[...]
    softplus_neg_lam = jnp.maximum(-lam, 0.0) + jnp.log1p(jnp.exp(-jnp.abs(lam)))
    log2_a_scale = (-RG_C * LOG2_E) * softplus_neg_lam
    row0 = lax.broadcasted_iota(jnp.int32, (SUBLANES, LRU_BLOCK_W), 0) == 0

    d_rnn = y_ref.shape[1]

    def project(n):
        w = jnp.concatenate(
            [win_ref[:, n * LRU_BLOCK_W:(n + 1) * LRU_BLOCK_W],
             win_ref[:, d_rnn + n * LRU_BLOCK_W:d_rnn + (n + 1) * LRU_BLOCK_W]], axis=1)
        return jnp.dot(xbuf[...], w, preferred_element_type=F32)

    def recur(sb, n, ug):
        cols = slice(n * LRU_BLOCK_W, (n + 1) * LRU_BLOCK_W)
        u_raw = ug[:, :LRU_BLOCK_W]
        g = ug[:, LRU_BLOCK_W:]
        cur = u_raw[tm - hist:, :]
        prev = tailbuf[:, cols]
        tailbuf[:, cols] = cur
        tiles = []
        for j in range(CONV_W - 1):
            rows = slice(j * SUBLANES, (j + 1) * SUBLANES)
            tiles.append(jnp.where(row0, pltpu.roll(prev[rows], 1, 0),
                                   pltpu.roll(cur[rows], 1, 0)))
        ext = jnp.concatenate(tiles + [u_raw], axis=0)
        u = cb_ref[:, cols]
        for tap in range(CONV_W):
            start = tap * SUBLANES
            u = u + ext[start:start + tm] * cw_ref[tap:tap + 1, cols]
        gates = jnp.dot(u.astype(BF16), wax_ref[n], preferred_element_type=F32)
        gates = jax.nn.sigmoid(gates + bax_ref[n])
        r = gates[:, :LRU_BLOCK_W]
        i = gates[:, LRU_BLOCK_W:]
        a = jnp.exp2(log2_a_scale[:, cols] * r)
        one_m_a2 = 1.0 - a * a
        mult = jnp.where(one_m_a2 > 0.0, one_m_a2 * lax.rsqrt(one_m_a2), 0.0)
        b = mult * (i * u)

        hs, ps = [b[:SUBLANES]], [a[:SUBLANES]]
        for k in range(1, LRU_SUB):
            rows = slice(k * SUBLANES, (k + 1) * SUBLANES)
            hs.append(a[rows] * hs[-1] + b[rows])
            ps.append(a[rows] * ps[-1])
        p_inc, h_inc = _sublane_scan(ps[-1], hs[-1])
        h0 = hcarry[:, cols]
        row = lax.broadcasted_iota(jnp.int32, p_inc.shape, 0)
        p_exc = jnp.where(row == 0, 1.0, pltpu.roll(p_inc, 1, 0))
        h_exc = jnp.where(row == 0, 0.0, pltpu.roll(h_inc, 1, 0))
        carry = p_exc * h0 + h_exc
        hcarry[:, cols] = (p_inc * h0 + h_inc)[SUBLANES - 1:SUBLANES, :]
        h = jnp.concatenate([hk + pk * carry for hk, pk in zip(hs, ps)], axis=0)
        ybuf[sb, :, cols] = (h * (g * jax.nn.sigmoid(g))).astype(BF16)

    pending = [project(n) for n in range(LRU_AHEAD)]
    for n in range(LRU_BLOCKS):
        ug = pending.pop(0)
        if n + LRU_AHEAD < LRU_BLOCKS:
            pending.append(project(n + LRU_AHEAD))
        for sb in range(LRU_SB):
            recur(sb, n, ug[sb * tm:(sb + 1) * tm])

    for sb in range(LRU_SB):
        y = jnp.dot(perm_ref[1], ybuf[sb], preferred_element_type=F32)
        y_ref[sb * tm:(sb + 1) * tm, :] = y.astype(y_ref.dtype)


def _lru(h, w_in, conv_w, conv_b, wax, bax, lam, batch, seq):
    m, d = h.shape
    d_rnn = lam.shape[1]
    rows = LRU_SB * LRU_TM
    nt = seq // rows
    perm = _time_permutation()
    perms = jnp.asarray(np.stack([perm, perm.T]), BF16)
    const = lambda b, t: (0, 0)
    const3 = lambda b, t: (0, 0, 0)
    hist = (CONV_W - 1) * SUBLANES
    block_f32 = _nbytes((LRU_TM, LRU_BLOCK_W), F32)
    vmem = _vmem_limit(
        pipelined=[_nbytes((rows, d), F32), _nbytes((rows, d_rnn), BF16),
                   _nbytes((2, LRU_TM, LRU_TM), BF16),
                   _nbytes((LRU_BLOCKS, LRU_BLOCK_W, 2 * LRU_BLOCK_W), BF16),
                   _nbytes((LRU_BLOCKS, SUBLANES, 2 * LRU_BLOCK_W), F32),
                   (CONV_W + 3 * SUBLANES) * d_rnn * 4],
        resident=[_nbytes((d, 2 * d_rnn), BF16), _nbytes((rows, d), BF16),
                  _nbytes((rows, d_rnn), BF16), _nbytes((hist + SUBLANES, d_rnn), F32)],
        temporaries=[_nbytes((LRU_TM, d), F32), (LRU_AHEAD + 1) * 2 * block_f32,
                     12 * block_f32])
    return pl.pallas_call(
        _lru_kernel,
        grid=(batch, nt),
        in_specs=[
            pl.BlockSpec((rows, d), lambda b, t: (b * nt + t, 0)),
            pl.BlockSpec((2, LRU_TM, LRU_TM), const3),
            pl.BlockSpec((d, 2 * d_rnn), const, pipeline_mode=pl.Buffered(1)),
            pl.BlockSpec((CONV_W, d_rnn), const),
            pl.BlockSpec((1, d_rnn), const),
            pl.BlockSpec((LRU_BLOCKS, LRU_BLOCK_W, 2 * LRU_BLOCK_W), const3),
            pl.BlockSpec((LRU_BLOCKS, 1, 2 * LRU_BLOCK_W), const3),
            pl.BlockSpec((1, d_rnn), const),
        ],
        out_specs=pl.BlockSpec((rows, d_rnn), lambda b, t: (b * nt + t, 0)),
        out_shape=jax.ShapeDtypeStruct((m, d_rnn), BF16),
        scratch_shapes=[
            pltpu.VMEM((rows, d), BF16),
            pltpu.VMEM((LRU_SB, LRU_TM, d_rnn), BF16),
            pltpu.VMEM((hist, d_rnn), F32),
            pltpu.VMEM((1, d_rnn), F32),
        ],
        compiler_params=pltpu.CompilerParams(
            dimension_semantics=("parallel", "arbitrary"),
            vmem_limit_bytes=vmem),
        name="rglru",
    )(h, perms, w_in, conv_w, conv_b, wax, bax, lam)


def kernel(x, attn_w_in, attn_w_out, attn_rel_bias, lru_w_in, lru_conv_w, lru_conv_b,
           lru_wa, lru_ba, lru_wx, lru_bx, lru_lambda, lru_w_out, ln_gain, ln_bias):
    batch, seq, d_model = x.shape
    m = batch * seq
    d_att = ATT_HEADS * HEAD_DIM
    h = x.reshape(m, d_model)

    col_scale = jnp.concatenate([jnp.full((d_att,), Q_SCALE, F32),
                                 jnp.ones((3 * d_att,), F32)])[None, :]
    qkvg = _proj(h, attn_w_in[0].astype(BF16), col_scale)
    og = _attention(qkvg, _attention_bias_tiles(attn_rel_bias[0]), batch, seq)
    h = _out_ln(og, attn_w_out[0].astype(BF16), h, ln_gain[0][None, :], ln_bias[0][None, :],
                "attn_out_ln")

    wax = jnp.concatenate([lru_wa[0], lru_wx[0]], axis=-1).astype(BF16)
    bax = jnp.concatenate([lru_ba[0], lru_bx[0]], axis=-1)[:, None, :]
    y = _lru(h, lru_w_in[0].astype(BF16), lru_conv_w[0], lru_conv_b[0][None, :], wax, bax,
             lru_lambda[0][None, :], batch, seq)
    h = _out_ln(y, lru_w_out[0].astype(BF16), h, ln_gain[1][None, :], ln_bias[1][None, :],
                "lru_out_ln")
    return h.reshape(batch, seq, d_model)
```

```python
import math

import jax
import jax.numpy as jnp
import numpy as np
from jax import lax
from jax.experimental import pallas as pl
from jax.experimental.pallas import tpu as pltpu

DEPTH = 2
CHUNK = 64
LEFT_CHUNKS = 8
LEFT = LEFT_CHUNKS * CHUNK
ATT_HEADS = 16
HEAD_DIM = 128
MAX_REL_DIST = 256
LRU_BLOCKS = 16
LRU_BLOCK_W = 128
CONV_W = 4
RG_C = 8.0
LN_EPS = 1e-5
NEG_INF = -1e30
DEEPNORM_ALPHA = (2.0 * DEPTH) ** 0.25
LOG2_E = math.log2(math.e)
Q_SCALE = HEAD_DIM ** -0.5 * LOG2_E

SUBLANES = 8
MIB = 1024 * 1024
V7X_VMEM_BYTES = 64 * MIB

PROJ_TM = 1024
PROJ_TN = 2048
ATT_TQ = 256
ATT_WIN = LEFT + ATT_TQ
ATT_HB = 8
ATT_QB = 4
ATT_AHEAD = 2
ATT_ONES_ROWS = 16
ATT_PIECE = 128
OUT_TM = 512
OUT_SUB = 256
LRU_TM = 256
LRU_SB = 2
LRU_SUB = LRU_TM // SUBLANES
LRU_AHEAD = 2

F32 = jnp.float32
BF16 = jnp.bfloat16


def _nbytes(shape, dtype):
    return math.prod(shape) * jnp.dtype(dtype).itemsize


def _vmem_limit(pipelined, resident=(), temporaries=()):
    need = 2 * sum(pipelined) + sum(resident) + sum(temporaries)
    assert need <= V7X_VMEM_BYTES, f"{need} bytes of VMEM requested"
    return need


def _proj_kernel(x_ref, w_ref, s_ref, o_ref):
    x = x_ref[...].astype(BF16)
    acc = jnp.dot(x, w_ref[...], preferred_element_type=F32)
    o_ref[...] = (acc * s_ref[...]).astype(o_ref.dtype)


def _proj(x, w, col_scale):
    m, k = x.shape
    n = w.shape[1]
    vmem = _vmem_limit(
        pipelined=[_nbytes((PROJ_TM, k), F32), _nbytes((k, PROJ_TN), BF16),
                   _nbytes((SUBLANES, PROJ_TN), F32), _nbytes((PROJ_TM, PROJ_TN), BF16)],
        temporaries=[_nbytes((PROJ_TM, k), BF16), _nbytes((PROJ_TM, PROJ_TN), F32)])
    return pl.pallas_call(
        _proj_kernel,
        grid=(m // PROJ_TM, n // PROJ_TN),
        in_specs=[
            pl.BlockSpec((PROJ_TM, k), lambda i, j: (i, 0)),
            pl.BlockSpec((k, PROJ_TN), lambda i, j: (0, j)),
            pl.BlockSpec((1, PROJ_TN), lambda i, j: (0, j)),
        ],
        out_specs=pl.BlockSpec((PROJ_TM, PROJ_TN), lambda i, j: (i, j)),
        out_shape=jax.ShapeDtypeStruct((m, n), BF16),
        compiler_params=pltpu.CompilerParams(
            dimension_semantics=("parallel", "arbitrary"),
            vmem_limit_bytes=vmem),
        name="attn_in_proj",
    )(x, w, col_scale)


def _band_allowed(i, j):
    dchunk = i // CHUNK - j // CHUNK
    return (dchunk >= 0) & (dchunk <= LEFT_CHUNKS)


def _live_pieces():
    i = np.arange(ATT_TQ)[None, :]
    live = []
    for c in range(ATT_WIN // ATT_TQ):
        j = np.arange(ATT_TQ)[:, None] + c * ATT_TQ - LEFT
        ok = _band_allowed(i, j)
        n_half = ATT_TQ // ATT_PIECE
        live.append([(r, l) for r in range(n_half) for l in range(n_half)
                     if ok[r * ATT_PIECE:(r + 1) * ATT_PIECE,
                           l * ATT_PIECE:(l + 1) * ATT_PIECE].any()])
    return live


_LIVE_PIECES = _live_pieces()


def _attention_kernel(q_ref, k_ref, v_ref, g_ref, bias_ref, o_ref, vt_ref):
    n_kblk = ATT_WIN // ATT_TQ

    def head_cols(hh):
        return slice(hh * HEAD_DIM, (hh + 1) * HEAD_DIM)

    def window(qblk):
        key_blk, key_row, bias_row = [], [], []
        for c in range(n_kblk):
            rel = qblk - LEFT // ATT_TQ + c
            blk = jnp.maximum(rel, 0)
            key_blk.append(blk)
            key_row.append(pl.multiple_of(blk * ATT_TQ, ATT_TQ))
            bias_row.append(pl.multiple_of(
                jnp.where(rel >= 0, c * ATT_TQ, LEFT + ATT_WIN - ATT_TQ), ATT_TQ))
        return key_blk, key_row, bias_row

    qblks = [pl.program_id(2) * ATT_QB + j for j in range(ATT_QB)]
    windows = [window(qblk) for qblk in qblks]

    for qblk in qblks:
        q_row = pl.multiple_of(qblk * ATT_TQ, ATT_TQ)
        for hh in range(ATT_HB):
            vt_ref[hh, qblk, :HEAD_DIM, :] = v_ref[pl.ds(q_row, ATT_TQ), head_cols(hh)].T
            vt_ref[hh, qblk, HEAD_DIM:, :] = jnp.ones((ATT_ONES_ROWS, ATT_TQ), BF16)

    def scores(j, hh):
        _, key_row, bias_row = windows[j]
        q = q_ref[j * ATT_TQ:(j + 1) * ATT_TQ, head_cols(hh)]
        pieces = {}
        for c in range(n_kblk):
            k = k_ref[pl.ds(key_row[c], ATT_TQ), head_cols(hh)]
            st = lax.dot_general(k, q, (((1,), (1,)), ((), ())), preferred_element_type=F32)
            for r, l in _LIVE_PIECES[c]:
                rows = slice(r * ATT_PIECE, (r + 1) * ATT_PIECE)
                lanes = slice(l * ATT_PIECE, (l + 1) * ATT_PIECE)
                bias = bias_ref[hh, pl.ds(bias_row[c] + r * ATT_PIECE, ATT_PIECE), lanes]
                pieces[c, r, l] = st[rows, lanes] + bias
        return pieces

    def attend(j, hh, pieces):
        key_blk = windows[j][0]
        n_half = ATT_TQ // ATT_PIECE
        probs = {}
        for l in range(n_half):
            live = [key for key in pieces if key[2] == l]
            m = jnp.max(pieces[live[0]], axis=0, keepdims=True)
            for key in live[1:]:
                m = jnp.maximum(m, jnp.max(pieces[key], axis=0, keepdims=True))
            for key in live:
                probs[key] = jnp.exp2(pieces[key] - m).astype(BF16)
        zero = jnp.zeros((ATT_PIECE, ATT_PIECE), BF16)
        ot = None
        for c in range(n_kblk):
            p = jnp.concatenate(
                [jnp.concatenate([probs.get((c, r, l), zero) for l in range(n_half)], axis=1)
                 for r in range(n_half)], axis=0)
            oc = jnp.dot(vt_ref[hh, key_blk[c]], p, preferred_element_type=F32)
            ot = oc if ot is None else ot + oc
        return ot

    def finish(j, hh, ot):
        rows = slice(j * ATT_TQ, (j + 1) * ATT_TQ)
        o = (ot[:HEAD_DIM] / ot[HEAD_DIM:HEAD_DIM + 1]).T
        g = g_ref[rows, head_cols(hh)].astype(F32)
        o_ref[rows, head_cols(hh)] = (o * (g * jax.nn.sigmoid(g))).astype(o_ref.dtype)

    units = [(j, hh) for j in range(ATT_QB) for hh in range(ATT_HB)]
    pending = [scores(*u) for u in units[:ATT_AHEAD]]
    for n, unit in enumerate(units):
        pieces = pending.pop(0)
        if n + ATT_AHEAD < len(units):
            pending.append(scores(*units[n + ATT_AHEAD]))
        finish(*unit, attend(*unit, pieces))


def _attention_bias_tiles(rel_table):
    n_heads = rel_table.shape[0]
    n_rel = LEFT + ATT_WIN
    i = np.arange(ATT_TQ)[None, :]
    j = np.arange(n_rel)[:, None] - LEFT
    span = ATT_TQ + n_rel - 1
    edge = (span - (2 * MAX_REL_DIST + 1)) // 2
    ext = jnp.pad(rel_table.astype(F32), ((0, 0), (edge, edge)), mode="edge")
    sq = math.isqrt(ATT_TQ)
    wide = n_rel - sq + ATT_TQ
    fine = jnp.stack([ext[:, sq - 1 - t:sq - 1 - t + wide] for t in range(sq)], axis=1)
    b = jnp.stack([fine[:, :, n_rel - sq * (p + 1):n_rel - sq * (p + 1) + ATT_TQ]
                   for p in range(n_rel // sq)], axis=1)
    b = b.reshape(n_heads, n_rel, ATT_TQ)
    return jnp.where(jnp.asarray(_band_allowed(i, j))[None], b * LOG2_E, NEG_INF)


def _attention(qkvg, bias_tiles, batch, seq):
    m = qkvg.shape[0]
    d_att = ATT_HEADS * HEAD_DIM
    hw = ATT_HB * HEAD_DIM
    ncol = d_att // hw
    rows = ATT_QB * ATT_TQ
    nq = seq // rows
    vt_shape = (ATT_HB, seq // ATT_TQ, HEAD_DIM + ATT_ONES_ROWS, ATT_TQ)
    vmem = _vmem_limit(
        pipelined=[3 * _nbytes((rows, hw), BF16), 2 * _nbytes((seq, hw), BF16),
                   _nbytes((ATT_HB, LEFT + ATT_WIN, ATT_TQ), F32)],
        resident=[_nbytes(vt_shape, BF16)],
        temporaries=[(ATT_AHEAD + 1) * _nbytes((ATT_WIN, ATT_TQ), F32),
                     _nbytes((ATT_WIN, ATT_TQ), BF16)])
    return pl.pallas_call(
        _attention_kernel,
        grid=(ATT_HEADS // ATT_HB, batch, nq),
        in_specs=[
            pl.BlockSpec((rows, hw), lambda h, b, q: (b * nq + q, h)),
            pl.BlockSpec((seq, hw), lambda h, b, q: (b, ncol + h)),
            pl.BlockSpec((seq, hw), lambda h, b, q: (b, 2 * ncol + h)),
            pl.BlockSpec((rows, hw), lambda h, b, q: (b * nq + q, 3 * ncol + h)),
            pl.BlockSpec((ATT_HB, LEFT + ATT_WIN, ATT_TQ), lambda h, b, q: (h, 0, 0)),
        ],
        out_specs=pl.BlockSpec((rows, hw), lambda h, b, q: (b * nq + q, h)),
        out_shape=jax.ShapeDtypeStruct((m, d_att), BF16),
        scratch_shapes=[pltpu.VMEM(vt_shape, BF16)],
        compiler_params=pltpu.CompilerParams(
            dimension_semantics=("parallel", "parallel", "arbitrary"),
            vmem_limit_bytes=vmem),
        name="chunk_attention",
    )(qkvg, qkvg, qkvg, qkvg, bias_tiles)


def _out_ln_kernel(y_ref, w_ref, r_ref, gain_ref, bias_ref, o_ref):
    for i in range(OUT_TM // OUT_SUB):
        rows = slice(i * OUT_SUB, (i + 1) * OUT_SUB)
        acc = jnp.dot(y_ref[rows, :], w_ref[...], preferred_element_type=F32)
        z = DEEPNORM_ALPHA * r_ref[rows, :] + acc
        mu = jnp.mean(z, axis=-1, keepdims=True)
        zc = z - mu
        var = jnp.mean(zc * zc, axis=-1, keepdims=True)
        o_ref[rows, :] = zc * lax.rsqrt(var + LN_EPS) * gain_ref[...] + bias_ref[...]


def _out_ln(y, w, resid, gain, bias, name):
    m, k = y.shape
    n = w.shape[1]
    const = lambda i: (0, 0)
    vmem = _vmem_limit(
        pipelined=[_nbytes((OUT_TM, k), BF16), 2 * _nbytes((OUT_TM, n), F32),
                   2 * _nbytes((SUBLANES, n), F32)],
        resident=[_nbytes((k, n), BF16)],
        temporaries=[2 * 2 * _nbytes((OUT_SUB, n), F32)])
    return pl.pallas_call(
        _out_ln_kernel,
        grid=(m // OUT_TM,),
        in_specs=[
            pl.BlockSpec((OUT_TM, k), lambda i: (i, 0)),
            pl.BlockSpec((k, n), const, pipeline_mode=pl.Buffered(1)),
            pl.BlockSpec((OUT_TM, n), lambda i: (i, 0)),
            pl.BlockSpec((1, n), const),
            pl.BlockSpec((1, n), const),
        ],
        out_specs=pl.BlockSpec((OUT_TM, n), lambda i: (i, 0)),
        out_shape=jax.ShapeDtypeStruct((m, n), F32),
        compiler_params=pltpu.CompilerParams(
            dimension_semantics=("parallel",),
            vmem_limit_bytes=vmem),
        name=name,
    )(y, w, resid, gain, bias)


def _time_permutation():
    r = np.arange(LRU_TM)
    p = np.zeros((LRU_TM, LRU_TM), np.float32)
    p[r, LRU_SUB * (r % SUBLANES) + r // SUBLANES] = 1.0
    return p


def _sublane_scan(a, b):
    row = lax.broadcasted_iota(jnp.int32, a.shape, 0)
    d = 1
    while d < SUBLANES:
        keep = row >= d
        a_sh = jnp.where(keep, pltpu.roll(a, d, 0), 1.0)
        b_sh = jnp.where(keep, pltpu.roll(b, d, 0), 0.0)
        b = a * b_sh + b
        a = a * a_sh
        d *= 2
    return a, b


def _lru_kernel(h_ref, perm_ref, win_ref, cw_ref, cb_ref, wax_ref, bax_ref, lam_ref, y_ref,
                xbuf, ybuf, tailbuf, hcarry):
    t = pl.program_id(1)
    tm = LRU_TM
    hist = (CONV_W - 1) * SUBLANES

    @pl.when(t == 0)
    def _():
        tailbuf[...] = jnp.zeros_like(tailbuf)
        hcarry[...] = jnp.zeros_like(hcarry)

    for sb in range(LRU_SB):
        x = h_ref[sb * tm:(sb + 1) * tm, :].astype(BF16)
        xbuf[sb] = jnp.dot(perm_ref[0], x, preferred_element_type=F32).astype(BF16)

    lam = lam_ref[...]
    softplus_neg_lam = jnp.maximum(-lam, 0.0) + jnp.log1p(jnp.exp(-jnp.abs(lam)))
    log2_a_scale = (-RG_C * LOG2_E) * softplus_neg_lam
    row0 = lax.broadcasted_iota(jnp.int32, (SUBLANES, LRU_BLOCK_W), 0) == 0

    d_rnn = y_ref.shape[1]

    def project(sb, n):
        w = jnp.concatenate(
            [win_ref[:, n * LRU_BLOCK_W:(n + 1) * LRU_BLOCK_W],
             win_ref[:, d_rnn + n * LRU_BLOCK_W:d_rnn + (n + 1) * LRU_BLOCK_W]], axis=1)
        return jnp.dot(xbuf[sb], w, preferred_element_type=F32)

    def recur(sb, n, ug):
        cols = slice(n * LRU_BLOCK_W, (n + 1) * LRU_BLOCK_W)
        u_raw = ug[:, :LRU_BLOCK_W]
        g = ug[:, LRU_BLOCK_W:]
        cur = u_raw[tm - hist:, :]
        prev = tailbuf[:, cols]
        tailbuf[:, cols] = cur
        tiles = []
        for j in range(CONV_W - 1):
            rows = slice(j * SUBLANES, (j + 1) * SUBLANES)
            tiles.append(jnp.where(row0, pltpu.roll(prev[rows], 1, 0),
                                   pltpu.roll(cur[rows], 1, 0)))
        ext = jnp.concatenate(tiles + [u_raw], axis=0)
        u = cb_ref[:, cols]
        for tap in range(CONV_W):
            start = tap * SUBLANES
            u = u + ext[start:start + tm] * cw_ref[tap:tap + 1, cols]
        gates = jnp.dot(u.astype(BF16), wax_ref[n], preferred_element_type=F32)
        gates = jax.nn.sigmoid(gates + bax_ref[n])
        r = gates[:, :LRU_BLOCK_W]
        i = gates[:, LRU_BLOCK_W:]
        a = jnp.exp2(log2_a_scale[:, cols] * r)
        one_m_a2 = 1.0 - a * a
        mult = jnp.where(one_m_a2 > 0.0, one_m_a2 * lax.rsqrt(one_m_a2), 0.0)
        b = mult * (i * u)

        hs, ps = [b[:SUBLANES]], [a[:SUBLANES]]
        for k in range(1, LRU_SUB):
            rows = slice(k * SUBLANES, (k + 1) * SUBLANES)
            hs.append(a[rows] * hs[-1] + b[rows])
            ps.append(a[rows] * ps[-1])
        p_inc, h_inc = _sublane_scan(ps[-1], hs[-1])
        h0 = hcarry[:, cols]
        row = lax.broadcasted_iota(jnp.int32, p_inc.shape, 0)
        p_exc = jnp.where(row == 0, 1.0, pltpu.roll(p_inc, 1, 0))
        h_exc = jnp.where(row == 0, 0.0, pltpu.roll(h_inc, 1, 0))
        carry = p_exc * h0 + h_exc
        hcarry[:, cols] = (p_inc * h0 + h_inc)[SUBLANES - 1:SUBLANES, :]
        h = jnp.concatenate([hk + pk * carry for hk, pk in zip(hs, ps)], axis=0)
        ybuf[sb, :, cols] = (h * (g * jax.nn.sigmoid(g))).astype(BF16)

    units = [(sb, n) for sb in range(LRU_SB) for n in range(LRU_BLOCKS)]
    pending = [project(*u) for u in units[:LRU_AHEAD]]
    for i, unit in enumerate(units):
        ug = pending.pop(0)
        if i + LRU_AHEAD < len(units):
            pending.append(project(*units[i + LRU_AHEAD]))
        recur(*unit, ug)

    for sb in range(LRU_SB):
        y = jnp.dot(perm_ref[1], ybuf[sb], preferred_element_type=F32)
        y_ref[sb * tm:(sb + 1) * tm, :] = y.astype(y_ref.dtype)


def _lru(h, w_in, conv_w, conv_b, wax, bax, lam, batch, seq):
    m, d = h.shape
    d_rnn = lam.shape[1]
    rows = LRU_SB * LRU_TM
    nt = seq // rows
    perm = _time_permutation()
    perms = jnp.asarray(np.stack([perm, perm.T]), BF16)
    const = lambda b, t: (0, 0)
    const3 = lambda b, t: (0, 0, 0)
    hist = (CONV_W - 1) * SUBLANES
    block_f32 = _nbytes((LRU_TM, LRU_BLOCK_W), F32)
    vmem = _vmem_limit(
        pipelined=[_nbytes((rows, d), F32), _nbytes((rows, d_rnn), BF16),
                   _nbytes((2, LRU_TM, LRU_TM), BF16),
                   _nbytes((LRU_BLOCKS, LRU_BLOCK_W, 2 * LRU_BLOCK_W), BF16),
                   _nbytes((LRU_BLOCKS, SUBLANES, 2 * LRU_BLOCK_W), F32),
                   (CONV_W + 3 * SUBLANES) * d_rnn * 4],
        resident=[_nbytes((d, 2 * d_rnn), BF16), _nbytes((rows, d), BF16),
                  _nbytes((rows, d_rnn), BF16), _nbytes((hist + SUBLANES, d_rnn), F32)],
        temporaries=[_nbytes((LRU_TM, d), F32), (LRU_AHEAD + 1) * 2 * block_f32,
                     12 * block_f32])
    return pl.pallas_call(
        _lru_kernel,
        grid=(batch, nt),
        in_specs=[
            pl.BlockSpec((rows, d), lambda b, t: (b * nt + t, 0)),
            pl.BlockSpec((2, LRU_TM, LRU_TM), const3),
            pl.BlockSpec((d, 2 * d_rnn), const, pipeline_mode=pl.Buffered(1)),
            pl.BlockSpec((CONV_W, d_rnn), const),
            pl.BlockSpec((1, d_rnn), const),
            pl.BlockSpec((LRU_BLOCKS, LRU_BLOCK_W, 2 * LRU_BLOCK_W), const3),
            pl.BlockSpec((LRU_BLOCKS, 1, 2 * LRU_BLOCK_W), const3),
            pl.BlockSpec((1, d_rnn), const),
        ],
        out_specs=pl.BlockSpec((rows, d_rnn), lambda b, t: (b * nt + t, 0)),
        out_shape=jax.ShapeDtypeStruct((m, d_rnn), BF16),
        scratch_shapes=[
            pltpu.VMEM((LRU_SB, LRU_TM, d), BF16),
            pltpu.VMEM((LRU_SB, LRU_TM, d_rnn), BF16),
            pltpu.VMEM((hist, d_rnn), F32),
            pltpu.VMEM((1, d_rnn), F32),
        ],
        compiler_params=pltpu.CompilerParams(
            dimension_semantics=("parallel", "arbitrary"),
            vmem_limit_bytes=vmem),
        name="rglru",
    )(h, perms, w_in, conv_w, conv_b, wax, bax, lam)


def kernel(x, attn_w_in, attn_w_out, attn_rel_bias, lru_w_in, lru_conv_w, lru_conv_b,
           lru_wa, lru_ba, lru_wx, lru_bx, lru_lambda, lru_w_out, ln_gain, ln_bias):
    batch, seq, d_model = x.shape
    m = batch * seq
    d_att = ATT_HEADS * HEAD_DIM
    h = x.reshape(m, d_model)

    col_scale = jnp.concatenate([jnp.full((d_att,), Q_SCALE, F32),
                                 jnp.ones((3 * d_att,), F32)])[None, :]
    qkvg = _proj(h, attn_w_in[0].astype(BF16), col_scale)
    og = _attention(qkvg, _attention_bias_tiles(attn_rel_bias[0]), batch, seq)
    h = _out_ln(og, attn_w_out[0].astype(BF16), h, ln_gain[0][None, :], ln_bias[0][None, :],
                "attn_out_ln")

    wax = jnp.concatenate([lru_wa[0], lru_wx[0]], axis=-1).astype(BF16)
    bax = jnp.concatenate([lru_ba[0], lru_bx[0]], axis=-1)[:, None, :]
    y = _lru(h, lru_w_in[0].astype(BF16), lru_conv_w[0], lru_conv_b[0][None, :], wax, bax,
             lru_lambda[0][None, :], batch, seq)
    h = _out_ln(y, lru_w_out[0].astype(BF16), h, ln_gain[1][None, :], ln_bias[1][None, :],
                "lru_out_ln")
    return h.reshape(batch, seq, d_model)
```

```python
import math

import jax
import jax.numpy as jnp
import numpy as np
from jax import lax
from jax.experimental import pallas as pl
from jax.experimental.pallas import tpu as pltpu

DEPTH = 2
CHUNK = 64
LEFT_CHUNKS = 8
LEFT = LEFT_CHUNKS * CHUNK
ATT_HEADS = 16
HEAD_DIM = 128
MAX_REL_DIST = 256
LRU_BLOCKS = 16
LRU_BLOCK_W = 128
CONV_W = 4
RG_C = 8.0
LN_EPS = 1e-5
NEG_INF = -1e30
DEEPNORM_ALPHA = (2.0 * DEPTH) ** 0.25
LOG2_E = math.log2(math.e)
Q_SCALE = HEAD_DIM ** -0.5 * LOG2_E

SUBLANES = 8
MIB = 1024 * 1024
V7X_VMEM_BYTES = 64 * MIB

PROJ_TM = 1024
PROJ_TN = 2048
ATT_TQ = 256
ATT_WIN = LEFT + ATT_TQ
ATT_HB = 8
ATT_QB = 4
ATT_AHEAD = 2
ATT_ONES_ROWS = 16
ATT_PIECE = 128
OUT_TM = 512
OUT_SUB = 256
OUT_TN = 256
OUT_AHEAD = 2
LRU_TM = 256
LRU_SB = 2
LRU_SUB = LRU_TM // SUBLANES
LRU_AHEAD = 2

F32 = jnp.float32
BF16 = jnp.bfloat16


def _nbytes(shape, dtype):
    return math.prod(shape) * jnp.dtype(dtype).itemsize


def _vmem_limit(pipelined, resident=(), temporaries=()):
    need = 2 * sum(pipelined) + sum(resident) + sum(temporaries)
    assert need <= V7X_VMEM_BYTES, f"{need} bytes of VMEM requested"
    return need


def _proj_kernel(x_ref, w_ref, s_ref, o_ref):
    x = x_ref[...].astype(BF16)
    acc = jnp.dot(x, w_ref[...], preferred_element_type=F32)
    o_ref[...] = (acc * s_ref[...]).astype(o_ref.dtype)


def _proj(x, w, col_scale):
    m, k = x.shape
    n = w.shape[1]
    vmem = _vmem_limit(
        pipelined=[_nbytes((PROJ_TM, k), F32), _nbytes((k, PROJ_TN), BF16),
                   _nbytes((SUBLANES, PROJ_TN), F32), _nbytes((PROJ_TM, PROJ_TN), BF16)],
        temporaries=[_nbytes((PROJ_TM, k), BF16), _nbytes((PROJ_TM, PROJ_TN), F32)])
    return pl.pallas_call(
        _proj_kernel,
        grid=(m // PROJ_TM, n // PROJ_TN),
        in_specs=[
            pl.BlockSpec((PROJ_TM, k), lambda i, j: (i, 0)),
            pl.BlockSpec((k, PROJ_TN), lambda i, j: (0, j)),
            pl.BlockSpec((1, PROJ_TN), lambda i, j: (0, j)),
        ],
        out_specs=pl.BlockSpec((PROJ_TM, PROJ_TN), lambda i, j: (i, j)),
        out_shape=jax.ShapeDtypeStruct((m, n), BF16),
        compiler_params=pltpu.CompilerParams(
            dimension_semantics=("parallel", "arbitrary"),
            vmem_limit_bytes=vmem),
        name="attn_in_proj",
    )(x, w, col_scale)


def _band_allowed(i, j):
    dchunk = i // CHUNK - j // CHUNK
    return (dchunk >= 0) & (dchunk <= LEFT_CHUNKS)


def _live_pieces():
    i = np.arange(ATT_TQ)[None, :]
    live = []
    for c in range(ATT_WIN // ATT_TQ):
        j = np.arange(ATT_TQ)[:, None] + c * ATT_TQ - LEFT
        ok = _band_allowed(i, j)
        n_half = ATT_TQ // ATT_PIECE
        live.append([(r, l) for r in range(n_half) for l in range(n_half)
                     if ok[r * ATT_PIECE:(r + 1) * ATT_PIECE,
                           l * ATT_PIECE:(l + 1) * ATT_PIECE].any()])
    return live


_LIVE_PIECES = _live_pieces()


def _attention_kernel(q_ref, k_ref, v_ref, g_ref, bias_ref, o_ref, vt_ref):
    n_kblk = ATT_WIN // ATT_TQ

    def head_cols(hh):
        return slice(hh * HEAD_DIM, (hh + 1) * HEAD_DIM)

    def window(qblk):
        key_blk, key_row, bias_row = [], [], []
        for c in range(n_kblk):
            rel = qblk - LEFT // ATT_TQ + c
            blk = jnp.maximum(rel, 0)
            key_blk.append(blk)
            key_row.append(pl.multiple_of(blk * ATT_TQ, ATT_TQ))
            bias_row.append(pl.multiple_of(
                jnp.where(rel >= 0, c * ATT_TQ, LEFT + ATT_WIN - ATT_TQ), ATT_TQ))
        return key_blk, key_row, bias_row

    qblks = [pl.program_id(2) * ATT_QB + j for j in range(ATT_QB)]
    windows = [window(qblk) for qblk in qblks]

    for qblk in qblks:
        q_row = pl.multiple_of(qblk * ATT_TQ, ATT_TQ)
        for hh in range(ATT_HB):
            vt_ref[hh, qblk, :HEAD_DIM, :] = v_ref[pl.ds(q_row, ATT_TQ), head_cols(hh)].T
            vt_ref[hh, qblk, HEAD_DIM:, :] = jnp.ones((ATT_ONES_ROWS, ATT_TQ), BF16)

    def scores(j, hh):
        _, key_row, bias_row = windows[j]
        q = q_ref[j * ATT_TQ:(j + 1) * ATT_TQ, head_cols(hh)]
        pieces = {}
        for c in range(n_kblk):
            k = k_ref[pl.ds(key_row[c], ATT_TQ), head_cols(hh)]
            st = lax.dot_general(k, q, (((1,), (1,)), ((), ())), preferred_element_type=F32)
            for r, l in _LIVE_PIECES[c]:
                rows = slice(r * ATT_PIECE, (r + 1) * ATT_PIECE)
                lanes = slice(l * ATT_PIECE, (l + 1) * ATT_PIECE)
                bias = bias_ref[hh, pl.ds(bias_row[c] + r * ATT_PIECE, ATT_PIECE), lanes]
                pieces[c, r, l] = st[rows, lanes] + bias
        return pieces

    def attend(j, hh, pieces):
        key_blk = windows[j][0]
        n_half = ATT_TQ // ATT_PIECE
        probs = {}
        for l in range(n_half):
            live = [key for key in pieces if key[2] == l]
            m = jnp.max(pieces[live[0]], axis=0, keepdims=True)
            for key in live[1:]:
                m = jnp.maximum(m, jnp.max(pieces[key], axis=0, keepdims=True))
            for key in live:
                probs[key] = jnp.exp2(pieces[key] - m).astype(BF16)
        zero = jnp.zeros((ATT_PIECE, ATT_PIECE), BF16)
        ot = None
        for c in range(n_kblk):
            p = jnp.concatenate(
                [jnp.concatenate([probs.get((c, r, l), zero) for l in range(n_half)], axis=1)
                 for r in range(n_half)], axis=0)
            oc = jnp.dot(vt_ref[hh, key_blk[c]], p, preferred_element_type=F32)
            ot = oc if ot is None else ot + oc
        return ot

    def finish(j, hh, ot):
        rows = slice(j * ATT_TQ, (j + 1) * ATT_TQ)
        o = (ot[:HEAD_DIM] / ot[HEAD_DIM:HEAD_DIM + 1]).T
        g = g_ref[rows, head_cols(hh)].astype(F32)
        o_ref[rows, head_cols(hh)] = (o * (g * jax.nn.sigmoid(g))).astype(o_ref.dtype)

    units = [(j, hh) for j in range(ATT_QB) for hh in range(ATT_HB)]
    pending = [scores(*u) for u in units[:ATT_AHEAD]]
    for n, unit in enumerate(units):
        pieces = pending.pop(0)
        if n + ATT_AHEAD < len(units):
            pending.append(scores(*units[n + ATT_AHEAD]))
        finish(*unit, attend(*unit, pieces))


def _attention_bias_tiles(rel_table):
    n_heads = rel_table.shape[0]
    n_rel = LEFT + ATT_WIN
    i = np.arange(ATT_TQ)[None, :]
    j = np.arange(n_rel)[:, None] - LEFT
    span = ATT_TQ + n_rel - 1
    edge = (span - (2 * MAX_REL_DIST + 1)) // 2
    ext = jnp.pad(rel_table.astype(F32), ((0, 0), (edge, edge)), mode="edge")
    sq = math.isqrt(ATT_TQ)
    wide = n_rel - sq + ATT_TQ
    fine = jnp.stack([ext[:, sq - 1 - t:sq - 1 - t + wide] for t in range(sq)], axis=1)
    b = jnp.stack([fine[:, :, n_rel - sq * (p + 1):n_rel - sq * (p + 1) + ATT_TQ]
                   for p in range(n_rel // sq)], axis=1)
    b = b.reshape(n_heads, n_rel, ATT_TQ)
    return jnp.where(jnp.asarray(_band_allowed(i, j))[None], b * LOG2_E, NEG_INF)


def _attention(qkvg, bias_tiles, batch, seq):
    m = qkvg.shape[0]
    d_att = ATT_HEADS * HEAD_DIM
    hw = ATT_HB * HEAD_DIM
    ncol = d_att // hw
    rows = ATT_QB * ATT_TQ
    nq = seq // rows
    vt_shape = (ATT_HB, seq // ATT_TQ, HEAD_DIM + ATT_ONES_ROWS, ATT_TQ)
    vmem = _vmem_limit(
        pipelined=[3 * _nbytes((rows, hw), BF16), 2 * _nbytes((seq, hw), BF16),
                   _nbytes((ATT_HB, LEFT + ATT_WIN, ATT_TQ), F32)],
        resident=[_nbytes(vt_shape, BF16)],
        temporaries=[(ATT_AHEAD + 1) * _nbytes((ATT_WIN, ATT_TQ), F32),
                     _nbytes((ATT_WIN, ATT_TQ), BF16)])
    return pl.pallas_call(
        _attention_kernel,
        grid=(ATT_HEADS // ATT_HB, batch, nq),
        in_specs=[
            pl.BlockSpec((rows, hw), lambda h, b, q: (b * nq + q, h)),
            pl.BlockSpec((seq, hw), lambda h, b, q: (b, ncol + h)),
            pl.BlockSpec((seq, hw), lambda h, b, q: (b, 2 * ncol + h)),
            pl.BlockSpec((rows, hw), lambda h, b, q: (b * nq + q, 3 * ncol + h)),
            pl.BlockSpec((ATT_HB, LEFT + ATT_WIN, ATT_TQ), lambda h, b, q: (h, 0, 0)),
        ],
        out_specs=pl.BlockSpec((rows, hw), lambda h, b, q: (b * nq + q, h)),
        out_shape=jax.ShapeDtypeStruct((m, d_att), BF16),
        scratch_shapes=[pltpu.VMEM(vt_shape, BF16)],
        compiler_params=pltpu.CompilerParams(
            dimension_semantics=("parallel", "parallel", "arbitrary"),
            vmem_limit_bytes=vmem),
        name="chunk_attention",
    )(qkvg, qkvg, qkvg, qkvg, bias_tiles)


def _out_ln_kernel(y_ref, w_ref, r_ref, gain_ref, bias_ref, o_ref, z_ref):
    n = w_ref.shape[1]
    n_tiles = n // OUT_TN

    def rows_of(i):
        return slice(i * OUT_SUB, (i + 1) * OUT_SUB)

    def cols_of(t):
        return slice(t * OUT_TN, (t + 1) * OUT_TN)

    def project(i, t):
        return jnp.dot(y_ref[rows_of(i), :], w_ref[:, cols_of(t)], preferred_element_type=F32)

    always = pl.program_id(0) >= 0

    def normalise(i, tt, mu, rstd, anchor=None):
        zt = z_ref[rows_of(i), cols_of(tt)]
        out = (zt - mu) * rstd * gain_ref[:, cols_of(tt)] + bias_ref[:, cols_of(tt)]
        if anchor is not None:
            out = jnp.where(always, out, anchor)
        o_ref[rows_of(i), cols_of(tt)] = out

    units = [(i, t) for i in range(OUT_TM // OUT_SUB) for t in range(n_tiles)]
    pending = [project(*u) for u in units[:OUT_AHEAD]]
    jobs = []
    shift = s1 = s2 = None
    for k, (i, t) in enumerate(units):
        acc = pending.pop(0)
        if k + OUT_AHEAD < len(units):
            pending.append(project(*units[k + OUT_AHEAD]))
        z = DEEPNORM_ALPHA * r_ref[rows_of(i), cols_of(t)] + acc
        z_ref[rows_of(i), cols_of(t)] = z
        if t == 0:
            shift = jnp.mean(z, axis=-1, keepdims=True)
        d = z - shift
        d1 = jnp.sum(d, axis=-1, keepdims=True)
        d2 = jnp.sum(d * d, axis=-1, keepdims=True)
        s1 = d1 if t == 0 else s1 + d1
        s2 = d2 if t == 0 else s2 + d2
        if jobs:
            normalise(*jobs.pop(0), anchor=acc)
        if t == n_tiles - 1:
            mean_d = s1 * (1.0 / n)
            var = s2 * (1.0 / n) - mean_d * mean_d
            mu = shift + mean_d
            rstd = lax.rsqrt(var + LN_EPS)
            jobs += [(i, tt, mu, rstd) for tt in range(n_tiles)]
    for job in jobs:
        normalise(*job)


def _out_ln(y, w, resid, gain, bias, name):
    m, k = y.shape
    n = w.shape[1]
    const = lambda i: (0, 0)
    vmem = _vmem_limit(
        pipelined=[_nbytes((OUT_TM, k), BF16), 2 * _nbytes((OUT_TM, n), F32),
                   2 * _nbytes((SUBLANES, n), F32)],
        resident=[_nbytes((k, n), BF16), _nbytes((OUT_TM, n), F32)],
        temporaries=[(OUT_AHEAD + 1) * _nbytes((OUT_SUB, OUT_TN), F32),
                     _nbytes((OUT_SUB, n), F32)])
    return pl.pallas_call(
        _out_ln_kernel,
        grid=(m // OUT_TM,),
        in_specs=[
            pl.BlockSpec((OUT_TM, k), lambda i: (i, 0)),
            pl.BlockSpec((k, n), const, pipeline_mode=pl.Buffered(1)),
            pl.BlockSpec((OUT_TM, n), lambda i: (i, 0)),
            pl.BlockSpec((1, n), const),
            pl.BlockSpec((1, n), const),
        ],
        out_specs=pl.BlockSpec((OUT_TM, n), lambda i: (i, 0)),
        out_shape=jax.ShapeDtypeStruct((m, n), F32),
        scratch_shapes=[pltpu.VMEM((OUT_TM, n), F32)],
        compiler_params=pltpu.CompilerParams(
            dimension_semantics=("parallel",),
            vmem_limit_bytes=vmem),
        name=name,
    )(y, w, resid, gain, bias)


def _time_permutation():
    r = np.arange(LRU_TM)
    p = np.zeros((LRU_TM, LRU_TM), np.float32)
    p[r, LRU_SUB * (r % SUBLANES) + r // SUBLANES] = 1.0
    return p


def _sublane_scan(a, b):
    row = lax.broadcasted_iota(jnp.int32, a.shape, 0)
    d = 1
    while d < SUBLANES:
        keep = row >= d
        a_sh = jnp.where(keep, pltpu.roll(a, d, 0), 1.0)
        b_sh = jnp.where(keep, pltpu.roll(b, d, 0), 0.0)
        b = a * b_sh + b
        a = a * a_sh
        d *= 2
    return a, b


def _lru_kernel(h_ref, perm_ref, win_ref, cw_ref, cb_ref, wax_ref, bax_ref, lam_ref, y_ref,
                xbuf, ybuf, tailbuf, hcarry):
    t = pl.program_id(1)
    tm = LRU_TM
    hist = (CONV_W - 1) * SUBLANES

    @pl.when(t == 0)
    def _():
        tailbuf[...] = jnp.zeros_like(tailbuf)
        hcarry[...] = jnp.zeros_like(hcarry)

    for sb in range(LRU_SB):
        x = h_ref[sb * tm:(sb + 1) * tm, :].astype(BF16)
        xbuf[sb] = jnp.dot(perm_ref[0], x, preferred_element_type=F32).astype(BF16)

    lam = lam_ref[...]
    softplus_neg_lam = jnp.maximum(-lam, 0.0) + jnp.log1p(jnp.exp(-jnp.abs(lam)))
    log2_a_scale = (-RG_C * LOG2_E) * softplus_neg_lam
    row0 = lax.broadcasted_iota(jnp.int32, (SUBLANES, LRU_BLOCK_W), 0) == 0

    d_rnn = y_ref.shape[1]

    def project(sb, n):
        w = jnp.concatenate(
            [win_ref[:, n * LRU_BLOCK_W:(n + 1) * LRU_BLOCK_W],
             win_ref[:, d_rnn + n * LRU_BLOCK_W:d_rnn + (n + 1) * LRU_BLOCK_W]], axis=1)
        return jnp.dot(xbuf[sb], w, preferred_element_type=F32)

    def recur(sb, n, ug):
        cols = slice(n * LRU_BLOCK_W, (n + 1) * LRU_BLOCK_W)
        u_raw = ug[:, :LRU_BLOCK_W]
        g = ug[:, LRU_BLOCK_W:]
        cur = u_raw[tm - hist:, :]
        prev = tailbuf[:, cols]
        tailbuf[:, cols] = cur
        tiles = []
        for j in range(CONV_W - 1):
            rows = slice(j * SUBLANES, (j + 1) * SUBLANES)
            tiles.append(jnp.where(row0, pltpu.roll(prev[rows], 1, 0),
                                   pltpu.roll(cur[rows], 1, 0)))
        ext = jnp.concatenate(tiles + [u_raw], axis=0)
        u = cb_ref[:, cols]
        for tap in range(CONV_W):
            start = tap * SUBLANES
            u = u + ext[start:start + tm] * cw_ref[tap:tap + 1, cols]
        gates = jnp.dot(u.astype(BF16), wax_ref[n], preferred_element_type=F32)
        gates = jax.nn.sigmoid(gates + bax_ref[n])
        r = gates[:, :LRU_BLOCK_W]
        i = gates[:, LRU_BLOCK_W:]
        a = jnp.exp2(log2_a_scale[:, cols] * r)
        one_m_a2 = 1.0 - a * a
        mult = jnp.where(one_m_a2 > 0.0, one_m_a2 * lax.rsqrt(one_m_a2), 0.0)
        b = mult * (i * u)

        hs, ps = [b[:SUBLANES]], [a[:SUBLANES]]
        for k in range(1, LRU_SUB):
            rows = slice(k * SUBLANES, (k + 1) * SUBLANES)
            hs.append(a[rows] * hs[-1] + b[rows])
            ps.append(a[rows] * ps[-1])
        p_inc, h_inc = _sublane_scan(ps[-1], hs[-1])
        h0 = hcarry[:, cols]
        row = lax.broadcasted_iota(jnp.int32, p_inc.shape, 0)
        p_exc = jnp.where(row == 0, 1.0, pltpu.roll(p_inc, 1, 0))
        h_exc = jnp.where(row == 0, 0.0, pltpu.roll(h_inc, 1, 0))
        carry = p_exc * h0 + h_exc
        hcarry[:, cols] = (p_inc * h0 + h_inc)[SUBLANES - 1:SUBLANES, :]
        h = jnp.concatenate([hk + pk * carry for hk, pk in zip(hs, ps)], axis=0)
        ybuf[sb, :, cols] = (h * (g * jax.nn.sigmoid(g))).astype(BF16)

    units = [(sb, n) for sb in range(LRU_SB) for n in range(LRU_BLOCKS)]
    pending = [project(*u) for u in units[:LRU_AHEAD]]
    for i, unit in enumerate(units):
        ug = pending.pop(0)
        if i + LRU_AHEAD < len(units):
            pending.append(project(*units[i + LRU_AHEAD]))
        recur(*unit, ug)

    for sb in range(LRU_SB):
        y = jnp.dot(perm_ref[1], ybuf[sb], preferred_element_type=F32)
        y_ref[sb * tm:(sb + 1) * tm, :] = y.astype(y_ref.dtype)


def _lru(h, w_in, conv_w, conv_b, wax, bax, lam, batch, seq):
    m, d = h.shape
    d_rnn = lam.shape[1]
    rows = LRU_SB * LRU_TM
    nt = seq // rows
    perm = _time_permutation()
    perms = jnp.asarray(np.stack([perm, perm.T]), BF16)
    const = lambda b, t: (0, 0)
    const3 = lambda b, t: (0, 0, 0)
    hist = (CONV_W - 1) * SUBLANES
    block_f32 = _nbytes((LRU_TM, LRU_BLOCK_W), F32)
    vmem = _vmem_limit(
        pipelined=[_nbytes((rows, d), F32), _nbytes((rows, d_rnn), BF16),
                   _nbytes((2, LRU_TM, LRU_TM), BF16),
                   _nbytes((LRU_BLOCKS, LRU_BLOCK_W, 2 * LRU_BLOCK_W), BF16),
                   _nbytes((LRU_BLOCKS, SUBLANES, 2 * LRU_BLOCK_W), F32),
                   (CONV_W + 3 * SUBLANES) * d_rnn * 4],
        resident=[_nbytes((d, 2 * d_rnn), BF16), _nbytes((rows, d), BF16),
                  _nbytes((rows, d_rnn), BF16), _nbytes((hist + SUBLANES, d_rnn), F32)],
        temporaries=[_nbytes((LRU_TM, d), F32), (LRU_AHEAD + 1) * 2 * block_f32,
                     12 * block_f32])
    return pl.pallas_call(
        _lru_kernel,
        grid=(batch, nt),
        in_specs=[
            pl.BlockSpec((rows, d), lambda b, t: (b * nt + t, 0)),
            pl.BlockSpec((2, LRU_TM, LRU_TM), const3),
            pl.BlockSpec((d, 2 * d_rnn), const, pipeline_mode=pl.Buffered(1)),
            pl.BlockSpec((CONV_W, d_rnn), const),
            pl.BlockSpec((1, d_rnn), const),
            pl.BlockSpec((LRU_BLOCKS, LRU_BLOCK_W, 2 * LRU_BLOCK_W), const3),
            pl.BlockSpec((LRU_BLOCKS, 1, 2 * LRU_BLOCK_W), const3),
            pl.BlockSpec((1, d_rnn), const),
        ],
        out_specs=pl.BlockSpec((rows, d_rnn), lambda b, t: (b * nt + t, 0)),
        out_shape=jax.ShapeDtypeStruct((m, d_rnn), BF16),
        scratch_shapes=[
            pltpu.VMEM((LRU_SB, LRU_TM, d), BF16),
            pltpu.VMEM((LRU_SB, LRU_TM, d_rnn), BF16),
            pltpu.VMEM((hist, d_rnn), F32),
            pltpu.VMEM((1, d_rnn), F32),
        ],
        compiler_params=pltpu.CompilerParams(
            dimension_semantics=("parallel", "arbitrary"),
            vmem_limit_bytes=vmem),
        name="rglru",
    )(h, perms, w_in, conv_w, conv_b, wax, bax, lam)


def kernel(x, attn_w_in, attn_w_out, attn_rel_bias, lru_w_in, lru_conv_w, lru_conv_b,
           lru_wa, lru_ba, lru_wx, lru_bx, lru_lambda, lru_w_out, ln_gain, ln_bias):
    batch, seq, d_model = x.shape
    m = batch * seq
    d_att = ATT_HEADS * HEAD_DIM
    h = x.reshape(m, d_model)

    col_scale = jnp.concatenate([jnp.full((d_att,), Q_SCALE, F32),
                                 jnp.ones((3 * d_att,), F32)])[None, :]
    qkvg = _proj(h, attn_w_in[0].astype(BF16), col_scale)
    og = _attention(qkvg, _attention_bias_tiles(attn_rel_bias[0]), batch, seq)
    h = _out_ln(og, attn_w_out[0].astype(BF16), h, ln_gain[0][None, :], ln_bias[0][None, :],
                "attn_out_ln")

    wax = jnp.concatenate([lru_wa[0], lru_wx[0]], axis=-1).astype(BF16)
    bax = jnp.concatenate([lru_ba[0], lru_bx[0]], axis=-1)[:, None, :]
    y = _lru(h, lru_w_in[0].astype(BF16), lru_conv_w[0], lru_conv_b[0][None, :], wax, bax,
             lru_lambda[0][None, :], batch, seq)
    h = _out_ln(y, lru_w_out[0].astype(BF16), h, ln_gain[1][None, :], ln_bias[1][None, :],
                "lru_out_ln")
    return h.reshape(batch, seq, d_model)
```

```python
import math

import jax
import jax.numpy as jnp
import numpy as np
from jax import lax
from jax.experimental import pallas as pl
from jax.experimental.pallas import tpu as pltpu

DEPTH = 2
CHUNK = 64
LEFT_CHUNKS = 8
LEFT = LEFT_CHUNKS * CHUNK
ATT_HEADS = 16
HEAD_DIM = 128
MAX_REL_DIST = 256
LRU_BLOCKS = 16
LRU_BLOCK_W = 128
CONV_W = 4
RG_C = 8.0
LN_EPS = 1e-5
NEG_INF = -1e30
DEEPNORM_ALPHA = (2.0 * DEPTH) ** 0.25
LOG2_E = math.log2(math.e)
Q_SCALE = HEAD_DIM ** -0.5 * LOG2_E

SUBLANES = 8
MIB = 1024 * 1024
V7X_VMEM_BYTES = 64 * MIB

PROJ_TM = 1024
PROJ_TN = 2048
ATT_TQ = 256
ATT_WIN = LEFT + ATT_TQ
ATT_HB = 8
ATT_QB = 4
ATT_AHEAD = 2
ATT_ONES_ROWS = 16
ATT_PIECE = 128
OUT_TM = 1024
OUT_SUB = 256
OUT_TN = 256
OUT_AHEAD = 2
LRU_TM = 256
LRU_SB = 2
LRU_SUB = LRU_TM // SUBLANES
LRU_AHEAD = 2

F32 = jnp.float32
BF16 = jnp.bfloat16


def _nbytes(shape, dtype):
    return math.prod(shape) * jnp.dtype(dtype).itemsize


def _vmem_limit(pipelined, resident=(), temporaries=()):
    need = 2 * sum(pipelined) + sum(resident) + sum(temporaries)
    assert need <= V7X_VMEM_BYTES, f"{need} bytes of VMEM requested"
    return need


def _proj_kernel(x_ref, w_ref, s_ref, o_ref):
    x = x_ref[...].astype(BF16)
    acc = jnp.dot(x, w_ref[...], preferred_element_type=F32)
    o_ref[...] = (acc * s_ref[...]).astype(o_ref.dtype)


def _proj(x, w, col_scale):
    m, k = x.shape
    n = w.shape[1]
    vmem = _vmem_limit(
        pipelined=[_nbytes((PROJ_TM, k), F32), _nbytes((k, PROJ_TN), BF16),
                   _nbytes((SUBLANES, PROJ_TN), F32), _nbytes((PROJ_TM, PROJ_TN), BF16)],
        temporaries=[_nbytes((PROJ_TM, k), BF16), _nbytes((PROJ_TM, PROJ_TN), F32)])
    return pl.pallas_call(
        _proj_kernel,
        grid=(m // PROJ_TM, n // PROJ_TN),
        in_specs=[
            pl.BlockSpec((PROJ_TM, k), lambda i, j: (i, 0)),
            pl.BlockSpec((k, PROJ_TN), lambda i, j: (0, j)),
            pl.BlockSpec((1, PROJ_TN), lambda i, j: (0, j)),
        ],
        out_specs=pl.BlockSpec((PROJ_TM, PROJ_TN), lambda i, j: (i, j)),
        out_shape=jax.ShapeDtypeStruct((m, n), BF16),
        compiler_params=pltpu.CompilerParams(
            dimension_semantics=("parallel", "arbitrary"),
            vmem_limit_bytes=vmem),
        name="attn_in_proj",
    )(x, w, col_scale)


def _band_allowed(i, j):
    dchunk = i // CHUNK - j // CHUNK
    return (dchunk >= 0) & (dchunk <= LEFT_CHUNKS)


def _live_pieces():
    i = np.arange(ATT_TQ)[None, :]
    live = []
    for c in range(ATT_WIN // ATT_TQ):
        j = np.arange(ATT_TQ)[:, None] + c * ATT_TQ - LEFT
        ok = _band_allowed(i, j)
        n_half = ATT_TQ // ATT_PIECE
        live.append([(r, l) for r in range(n_half) for l in range(n_half)
                     if ok[r * ATT_PIECE:(r + 1) * ATT_PIECE,
                           l * ATT_PIECE:(l + 1) * ATT_PIECE].any()])
    return live


_LIVE_PIECES = _live_pieces()


def _attention_kernel(q_ref, k_ref, v_ref, g_ref, bias_ref, o_ref, vt_ref):
    n_kblk = ATT_WIN // ATT_TQ

    def head_cols(hh):
        return slice(hh * HEAD_DIM, (hh + 1) * HEAD_DIM)

    def window(qblk):
        key_blk, key_row, bias_row = [], [], []
        for c in range(n_kblk):
            rel = qblk - LEFT // ATT_TQ + c
            blk = jnp.maximum(rel, 0)
            key_blk.append(blk)
            key_row.append(pl.multiple_of(blk * ATT_TQ, ATT_TQ))
            bias_row.append(pl.multiple_of(
                jnp.where(rel >= 0, c * ATT_TQ, LEFT + ATT_WIN - ATT_TQ), ATT_TQ))
        return key_blk, key_row, bias_row

    qblks = [pl.program_id(2) * ATT_QB + j for j in range(ATT_QB)]
    windows = [window(qblk) for qblk in qblks]

    for qblk in qblks:
        q_row = pl.multiple_of(qblk * ATT_TQ, ATT_TQ)
        for hh in range(ATT_HB):
            vt_ref[hh, qblk, :HEAD_DIM, :] = v_ref[pl.ds(q_row, ATT_TQ), head_cols(hh)].T
            vt_ref[hh, qblk, HEAD_DIM:, :] = jnp.ones((ATT_ONES_ROWS, ATT_TQ), BF16)

    def scores(j, hh):
        _, key_row, bias_row = windows[j]
        q = q_ref[j * ATT_TQ:(j + 1) * ATT_TQ, head_cols(hh)]
        pieces = {}
        for c in range(n_kblk):
            k = k_ref[pl.ds(key_row[c], ATT_TQ), head_cols(hh)]
            st = lax.dot_general(k, q, (((1,), (1,)), ((), ())), preferred_element_type=F32)
            for r, l in _LIVE_PIECES[c]:
                rows = slice(r * ATT_PIECE, (r + 1) * ATT_PIECE)
                lanes = slice(l * ATT_PIECE, (l + 1) * ATT_PIECE)
                bias = bias_ref[hh, pl.ds(bias_row[c] + r * ATT_PIECE, ATT_PIECE), lanes]
                pieces[c, r, l] = st[rows, lanes] + bias
        return pieces

    def attend(j, hh, pieces):
        key_blk = windows[j][0]
        n_half = ATT_TQ // ATT_PIECE
        probs = {}
        for l in range(n_half):
            live = [key for key in pieces if key[2] == l]
            m = jnp.max(pieces[live[0]], axis=0, keepdims=True)
            for key in live[1:]:
                m = jnp.maximum(m, jnp.max(pieces[key], axis=0, keepdims=True))
            for key in live:
                probs[key] = jnp.exp2(pieces[key] - m).astype(BF16)
        zero = jnp.zeros((ATT_PIECE, ATT_PIECE), BF16)
        ot = None
        for c in range(n_kblk):
            p = jnp.concatenate(
                [jnp.concatenate([probs.get((c, r, l), zero) for l in range(n_half)], axis=1)
                 for r in range(n_half)], axis=0)
            oc = jnp.dot(vt_ref[hh, key_blk[c]], p, preferred_element_type=F32)
            ot = oc if ot is None else ot + oc
        return ot

    def finish(j, hh, ot):
        rows = slice(j * ATT_TQ, (j + 1) * ATT_TQ)
        o = (ot[:HEAD_DIM] / ot[HEAD_DIM:HEAD_DIM + 1]).T
        g = g_ref[rows, head_cols(hh)].astype(F32)
        o_ref[rows, head_cols(hh)] = (o * (g * jax.nn.sigmoid(g))).astype(o_ref.dtype)

    units = [(j, hh) for j in range(ATT_QB) for hh in range(ATT_HB)]
    pending = [scores(*u) for u in units[:ATT_AHEAD]]
    for n, unit in enumerate(units):
        pieces = pending.pop(0)
        if n + ATT_AHEAD < len(units):
            pending.append(scores(*units[n + ATT_AHEAD]))
        finish(*unit, attend(*unit, pieces))


def _attention_bias_tiles(rel_table):
    n_heads = rel_table.shape[0]
    n_rel = LEFT + ATT_WIN
    i = np.arange(ATT_TQ)[None, :]
    j = np.arange(n_rel)[:, None] - LEFT
    span = ATT_TQ + n_rel - 1
    edge = (span - (2 * MAX_REL_DIST + 1)) // 2
    ext = jnp.pad(rel_table.astype(F32), ((0, 0), (edge, edge)), mode="edge")
    sq = math.isqrt(ATT_TQ)
    wide = n_rel - sq + ATT_TQ
    fine = jnp.stack([ext[:, sq - 1 - t:sq - 1 - t + wide] for t in range(sq)], axis=1)
    b = jnp.stack([fine[:, :, n_rel - sq * (p + 1):n_rel - sq * (p + 1) + ATT_TQ]
                   for p in range(n_rel // sq)], axis=1)
    b = b.reshape(n_heads, n_rel, ATT_TQ)
    return jnp.where(jnp.asarray(_band_allowed(i, j))[None], b * LOG2_E, NEG_INF)


def _attention(qkvg, bias_tiles, batch, seq):
    m = qkvg.shape[0]
    d_att = ATT_HEADS * HEAD_DIM
    hw = ATT_HB * HEAD_DIM
    ncol = d_att // hw
    rows = ATT_QB * ATT_TQ
    nq = seq // rows
    vt_shape = (ATT_HB, seq // ATT_TQ, HEAD_DIM + ATT_ONES_ROWS, ATT_TQ)
    vmem = _vmem_limit(
        pipelined=[3 * _nbytes((rows, hw), BF16), 2 * _nbytes((seq, hw), BF16),
                   _nbytes((ATT_HB, LEFT + ATT_WIN, ATT_TQ), F32)],
        resident=[_nbytes(vt_shape, BF16)],
        temporaries=[(ATT_AHEAD + 1) * _nbytes((ATT_WIN, ATT_TQ), F32),
                     _nbytes((ATT_WIN, ATT_TQ), BF16)])
    return pl.pallas_call(
        _attention_kernel,
        grid=(ATT_HEADS // ATT_HB, batch, nq),
        in_specs=[
            pl.BlockSpec((rows, hw), lambda h, b, q: (b * nq + q, h)),
            pl.BlockSpec((seq, hw), lambda h, b, q: (b, ncol + h)),
            pl.BlockSpec((seq, hw), lambda h, b, q: (b, 2 * ncol + h)),
            pl.BlockSpec((rows, hw), lambda h, b, q: (b * nq + q, 3 * ncol + h)),
            pl.BlockSpec((ATT_HB, LEFT + ATT_WIN, ATT_TQ), lambda h, b, q: (h, 0, 0)),
        ],
        out_specs=pl.BlockSpec((rows, hw), lambda h, b, q: (b * nq + q, h)),
        out_shape=jax.ShapeDtypeStruct((m, d_att), BF16),
        scratch_shapes=[pltpu.VMEM(vt_shape, BF16)],
        compiler_params=pltpu.CompilerParams(
            dimension_semantics=("parallel", "parallel", "arbitrary"),
            vmem_limit_bytes=vmem),
        name="chunk_attention",
    )(qkvg, qkvg, qkvg, qkvg, bias_tiles)


def _out_ln_kernel(y_ref, w_ref, r_ref, gain_ref, bias_ref, o_ref, z_ref):
    n = w_ref.shape[1]
    n_tiles = n // OUT_TN

    def rows_of(i):
        return slice(i * OUT_SUB, (i + 1) * OUT_SUB)

    def cols_of(t):
        return slice(t * OUT_TN, (t + 1) * OUT_TN)

    def project(i, t):
        return jnp.dot(y_ref[rows_of(i), :], w_ref[:, cols_of(t)], preferred_element_type=F32)

    always = pl.program_id(0) >= 0

    def normalise(i, tt, mu, rstd, anchor=None):
        zt = z_ref[rows_of(i), cols_of(tt)]
        out = (zt - mu) * rstd * gain_ref[:, cols_of(tt)] + bias_ref[:, cols_of(tt)]
        if anchor is not None:
            out = jnp.where(always, out, anchor)
        o_ref[rows_of(i), cols_of(tt)] = out

    units = [(i, t) for i in range(OUT_TM // OUT_SUB) for t in range(n_tiles)]
    pending = [project(*u) for u in units[:OUT_AHEAD]]
    jobs = []
    shift = s1 = s2 = None
    for k, (i, t) in enumerate(units):
        acc = pending.pop(0)
        if k + OUT_AHEAD < len(units):
            pending.append(project(*units[k + OUT_AHEAD]))
        z = DEEPNORM_ALPHA * r_ref[rows_of(i), cols_of(t)] + acc
        z_ref[rows_of(i), cols_of(t)] = z
        if t == 0:
            shift = jnp.mean(z, axis=-1, keepdims=True)
        d = z - shift
        d1 = jnp.sum(d, axis=-1, keepdims=True)
        d2 = jnp.sum(d * d, axis=-1, keepdims=True)
        s1 = d1 if t == 0 else s1 + d1
        s2 = d2 if t == 0 else s2 + d2
        if jobs:
            normalise(*jobs.pop(0), anchor=acc)
        if t == n_tiles - 1:
            mean_d = s1 * (1.0 / n)
            var = s2 * (1.0 / n) - mean_d * mean_d
            mu = shift + mean_d
            rstd = lax.rsqrt(var + LN_EPS)
            jobs += [(i, tt, mu, rstd) for tt in range(n_tiles)]
    for job in jobs:
        normalise(*job)


def _out_ln(y, w, resid, gain, bias, name):
    m, k = y.shape
    n = w.shape[1]
    const = lambda i: (0, 0)
    vmem = _vmem_limit(
        pipelined=[_nbytes((OUT_TM, k), BF16), 2 * _nbytes((OUT_TM, n), F32),
                   2 * _nbytes((SUBLANES, n), F32)],
        resident=[_nbytes((k, n), BF16), _nbytes((OUT_TM, n), F32)],
        temporaries=[(OUT_AHEAD + 1) * _nbytes((OUT_SUB, OUT_TN), F32),
                     _nbytes((OUT_SUB, n), F32)])
    return pl.pallas_call(
        _out_ln_kernel,
        grid=(m // OUT_TM,),
        in_specs=[
            pl.BlockSpec((OUT_TM, k), lambda i: (i, 0)),
            pl.BlockSpec((k, n), const, pipeline_mode=pl.Buffered(1)),
            pl.BlockSpec((OUT_TM, n), lambda i: (i, 0)),
            pl.BlockSpec((1, n), const),
            pl.BlockSpec((1, n), const),
        ],
        out_specs=pl.BlockSpec((OUT_TM, n), lambda i: (i, 0)),
        out_shape=jax.ShapeDtypeStruct((m, n), F32),
        scratch_shapes=[pltpu.VMEM((OUT_TM, n), F32)],
        compiler_params=pltpu.CompilerParams(
            dimension_semantics=("parallel",),
            vmem_limit_bytes=vmem),
        name=name,
    )(y, w, resid, gain, bias)


def _time_permutation():
    r = np.arange(LRU_TM)
    p = np.zeros((LRU_TM, LRU_TM), np.float32)
    p[r, LRU_SUB * (r % SUBLANES) + r // SUBLANES] = 1.0
    return p


def _sublane_scan(a, b):
    row = lax.broadcasted_iota(jnp.int32, a.shape, 0)
    d = 1
    while d < SUBLANES:
        keep = row >= d
        a_sh = jnp.where(keep, pltpu.roll(a, d, 0), 1.0)
        b_sh = jnp.where(keep, pltpu.roll(b, d, 0), 0.0)
        b = a * b_sh + b
        a = a * a_sh
        d *= 2
    return a, b


def _lru_kernel(h_ref, perm_ref, win_ref, cw_ref, cb_ref, wax_ref, bax_ref, lam_ref, y_ref,
                xbuf, ybuf, tailbuf, hcarry):
    t = pl.program_id(1)
    tm = LRU_TM
    hist = (CONV_W - 1) * SUBLANES

    @pl.when(t == 0)
    def _():
        tailbuf[...] = jnp.zeros_like(tailbuf)
        hcarry[...] = jnp.zeros_like(hcarry)

    for sb in range(LRU_SB):
        x = h_ref[sb * tm:(sb + 1) * tm, :].astype(BF16)
        xbuf[sb] = jnp.dot(perm_ref[0], x, preferred_element_type=F32).astype(BF16)

    lam = lam_ref[...]
    softplus_neg_lam = jnp.maximum(-lam, 0.0) + jnp.log1p(jnp.exp(-jnp.abs(lam)))
    log2_a_scale = (-RG_C * LOG2_E) * softplus_neg_lam
    row0 = lax.broadcasted_iota(jnp.int32, (SUBLANES, LRU_BLOCK_W), 0) == 0

    d_rnn = y_ref.shape[1]

    def project(sb, n):
        w = jnp.concatenate(
            [win_ref[:, n * LRU_BLOCK_W:(n + 1) * LRU_BLOCK_W],
             win_ref[:, d_rnn + n * LRU_BLOCK_W:d_rnn + (n + 1) * LRU_BLOCK_W]], axis=1)
        return jnp.dot(xbuf[sb], w, preferred_element_type=F32)

    def recur(sb, n, ug):
        cols = slice(n * LRU_BLOCK_W, (n + 1) * LRU_BLOCK_W)
        u_raw = ug[:, :LRU_BLOCK_W]
        g = ug[:, LRU_BLOCK_W:]
        cur = u_raw[tm - hist:, :]
        prev = tailbuf[:, cols]
        tailbuf[:, cols] = cur
        tiles = []
        for j in range(CONV_W - 1):
            rows = slice(j * SUBLANES, (j + 1) * SUBLANES)
            tiles.append(jnp.where(row0, pltpu.roll(prev[rows], 1, 0),
                                   pltpu.roll(cur[rows], 1, 0)))
        ext = jnp.concatenate(tiles + [u_raw], axis=0)
        u = cb_ref[:, cols]
        for tap in range(CONV_W):
            start = tap * SUBLANES
            u = u + ext[start:start + tm] * cw_ref[tap:tap + 1, cols]
        gates = jnp.dot(u.astype(BF16), wax_ref[n], preferred_element_type=F32)
        gates = jax.nn.sigmoid(gates + bax_ref[n])
        r = gates[:, :LRU_BLOCK_W]
        i = gates[:, LRU_BLOCK_W:]
        a = jnp.exp2(log2_a_scale[:, cols] * r)
        one_m_a2 = 1.0 - a * a
        mult = jnp.where(one_m_a2 > 0.0, one_m_a2 * lax.rsqrt(one_m_a2), 0.0)
        b = mult * (i * u)

        hs, ps = [b[:SUBLANES]], [a[:SUBLANES]]
        for k in range(1, LRU_SUB):
            rows = slice(k * SUBLANES, (k + 1) * SUBLANES)
            hs.append(a[rows] * hs[-1] + b[rows])
            ps.append(a[rows] * ps[-1])
        p_inc, h_inc = _sublane_scan(ps[-1], hs[-1])
        h0 = hcarry[:, cols]
        row = lax.broadcasted_iota(jnp.int32, p_inc.shape, 0)
        p_exc = jnp.where(row == 0, 1.0, pltpu.roll(p_inc, 1, 0))
        h_exc = jnp.where(row == 0, 0.0, pltpu.roll(h_inc, 1, 0))
        carry = p_exc * h0 + h_exc
        hcarry[:, cols] = (p_inc * h0 + h_inc)[SUBLANES - 1:SUBLANES, :]
        h = jnp.concatenate([hk + pk * carry for hk, pk in zip(hs, ps)], axis=0)
        ybuf[sb, :, cols] = (h * (g * jax.nn.sigmoid(g))).astype(BF16)

    units = [(sb, n) for sb in range(LRU_SB) for n in range(LRU_BLOCKS)]
    pending = [project(*u) for u in units[:LRU_AHEAD]]
    for i, unit in enumerate(units):
        ug = pending.pop(0)
        if i + LRU_AHEAD < len(units):
            pending.append(project(*units[i + LRU_AHEAD]))
        recur(*unit, ug)

    for sb in range(LRU_SB):
        y = jnp.dot(perm_ref[1], ybuf[sb], preferred_element_type=F32)
        y_ref[sb * tm:(sb + 1) * tm, :] = y.astype(y_ref.dtype)


def _lru(h, w_in, conv_w, conv_b, wax, bax, lam, batch, seq):
    m, d = h.shape
    d_rnn = lam.shape[1]
    rows = LRU_SB * LRU_TM
    nt = seq // rows
    perm = _time_permutation()
    perms = jnp.asarray(np.stack([perm, perm.T]), BF16)
    const = lambda b, t: (0, 0)
    const3 = lambda b, t: (0, 0, 0)
    hist = (CONV_W - 1) * SUBLANES
    block_f32 = _nbytes((LRU_TM, LRU_BLOCK_W), F32)
    vmem = _vmem_limit(
        pipelined=[_nbytes((rows, d), F32), _nbytes((rows, d_rnn), BF16),
                   _nbytes((2, LRU_TM, LRU_TM), BF16),
                   _nbytes((LRU_BLOCKS, LRU_BLOCK_W, 2 * LRU_BLOCK_W), BF16),
                   _nbytes((LRU_BLOCKS, SUBLANES, 2 * LRU_BLOCK_W), F32),
                   (CONV_W + 3 * SUBLANES) * d_rnn * 4],
        resident=[_nbytes((d, 2 * d_rnn), BF16), _nbytes((rows, d), BF16),
                  _nbytes((rows, d_rnn), BF16), _nbytes((hist + SUBLANES, d_rnn), F32)],
        temporaries=[_nbytes((LRU_TM, d), F32), (LRU_AHEAD + 1) * 2 * block_f32,
                     12 * block_f32])
    return pl.pallas_call(
        _lru_kernel,
        grid=(batch, nt),
        in_specs=[
            pl.BlockSpec((rows, d), lambda b, t: (b * nt + t, 0)),
            pl.BlockSpec((2, LRU_TM, LRU_TM), const3),
            pl.BlockSpec((d, 2 * d_rnn), const, pipeline_mode=pl.Buffered(1)),
            pl.BlockSpec((CONV_W, d_rnn), const),
            pl.BlockSpec((1, d_rnn), const),
            pl.BlockSpec((LRU_BLOCKS, LRU_BLOCK_W, 2 * LRU_BLOCK_W), const3),
            pl.BlockSpec((LRU_BLOCKS, 1, 2 * LRU_BLOCK_W), const3),
            pl.BlockSpec((1, d_rnn), const),
        ],
        out_specs=pl.BlockSpec((rows, d_rnn), lambda b, t: (b * nt + t, 0)),
        out_shape=jax.ShapeDtypeStruct((m, d_rnn), BF16),
        scratch_shapes=[
            pltpu.VMEM((LRU_SB, LRU_TM, d), BF16),
            pltpu.VMEM((LRU_SB, LRU_TM, d_rnn), BF16),
            pltpu.VMEM((hist, d_rnn), F32),
            pltpu.VMEM((1, d_rnn), F32),
        ],
        compiler_params=pltpu.CompilerParams(
            dimension_semantics=("parallel", "arbitrary"),
            vmem_limit_bytes=vmem),
        name="rglru",
    )(h, perms, w_in, conv_w, conv_b, wax, bax, lam)


def kernel(x, attn_w_in, attn_w_out, attn_rel_bias, lru_w_in, lru_conv_w, lru_conv_b,
           lru_wa, lru_ba, lru_wx, lru_bx, lru_lambda, lru_w_out, ln_gain, ln_bias):
    batch, seq, d_model = x.shape
    m = batch * seq
    d_att = ATT_HEADS * HEAD_DIM
    h = x.reshape(m, d_model)

    col_scale = jnp.concatenate([jnp.full((d_att,), Q_SCALE, F32),
                                 jnp.ones((3 * d_att,), F32)])[None, :]
    qkvg = _proj(h, attn_w_in[0].astype(BF16), col_scale)
    og = _attention(qkvg, _attention_bias_tiles(attn_rel_bias[0]), batch, seq)
    h = _out_ln(og, attn_w_out[0].astype(BF16), h, ln_gain[0][None, :], ln_bias[0][None, :],
                "attn_out_ln")

    wax = jnp.concatenate([lru_wa[0], lru_wx[0]], axis=-1).astype(BF16)
    bax = jnp.concatenate([lru_ba[0], lru_bx[0]], axis=-1)[:, None, :]
    y = _lru(h, lru_w_in[0].astype(BF16), lru_conv_w[0], lru_conv_b[0][None, :], wax, bax,
             lru_lambda[0][None, :], batch, seq)
    h = _out_ln(y, lru_w_out[0].astype(BF16), h, ln_gain[1][None, :], ln_bias[1][None, :],
                "lru_out_ln")
    return h.reshape(batch, seq, d_model)
```

```python
import math

import jax
import jax.numpy as jnp
import numpy as np
from jax import lax
from jax.experimental import pallas as pl
from jax.experimental.pallas import tpu as pltpu

DEPTH = 2
CHUNK = 64
LEFT_CHUNKS = 8
LEFT = LEFT_CHUNKS * CHUNK
ATT_HEADS = 16
HEAD_DIM = 128
MAX_REL_DIST = 256
LRU_BLOCKS = 16
LRU_BLOCK_W = 128
CONV_W = 4
RG_C = 8.0
LN_EPS = 1e-5
NEG_INF = -1e30
DEEPNORM_ALPHA = (2.0 * DEPTH) ** 0.25
LOG2_E = math.log2(math.e)
Q_SCALE = HEAD_DIM ** -0.5 * LOG2_E

SUBLANES = 8
MIB = 1024 * 1024
V7X_VMEM_BYTES = 64 * MIB

PROJ_TM = 1024
PROJ_TN = 2048
ATT_TQ = 256
ATT_WIN = LEFT + ATT_TQ
ATT_HB = 8
ATT_QB = 4
ATT_AHEAD = 2
ATT_ONES_ROWS = 16
ATT_PIECE = 128
OUT_TM = 1024
OUT_SUB = 256
OUT_TN = 256
OUT_AHEAD = 2
LRU_TM = 256
LRU_SB = 2
LRU_SUB = LRU_TM // SUBLANES
LRU_AHEAD = 2

F32 = jnp.float32
BF16 = jnp.bfloat16


def _nbytes(shape, dtype):
    return math.prod(shape) * jnp.dtype(dtype).itemsize


def _vmem_limit(pipelined, resident=(), temporaries=()):
    need = 2 * sum(pipelined) + sum(resident) + sum(temporaries)
    assert need <= V7X_VMEM_BYTES, f"{need} bytes of VMEM requested"
    return need


def _proj_kernel(x_ref, w_ref, s_ref, o_ref):
    x = x_ref[...].astype(BF16)
    acc = jnp.dot(x, w_ref[...], preferred_element_type=F32)
    o_ref[...] = (acc * s_ref[...]).astype(o_ref.dtype)


def _proj(x, w, col_scale):
    m, k = x.shape
    n = w.shape[1]
    vmem = _vmem_limit(
        pipelined=[_nbytes((PROJ_TM, k), F32), _nbytes((k, PROJ_TN), BF16),
                   _nbytes((SUBLANES, PROJ_TN), F32), _nbytes((PROJ_TM, PROJ_TN), BF16)],
        temporaries=[_nbytes((PROJ_TM, k), BF16), _nbytes((PROJ_TM, PROJ_TN), F32)])
    return pl.pallas_call(
        _proj_kernel,
        grid=(m // PROJ_TM, n // PROJ_TN),
        in_specs=[
            pl.BlockSpec((PROJ_TM, k), lambda i, j: (i, 0)),
            pl.BlockSpec((k, PROJ_TN), lambda i, j: (0, j)),
            pl.BlockSpec((1, PROJ_TN), lambda i, j: (0, j)),
        ],
        out_specs=pl.BlockSpec((PROJ_TM, PROJ_TN), lambda i, j: (i, j)),
        out_shape=jax.ShapeDtypeStruct((m, n), BF16),
        compiler_params=pltpu.CompilerParams(
            dimension_semantics=("parallel", "arbitrary"),
            vmem_limit_bytes=vmem),
        name="attn_in_proj",
    )(x, w, col_scale)


def _band_allowed(i, j):
    dchunk = i // CHUNK - j // CHUNK
    return (dchunk >= 0) & (dchunk <= LEFT_CHUNKS)


def _live_pieces():
    i = np.arange(ATT_TQ)[None, :]
    live = []
    for c in range(ATT_WIN // ATT_TQ):
        j = np.arange(ATT_TQ)[:, None] + c * ATT_TQ - LEFT
        ok = _band_allowed(i, j)
        n_half = ATT_TQ // ATT_PIECE
        live.append([(r, l) for r in range(n_half) for l in range(n_half)
                     if ok[r * ATT_PIECE:(r + 1) * ATT_PIECE,
                           l * ATT_PIECE:(l + 1) * ATT_PIECE].any()])
    return live


_LIVE_PIECES = _live_pieces()


def _attention_kernel(q_ref, k_ref, v_ref, g_ref, bias_ref, o_ref, vt_ref):
    n_kblk = ATT_WIN // ATT_TQ

    def head_cols(hh):
        return slice(hh * HEAD_DIM, (hh + 1) * HEAD_DIM)

    def window(qblk):
        key_blk, key_row, bias_row = [], [], []
        for c in range(n_kblk):
            rel = qblk - LEFT // ATT_TQ + c
            blk = jnp.maximum(rel, 0)
            key_blk.append(blk)
            key_row.append(pl.multiple_of(blk * ATT_TQ, ATT_TQ))
            bias_row.append(pl.multiple_of(
                jnp.where(rel >= 0, c * ATT_TQ, LEFT + ATT_WIN - ATT_TQ), ATT_TQ))
        return key_blk, key_row, bias_row

    qblks = [pl.program_id(2) * ATT_QB + j for j in range(ATT_QB)]
    windows = [window(qblk) for qblk in qblks]

    for qblk in qblks:
        q_row = pl.multiple_of(qblk * ATT_TQ, ATT_TQ)
        for hh in range(ATT_HB):
            vt_ref[hh, qblk, :HEAD_DIM, :] = v_ref[pl.ds(q_row, ATT_TQ), head_cols(hh)].T
            vt_ref[hh, qblk, HEAD_DIM:, :] = jnp.ones((ATT_ONES_ROWS, ATT_TQ), BF16)

    def scores(j, hh):
        _, key_row, bias_row = windows[j]
        q = q_ref[j * ATT_TQ:(j + 1) * ATT_TQ, head_cols(hh)]
        pieces = {}
        for c in range(n_kblk):
            k = k_ref[pl.ds(key_row[c], ATT_TQ), head_cols(hh)]
            st = lax.dot_general(k, q, (((1,), (1,)), ((), ())), preferred_element_type=F32)
            for r, l in _LIVE_PIECES[c]:
                rows = slice(r * ATT_PIECE, (r + 1) * ATT_PIECE)
                lanes = slice(l * ATT_PIECE, (l + 1) * ATT_PIECE)
                bias = bias_ref[hh, pl.ds(bias_row[c] + r * ATT_PIECE, ATT_PIECE), lanes]
                pieces[c, r, l] = st[rows, lanes] + bias
        return pieces

    def attend(j, hh, pieces):
        key_blk = windows[j][0]
        n_half = ATT_TQ // ATT_PIECE
        probs = {}
        for l in range(n_half):
            live = [key for key in pieces if key[2] == l]
            m = jnp.max(pieces[live[0]], axis=0, keepdims=True)
            for key in live[1:]:
                m = jnp.maximum(m, jnp.max(pieces[key], axis=0, keepdims=True))
            for key in live:
                probs[key] = jnp.exp2(pieces[key] - m).astype(BF16)
        zero = jnp.zeros((ATT_PIECE, ATT_PIECE), BF16)
        ot = None
        for c in range(n_kblk):
            p = jnp.concatenate(
                [jnp.concatenate([probs.get((c, r, l), zero) for l in range(n_half)], axis=1)
                 for r in range(n_half)], axis=0)
            oc = jnp.dot(vt_ref[hh, key_blk[c]], p, preferred_element_type=F32)
            ot = oc if ot is None else ot + oc
        return ot

    def finish(j, hh, ot):
        rows = slice(j * ATT_TQ, (j + 1) * ATT_TQ)
        o = (ot[:HEAD_DIM] / ot[HEAD_DIM:HEAD_DIM + 1]).T
        g = g_ref[rows, head_cols(hh)].astype(F32)
        o_ref[rows, head_cols(hh)] = (o * (g * jax.nn.sigmoid(g))).astype(o_ref.dtype)

    units = [(j, hh) for j in range(ATT_QB) for hh in range(ATT_HB)]
    pending = [scores(*u) for u in units[:ATT_AHEAD]]
    for n, unit in enumerate(units):
        pieces = pending.pop(0)
        if n + ATT_AHEAD < len(units):
            pending.append(scores(*units[n + ATT_AHEAD]))
        finish(*unit, attend(*unit, pieces))


def _attention_bias_tiles(rel_table):
    n_heads = rel_table.shape[0]
    n_rel = LEFT + ATT_WIN
    i = np.arange(ATT_TQ)[None, :]
    j = np.arange(n_rel)[:, None] - LEFT
    span = ATT_TQ + n_rel - 1
    edge = (span - (2 * MAX_REL_DIST + 1)) // 2
    ext = jnp.pad(rel_table.astype(F32) * LOG2_E, ((0, 0), (edge, edge)), mode="edge")
    sq = math.isqrt(ATT_TQ)
    wide = n_rel - sq + ATT_TQ
    fine = jnp.stack([ext[:, sq - 1 - t:sq - 1 - t + wide] for t in range(sq)], axis=1)
    b = jnp.stack([fine[:, :, n_rel - sq * (p + 1):n_rel - sq * (p + 1) + ATT_TQ]
                   for p in range(n_rel // sq)], axis=1)
    b = b.reshape(n_heads, n_rel, ATT_TQ)
    return jnp.where(jnp.asarray(_band_allowed(i, j))[None], b, NEG_INF)


def _attention(qkvg, bias_tiles, batch, seq):
    m = qkvg.shape[0]
    d_att = ATT_HEADS * HEAD_DIM
    hw = ATT_HB * HEAD_DIM
    ncol = d_att // hw
    rows = ATT_QB * ATT_TQ
    nq = seq // rows
    vt_shape = (ATT_HB, seq // ATT_TQ, HEAD_DIM + ATT_ONES_ROWS, ATT_TQ)
    vmem = _vmem_limit(
        pipelined=[3 * _nbytes((rows, hw), BF16), 2 * _nbytes((seq, hw), BF16),
                   _nbytes((ATT_HB, LEFT + ATT_WIN, ATT_TQ), F32)],
        resident=[_nbytes(vt_shape, BF16)],
        temporaries=[(ATT_AHEAD + 1) * _nbytes((ATT_WIN, ATT_TQ), F32),
                     _nbytes((ATT_WIN, ATT_TQ), BF16)])
    return pl.pallas_call(
        _attention_kernel,
        grid=(ATT_HEADS // ATT_HB, batch, nq),
        in_specs=[
            pl.BlockSpec((rows, hw), lambda h, b, q: (b * nq + q, h)),
            pl.BlockSpec((seq, hw), lambda h, b, q: (b, ncol + h)),
            pl.BlockSpec((seq, hw), lambda h, b, q: (b, 2 * ncol + h)),
            pl.BlockSpec((rows, hw), lambda h, b, q: (b * nq + q, 3 * ncol + h)),
            pl.BlockSpec((ATT_HB, LEFT + ATT_WIN, ATT_TQ), lambda h, b, q: (h, 0, 0)),
        ],
        out_specs=pl.BlockSpec((rows, hw), lambda h, b, q: (b * nq + q, h)),
        out_shape=jax.ShapeDtypeStruct((m, d_att), BF16),
        scratch_shapes=[pltpu.VMEM(vt_shape, BF16)],
        compiler_params=pltpu.CompilerParams(
            dimension_semantics=("parallel", "parallel", "arbitrary"),
            vmem_limit_bytes=vmem),
        name="chunk_attention",
    )(qkvg, qkvg, qkvg, qkvg, bias_tiles)


def _out_ln_kernel(y_ref, w_ref, r_ref, gain_ref, bias_ref, o_ref, z_ref):
    n = w_ref.shape[1]
    n_tiles = n // OUT_TN

    def rows_of(i):
        return slice(i * OUT_SUB, (i + 1) * OUT_SUB)

    def cols_of(t):
        return slice(t * OUT_TN, (t + 1) * OUT_TN)

    def project(i, t):
        return jnp.dot(y_ref[rows_of(i), :], w_ref[:, cols_of(t)], preferred_element_type=F32)

    always = pl.program_id(0) >= 0

    def normalise(i, tt, mu, rstd, anchor=None):
        zt = z_ref[rows_of(i), cols_of(tt)]
        out = (zt - mu) * rstd * gain_ref[:, cols_of(tt)] + bias_ref[:, cols_of(tt)]
        if anchor is not None:
            out = jnp.where(always, out, anchor)
        o_ref[rows_of(i), cols_of(tt)] = out

    units = [(i, t) for i in range(OUT_TM // OUT_SUB) for t in range(n_tiles)]
    pending = [project(*u) for u in units[:OUT_AHEAD]]
    jobs = []
    shift = s1 = s2 = None
    for k, (i, t) in enumerate(units):
        acc = pending.pop(0)
        if k + OUT_AHEAD < len(units):
            pending.append(project(*units[k + OUT_AHEAD]))
        z = DEEPNORM_ALPHA * r_ref[rows_of(i), cols_of(t)] + acc
        z_ref[rows_of(i), cols_of(t)] = z
        if t == 0:
            shift = jnp.mean(z, axis=-1, keepdims=True)
        d = z - shift
        d1 = jnp.sum(d, axis=-1, keepdims=True)
        d2 = jnp.sum(d * d, axis=-1, keepdims=True)
        s1 = d1 if t == 0 else s1 + d1
        s2 = d2 if t == 0 else s2 + d2
        if jobs:
            normalise(*jobs.pop(0), anchor=acc)
        if t == n_tiles - 1:
            mean_d = s1 * (1.0 / n)
            var = s2 * (1.0 / n) - mean_d * mean_d
            mu = shift + mean_d
            rstd = lax.rsqrt(var + LN_EPS)
            jobs += [(i, tt, mu, rstd) for tt in range(n_tiles)]
    for job in jobs:
        normalise(*job)


def _out_ln(y, w, resid, gain, bias, name):
    m, k = y.shape
    n = w.shape[1]
    const = lambda i: (0, 0)
    vmem = _vmem_limit(
        pipelined=[_nbytes((OUT_TM, k), BF16), 2 * _nbytes((OUT_TM, n), F32),
                   2 * _nbytes((SUBLANES, n), F32)],
        resident=[_nbytes((k, n), BF16), _nbytes((OUT_TM, n), F32)],
        temporaries=[(OUT_AHEAD + 1) * _nbytes((OUT_SUB, OUT_TN), F32),
                     _nbytes((OUT_SUB, n), F32)])
    return pl.pallas_call(
        _out_ln_kernel,
        grid=(m // OUT_TM,),
        in_specs=[
            pl.BlockSpec((OUT_TM, k), lambda i: (i, 0)),
            pl.BlockSpec((k, n), const, pipeline_mode=pl.Buffered(1)),
            pl.BlockSpec((OUT_TM, n), lambda i: (i, 0)),
            pl.BlockSpec((1, n), const),
            pl.BlockSpec((1, n), const),
        ],
        out_specs=pl.BlockSpec((OUT_TM, n), lambda i: (i, 0)),
        out_shape=jax.ShapeDtypeStruct((m, n), F32),
        scratch_shapes=[pltpu.VMEM((OUT_TM, n), F32)],
        compiler_params=pltpu.CompilerParams(
            dimension_semantics=("parallel",),
            vmem_limit_bytes=vmem),
        name=name,
    )(y, w, resid, gain, bias)


def _time_permutation():
    r = np.arange(LRU_TM)
    p = np.zeros((LRU_TM, LRU_TM), np.float32)
    p[r, LRU_SUB * (r % SUBLANES) + r // SUBLANES] = 1.0
    return p


def _sublane_scan(a, b):
    row = lax.broadcasted_iota(jnp.int32, a.shape, 0)
    d = 1
    while d < SUBLANES:
        keep = row >= d
        a_sh = jnp.where(keep, pltpu.roll(a, d, 0), 1.0)
        b_sh = jnp.where(keep, pltpu.roll(b, d, 0), 0.0)
        b = a * b_sh + b
        a = a * a_sh
        d *= 2
    return a, b


def _lru_kernel(h_ref, perm_ref, win_ref, cw_ref, cb_ref, wax_ref, bax_ref, lam_ref, y_ref,
                xbuf, ybuf, tailbuf, hcarry):
    t = pl.program_id(1)
    tm = LRU_TM
    hist = (CONV_W - 1) * SUBLANES

    @pl.when(t == 0)
    def _():
        tailbuf[...] = jnp.zeros_like(tailbuf)
        hcarry[...] = jnp.zeros_like(hcarry)

    for sb in range(LRU_SB):
        x = h_ref[sb * tm:(sb + 1) * tm, :].astype(BF16)
        xbuf[sb] = jnp.dot(perm_ref[0], x, preferred_element_type=F32).astype(BF16)

    lam = lam_ref[...]
    softplus_neg_lam = jnp.maximum(-lam, 0.0) + jnp.log1p(jnp.exp(-jnp.abs(lam)))
    log2_a_scale = (-RG_C * LOG2_E) * softplus_neg_lam
    row0 = lax.broadcasted_iota(jnp.int32, (SUBLANES, LRU_BLOCK_W), 0) == 0

    d_rnn = y_ref.shape[1]

    def project(sb, n):
        w = jnp.concatenate(
            [win_ref[:, n * LRU_BLOCK_W:(n + 1) * LRU_BLOCK_W],
             win_ref[:, d_rnn + n * LRU_BLOCK_W:d_rnn + (n + 1) * LRU_BLOCK_W]], axis=1)
        return jnp.dot(xbuf[sb], w, preferred_element_type=F32)

    def recur(sb, n, ug):
        cols = slice(n * LRU_BLOCK_W, (n + 1) * LRU_BLOCK_W)
        u_raw = ug[:, :LRU_BLOCK_W]
        g = ug[:, LRU_BLOCK_W:]
        cur = u_raw[tm - hist:, :]
        prev = tailbuf[:, cols]
        tailbuf[:, cols] = cur
        tiles = []
        for j in range(CONV_W - 1):
            rows = slice(j * SUBLANES, (j + 1) * SUBLANES)
            tiles.append(jnp.where(row0, pltpu.roll(prev[rows], 1, 0),
                                   pltpu.roll(cur[rows], 1, 0)))
        ext = jnp.concatenate(tiles + [u_raw], axis=0)
        u = cb_ref[:, cols]
        for tap in range(CONV_W):
            start = tap * SUBLANES
            u = u + ext[start:start + tm] * cw_ref[tap:tap + 1, cols]
        gates = jnp.dot(u.astype(BF16), wax_ref[n], preferred_element_type=F32)
        gates = jax.nn.sigmoid(gates + bax_ref[n])
        r = gates[:, :LRU_BLOCK_W]
        i = gates[:, LRU_BLOCK_W:]
        a = jnp.exp2(log2_a_scale[:, cols] * r)
        one_m_a2 = 1.0 - a * a
        mult = jnp.where(one_m_a2 > 0.0, one_m_a2 * lax.rsqrt(one_m_a2), 0.0)
        b = mult * (i * u)

        hs, ps = [b[:SUBLANES]], [a[:SUBLANES]]
        for k in range(1, LRU_SUB):
            rows = slice(k * SUBLANES, (k + 1) * SUBLANES)
            hs.append(a[rows] * hs[-1] + b[rows])
            ps.append(a[rows] * ps[-1])
        p_inc, h_inc = _sublane_scan(ps[-1], hs[-1])
        h0 = hcarry[:, cols]
        row = lax.broadcasted_iota(jnp.int32, p_inc.shape, 0)
        p_exc = jnp.where(row == 0, 1.0, pltpu.roll(p_inc, 1, 0))
        h_exc = jnp.where(row == 0, 0.0, pltpu.roll(h_inc, 1, 0))
        carry = p_exc * h0 + h_exc
        hcarry[:, cols] = (p_inc * h0 + h_inc)[SUBLANES - 1:SUBLANES, :]
        h = jnp.concatenate([hk + pk * carry for hk, pk in zip(hs, ps)], axis=0)
        ybuf[sb, :, cols] = (h * (g * jax.nn.sigmoid(g))).astype(BF16)

    units = [(sb, n) for sb in range(LRU_SB) for n in range(LRU_BLOCKS)]
    pending = [project(*u) for u in units[:LRU_AHEAD]]
    for i, unit in enumerate(units):
        ug = pending.pop(0)
        if i + LRU_AHEAD < len(units):
            pending.append(project(*units[i + LRU_AHEAD]))
        recur(*unit, ug)

    for sb in range(LRU_SB):
        y = jnp.dot(perm_ref[1], ybuf[sb], preferred_element_type=F32)
        y_ref[sb * tm:(sb + 1) * tm, :] = y.astype(y_ref.dtype)


def _lru(h, w_in, conv_w, conv_b, wax, bax, lam, batch, seq):
    m, d = h.shape
    d_rnn = lam.shape[1]
    rows = LRU_SB * LRU_TM
    nt = seq // rows
    perm = _time_permutation()
    perms = jnp.asarray(np.stack([perm, perm.T]), BF16)
    const = lambda b, t: (0, 0)
    const3 = lambda b, t: (0, 0, 0)
    hist = (CONV_W - 1) * SUBLANES
    block_f32 = _nbytes((LRU_TM, LRU_BLOCK_W), F32)
    vmem = _vmem_limit(
        pipelined=[_nbytes((rows, d), F32), _nbytes((rows, d_rnn), BF16),
                   _nbytes((2, LRU_TM, LRU_TM), BF16),
                   _nbytes((LRU_BLOCKS, LRU_BLOCK_W, 2 * LRU_BLOCK_W), BF16),
                   _nbytes((LRU_BLOCKS, SUBLANES, 2 * LRU_BLOCK_W), F32),
                   (CONV_W + 3 * SUBLANES) * d_rnn * 4],
        resident=[_nbytes((d, 2 * d_rnn), BF16), _nbytes((rows, d), BF16),
                  _nbytes((rows, d_rnn), BF16), _nbytes((hist + SUBLANES, d_rnn), F32)],
        temporaries=[_nbytes((LRU_TM, d), F32), (LRU_AHEAD + 1) * 2 * block_f32,
                     12 * block_f32])
    return pl.pallas_call(
        _lru_kernel,
        grid=(batch, nt),
        in_specs=[
            pl.BlockSpec((rows, d), lambda b, t: (b * nt + t, 0)),
            pl.BlockSpec((2, LRU_TM, LRU_TM), const3),
            pl.BlockSpec((d, 2 * d_rnn), const, pipeline_mode=pl.Buffered(1)),
            pl.BlockSpec((CONV_W, d_rnn), const),
            pl.BlockSpec((1, d_rnn), const),
            pl.BlockSpec((LRU_BLOCKS, LRU_BLOCK_W, 2 * LRU_BLOCK_W), const3),
            pl.BlockSpec((LRU_BLOCKS, 1, 2 * LRU_BLOCK_W), const3),
            pl.BlockSpec((1, d_rnn), const),
        ],
        out_specs=pl.BlockSpec((rows, d_rnn), lambda b, t: (b * nt + t, 0)),
        out_shape=jax.ShapeDtypeStruct((m, d_rnn), BF16),
        scratch_shapes=[
            pltpu.VMEM((LRU_SB, LRU_TM, d), BF16),
            pltpu.VMEM((LRU_SB, LRU_TM, d_rnn), BF16),
            pltpu.VMEM((hist, d_rnn), F32),
            pltpu.VMEM((1, d_rnn), F32),
        ],
        compiler_params=pltpu.CompilerParams(
            dimension_semantics=("parallel", "arbitrary"),
            vmem_limit_bytes=vmem),
        name="rglru",
    )(h, perms, w_in, conv_w, conv_b, wax, bax, lam)


def kernel(x, attn_w_in, attn_w_out, attn_rel_bias, lru_w_in, lru_conv_w, lru_conv_b,
           lru_wa, lru_ba, lru_wx, lru_bx, lru_lambda, lru_w_out, ln_gain, ln_bias):
    batch, seq, d_model = x.shape
    m = batch * seq
    d_att = ATT_HEADS * HEAD_DIM
    h = x.reshape(m, d_model)

    col_scale = jnp.concatenate([jnp.full((d_att,), Q_SCALE, F32),
                                 jnp.ones((3 * d_att,), F32)])[None, :]
    qkvg = _proj(h, attn_w_in[0].astype(BF16), col_scale)
    og = _attention(qkvg, _attention_bias_tiles(attn_rel_bias[0]), batch, seq)
    h = _out_ln(og, attn_w_out[0].astype(BF16), h, ln_gain[0][None, :], ln_bias[0][None, :],
                "attn_out_ln")

    wax = jnp.concatenate([lru_wa[0], lru_wx[0]], axis=-1).astype(BF16)
    bax = jnp.concatenate([lru_ba[0], lru_bx[0]], axis=-1)[:, None, :]
    y = _lru(h, lru_w_in[0].astype(BF16), lru_conv_w[0], lru_conv_b[0][None, :], wax, bax,
             lru_lambda[0][None, :], batch, seq)
    h = _out_ln(y, lru_w_out[0].astype(BF16), h, ln_gain[1][None, :], ln_bias[1][None, :],
                "lru_out_ln")
    return h.reshape(batch, seq, d_model)
```

```python
import math

import jax
import jax.numpy as jnp
import numpy as np
from jax import lax
from jax.experimental import pallas as pl
from jax.experimental.pallas import tpu as pltpu

DEPTH = 2
CHUNK = 64
LEFT_CHUNKS = 8
LEFT = LEFT_CHUNKS * CHUNK
ATT_HEADS = 16
HEAD_DIM = 128
MAX_REL_DIST = 256
LRU_BLOCKS = 16
LRU_BLOCK_W = 128
CONV_W = 4
RG_C = 8.0
LN_EPS = 1e-5
NEG_INF = -1e30
DEEPNORM_ALPHA = (2.0 * DEPTH) ** 0.25
LOG2_E = math.log2(math.e)
Q_SCALE = HEAD_DIM ** -0.5 * LOG2_E

SUBLANES = 8
MIB = 1024 * 1024
V7X_VMEM_BYTES = 64 * MIB

PROJ_TM = 1024
PROJ_TN = 2048
ATT_TQ = 256
ATT_WIN = LEFT + ATT_TQ
ATT_HB = 8
ATT_QB = 4
ATT_AHEAD = 2
ATT_ONES_ROWS = 16
ATT_PIECE = 128
OUT_TM = 1024
OUT_SUB = 256
OUT_TN = 256
OUT_AHEAD = 2
LRU_TM = 256
LRU_SB = 2
LRU_SUB = LRU_TM // SUBLANES
LRU_AHEAD = 2

F32 = jnp.float32
BF16 = jnp.bfloat16


def _nbytes(shape, dtype):
    return math.prod(shape) * jnp.dtype(dtype).itemsize


def _vmem_limit(pipelined, resident=(), temporaries=()):
    need = 2 * sum(pipelined) + sum(resident) + sum(temporaries)
    assert need <= V7X_VMEM_BYTES, f"{need} bytes of VMEM requested"
    return need


def _proj_kernel(x_ref, w_ref, s_ref, o_ref):
    x = x_ref[...].astype(BF16)
    acc = jnp.dot(x, w_ref[...], preferred_element_type=F32)
    o_ref[...] = (acc * s_ref[...]).astype(o_ref.dtype)


def _proj(x, w, col_scale):
    m, k = x.shape
    n = w.shape[1]
    vmem = _vmem_limit(
        pipelined=[_nbytes((PROJ_TM, k), F32), _nbytes((k, PROJ_TN), BF16),
                   _nbytes((SUBLANES, PROJ_TN), F32), _nbytes((PROJ_TM, PROJ_TN), BF16)],
        temporaries=[_nbytes((PROJ_TM, k), BF16), _nbytes((PROJ_TM, PROJ_TN), F32)])
    return pl.pallas_call(
        _proj_kernel,
        grid=(m // PROJ_TM, n // PROJ_TN),
        in_specs=[
            pl.BlockSpec((PROJ_TM, k), lambda i, j: (i, 0)),
            pl.BlockSpec((k, PROJ_TN), lambda i, j: (0, j)),
            pl.BlockSpec((1, PROJ_TN), lambda i, j: (0, j)),
        ],
        out_specs=pl.BlockSpec((PROJ_TM, PROJ_TN), lambda i, j: (i, j)),
        out_shape=jax.ShapeDtypeStruct((m, n), BF16),
        compiler_params=pltpu.CompilerParams(
            dimension_semantics=("parallel", "arbitrary"),
            vmem_limit_bytes=vmem),
        name="attn_in_proj",
    )(x, w, col_scale)


def _band_allowed(i, j):
    dchunk = i // CHUNK - j // CHUNK
    return (dchunk >= 0) & (dchunk <= LEFT_CHUNKS)


def _live_pieces():
    i = np.arange(ATT_TQ)[None, :]
    live = []
    for c in range(ATT_WIN // ATT_TQ):
        j = np.arange(ATT_TQ)[:, None] + c * ATT_TQ - LEFT
        ok = _band_allowed(i, j)
        n_half = ATT_TQ // ATT_PIECE
        live.append([(r, l) for r in range(n_half) for l in range(n_half)
                     if ok[r * ATT_PIECE:(r + 1) * ATT_PIECE,
                           l * ATT_PIECE:(l + 1) * ATT_PIECE].any()])
    return live


_LIVE_PIECES = _live_pieces()


def _attention_kernel(q_ref, k_ref, v_ref, g_ref, bias_ref, o_ref, vt_ref):
    n_kblk = ATT_WIN // ATT_TQ

    def head_cols(hh):
        return slice(hh * HEAD_DIM, (hh + 1) * HEAD_DIM)

    def window(qblk):
        key_blk, key_row, bias_row = [], [], []
        for c in range(n_kblk):
            rel = qblk - LEFT // ATT_TQ + c
            blk = jnp.maximum(rel, 0)
            key_blk.append(blk)
            key_row.append(pl.multiple_of(blk * ATT_TQ, ATT_TQ))
            bias_row.append(pl.multiple_of(
                jnp.where(rel >= 0, c * ATT_TQ, LEFT + ATT_WIN - ATT_TQ), ATT_TQ))
        return key_blk, key_row, bias_row

    qblks = [pl.program_id(2) * ATT_QB + j for j in range(ATT_QB)]
    windows = [window(qblk) for qblk in qblks]

    for qblk in qblks:
        q_row = pl.multiple_of(qblk * ATT_TQ, ATT_TQ)
        for hh in range(ATT_HB):
            vt_ref[hh, qblk, :HEAD_DIM, :] = v_ref[pl.ds(q_row, ATT_TQ), head_cols(hh)].T
            vt_ref[hh, qblk, HEAD_DIM:, :] = jnp.ones((ATT_ONES_ROWS, ATT_TQ), BF16)

    def scores(j, hh):
        _, key_row, bias_row = windows[j]
        q = q_ref[j * ATT_TQ:(j + 1) * ATT_TQ, head_cols(hh)]
        pieces = {}
        for c in range(n_kblk):
            k = k_ref[pl.ds(key_row[c], ATT_TQ), head_cols(hh)]
            st = lax.dot_general(k, q, (((1,), (1,)), ((), ())), preferred_element_type=F32)
            for r, l in _LIVE_PIECES[c]:
                rows = slice(r * ATT_PIECE, (r + 1) * ATT_PIECE)
                lanes = slice(l * ATT_PIECE, (l + 1) * ATT_PIECE)
                bias = bias_ref[hh, pl.ds(bias_row[c] + r * ATT_PIECE, ATT_PIECE), lanes]
                pieces[c, r, l] = st[rows, lanes] + bias
        return pieces

    def attend(j, hh, pieces):
        key_blk = windows[j][0]
        n_half = ATT_TQ // ATT_PIECE
        probs = {}
        for l in range(n_half):
            live = [key for key in pieces if key[2] == l]
            m = jnp.max(pieces[live[0]], axis=0, keepdims=True)
            for key in live[1:]:
                m = jnp.maximum(m, jnp.max(pieces[key], axis=0, keepdims=True))
            for key in live:
                probs[key] = jnp.exp2(pieces[key] - m).astype(BF16)
        zero = jnp.zeros((ATT_PIECE, ATT_PIECE), BF16)
        ot = None
        for c in range(n_kblk):
            p = jnp.concatenate(
                [jnp.concatenate([probs.get((c, r, l), zero) for l in range(n_half)], axis=1)
                 for r in range(n_half)], axis=0)
            oc = jnp.dot(vt_ref[hh, key_blk[c]], p, preferred_element_type=F32)
            ot = oc if ot is None else ot + oc
        return ot

    def finish(j, hh, ot):
        rows = slice(j * ATT_TQ, (j + 1) * ATT_TQ)
        o = (ot[:HEAD_DIM] / ot[HEAD_DIM:HEAD_DIM + 1]).T
        g = g_ref[rows, head_cols(hh)].astype(F32)
        o_ref[rows, head_cols(hh)] = (o * (g * jax.nn.sigmoid(g))).astype(o_ref.dtype)

    units = [(j, hh) for j in range(ATT_QB) for hh in range(ATT_HB)]
    pending = [scores(*u) for u in units[:ATT_AHEAD]]
    for n, unit in enumerate(units):
        pieces = pending.pop(0)
        if n + ATT_AHEAD < len(units):
            pending.append(scores(*units[n + ATT_AHEAD]))
        finish(*unit, attend(*unit, pieces))


def _attention_bias_tiles(rel_table):
    n_heads = rel_table.shape[0]
    n_rel = LEFT + ATT_WIN
    i = np.arange(ATT_TQ)[None, :]
    j = np.arange(n_rel)[:, None] - LEFT
    span = ATT_TQ + n_rel - 1
    edge = (span - (2 * MAX_REL_DIST + 1)) // 2
    ext = jnp.pad(rel_table.astype(F32), ((0, 0), (edge, edge)), mode="edge")
    sq = math.isqrt(ATT_TQ)
    wide = n_rel - sq + ATT_TQ
    fine = jnp.stack([ext[:, sq - 1 - t:sq - 1 - t + wide] for t in range(sq)], axis=1)
    b = jnp.stack([fine[:, :, n_rel - sq * (p + 1):n_rel - sq * (p + 1) + ATT_TQ]
                   for p in range(n_rel // sq)], axis=1)
    b = b.reshape(n_heads, n_rel, ATT_TQ)
    mask = np.where(_band_allowed(i, j), 0.0, NEG_INF).astype(np.float32)
    return b * LOG2_E + jnp.asarray(mask)[None]


def _attention(qkvg, bias_tiles, batch, seq):
    m = qkvg.shape[0]
    d_att = ATT_HEADS * HEAD_DIM
    hw = ATT_HB * HEAD_DIM
    ncol = d_att // hw
    rows = ATT_QB * ATT_TQ
    nq = seq // rows
    vt_shape = (ATT_HB, seq // ATT_TQ, HEAD_DIM + ATT_ONES_ROWS, ATT_TQ)
    vmem = _vmem_limit(
        pipelined=[3 * _nbytes((rows, hw), BF16), 2 * _nbytes((seq, hw), BF16),
                   _nbytes((ATT_HB, LEFT + ATT_WIN, ATT_TQ), F32)],
        resident=[_nbytes(vt_shape, BF16)],
        temporaries=[(ATT_AHEAD + 1) * _nbytes((ATT_WIN, ATT_TQ), F32),
                     _nbytes((ATT_WIN, ATT_TQ), BF16)])
    return pl.pallas_call(
        _attention_kernel,
        grid=(ATT_HEADS // ATT_HB, batch, nq),
        in_specs=[
            pl.BlockSpec((rows, hw), lambda h, b, q: (b * nq + q, h)),
            pl.BlockSpec((seq, hw), lambda h, b, q: (b, ncol + h)),
            pl.BlockSpec((seq, hw), lambda h, b, q: (b, 2 * ncol + h)),
            pl.BlockSpec((rows, hw), lambda h, b, q: (b * nq + q, 3 * ncol + h)),
            pl.BlockSpec((ATT_HB, LEFT + ATT_WIN, ATT_TQ), lambda h, b, q: (h, 0, 0)),
        ],
        out_specs=pl.BlockSpec((rows, hw), lambda h, b, q: (b * nq + q, h)),
        out_shape=jax.ShapeDtypeStruct((m, d_att), BF16),
        scratch_shapes=[pltpu.VMEM(vt_shape, BF16)],
        compiler_params=pltpu.CompilerParams(
            dimension_semantics=("parallel", "parallel", "arbitrary"),
            vmem_limit_bytes=vmem),
        name="chunk_attention",
    )(qkvg, qkvg, qkvg, qkvg, bias_tiles)


def _out_ln_kernel(y_ref, w_ref, r_ref, gain_ref, bias_ref, o_ref, z_ref):
    n = w_ref.shape[1]
    n_tiles = n // OUT_TN

    def rows_of(i):
        return slice(i * OUT_SUB, (i + 1) * OUT_SUB)

    def cols_of(t):
        return slice(t * OUT_TN, (t + 1) * OUT_TN)

    def project(i, t):
        return jnp.dot(y_ref[rows_of(i), :], w_ref[:, cols_of(t)], preferred_element_type=F32)

    always = pl.program_id(0) >= 0

    def normalise(i, tt, mu, rstd, anchor=None):
        zt = z_ref[rows_of(i), cols_of(tt)]
        out = (zt - mu) * rstd * gain_ref[:, cols_of(tt)] + bias_ref[:, cols_of(tt)]
        if anchor is not None:
            out = jnp.where(always, out, anchor)
        o_ref[rows_of(i), cols_of(tt)] = out

    units = [(i, t) for i in range(OUT_TM // OUT_SUB) for t in range(n_tiles)]
    pending = [project(*u) for u in units[:OUT_AHEAD]]
    jobs = []
    shift = s1 = s2 = None
    for k, (i, t) in enumerate(units):
        acc = pending.pop(0)
        if k + OUT_AHEAD < len(units):
            pending.append(project(*units[k + OUT_AHEAD]))
        z = DEEPNORM_ALPHA * r_ref[rows_of(i), cols_of(t)] + acc
        z_ref[rows_of(i), cols_of(t)] = z
        if t == 0:
            shift = jnp.mean(z, axis=-1, keepdims=True)
        d = z - shift
        d1 = jnp.sum(d, axis=-1, keepdims=True)
        d2 = jnp.sum(d * d, axis=-1, keepdims=True)
        s1 = d1 if t == 0 else s1 + d1
        s2 = d2 if t == 0 else s2 + d2
        if jobs:
            normalise(*jobs.pop(0), anchor=acc)
        if t == n_tiles - 1:
            mean_d = s1 * (1.0 / n)
            var = s2 * (1.0 / n) - mean_d * mean_d
            mu = shift + mean_d
            rstd = lax.rsqrt(var + LN_EPS)
            jobs += [(i, tt, mu, rstd) for tt in range(n_tiles)]
    for job in jobs:
        normalise(*job)


def _out_ln(y, w, resid, gain, bias, name):
    m, k = y.shape
    n = w.shape[1]
    const = lambda i: (0, 0)
    vmem = _vmem_limit(
        pipelined=[_nbytes((OUT_TM, k), BF16), 2 * _nbytes((OUT_TM, n), F32),
                   2 * _nbytes((SUBLANES, n), F32)],
        resident=[_nbytes((k, n), BF16), _nbytes((OUT_TM, n), F32)],
        temporaries=[(OUT_AHEAD + 1) * _nbytes((OUT_SUB, OUT_TN), F32),
                     _nbytes((OUT_SUB, n), F32)])
    return pl.pallas_call(
        _out_ln_kernel,
        grid=(m // OUT_TM,),
        in_specs=[
            pl.BlockSpec((OUT_TM, k), lambda i: (i, 0)),
            pl.BlockSpec((k, n), const, pipeline_mode=pl.Buffered(1)),
            pl.BlockSpec((OUT_TM, n), lambda i: (i, 0)),
            pl.BlockSpec((1, n), const),
            pl.BlockSpec((1, n), const),
        ],
        out_specs=pl.BlockSpec((OUT_TM, n), lambda i: (i, 0)),
        out_shape=jax.ShapeDtypeStruct((m, n), F32),
        scratch_shapes=[pltpu.VMEM((OUT_TM, n), F32)],
        compiler_params=pltpu.CompilerParams(
            dimension_semantics=("parallel",),
            vmem_limit_bytes=vmem),
        name=name,
    )(y, w, resid, gain, bias)


def _time_permutation():
    r = np.arange(LRU_TM)
    p = np.zeros((LRU_TM, LRU_TM), np.float32)
    p[r, LRU_SUB * (r % SUBLANES) + r // SUBLANES] = 1.0
    return p


def _sublane_scan(a, b):
    row = lax.broadcasted_iota(jnp.int32, a.shape, 0)
    d = 1
    while d < SUBLANES:
        keep = row >= d
        a_sh = jnp.where(keep, pltpu.roll(a, d, 0), 1.0)
        b_sh = jnp.where(keep, pltpu.roll(b, d, 0), 0.0)
        b = a * b_sh + b
        a = a * a_sh
        d *= 2
    return a, b


def _lru_kernel(h_ref, perm_ref, win_ref, cw_ref, cb_ref, wax_ref, bax_ref, lam_ref, y_ref,
                xbuf, ybuf, tailbuf, hcarry):
    t = pl.program_id(1)
    tm = LRU_TM
    hist = (CONV_W - 1) * SUBLANES

    @pl.when(t == 0)
    def _():
        tailbuf[...] = jnp.zeros_like(tailbuf)
        hcarry[...] = jnp.zeros_like(hcarry)

    for sb in range(LRU_SB):
        x = h_ref[sb * tm:(sb + 1) * tm, :].astype(BF16)
        xbuf[sb] = jnp.dot(perm_ref[0], x, preferred_element_type=F32).astype(BF16)

    lam = lam_ref[...]
    softplus_neg_lam = jnp.maximum(-lam, 0.0) + jnp.log1p(jnp.exp(-jnp.abs(lam)))
    log2_a_scale = (-RG_C * LOG2_E) * softplus_neg_lam
    row0 = lax.broadcasted_iota(jnp.int32, (SUBLANES, LRU_BLOCK_W), 0) == 0

    d_rnn = y_ref.shape[1]

    def project(sb, n):
        w = jnp.concatenate(
            [win_ref[:, n * LRU_BLOCK_W:(n + 1) * LRU_BLOCK_W],
             win_ref[:, d_rnn + n * LRU_BLOCK_W:d_rnn + (n + 1) * LRU_BLOCK_W]], axis=1)
        return jnp.dot(xbuf[sb], w, preferred_element_type=F32)

    def recur(sb, n, ug):
        cols = slice(n * LRU_BLOCK_W, (n + 1) * LRU_BLOCK_W)
        u_raw = ug[:, :LRU_BLOCK_W]
        g = ug[:, LRU_BLOCK_W:]
        cur = u_raw[tm - hist:, :]
        prev = tailbuf[:, cols]
        tailbuf[:, cols] = cur
        tiles = []
        for j in range(CONV_W - 1):
            rows = slice(j * SUBLANES, (j + 1) * SUBLANES)
            tiles.append(jnp.where(row0, pltpu.roll(prev[rows], 1, 0),
                                   pltpu.roll(cur[rows], 1, 0)))
        ext = jnp.concatenate(tiles + [u_raw], axis=0)
        u = cb_ref[:, cols]
        for tap in range(CONV_W):
            start = tap * SUBLANES
            u = u + ext[start:start + tm] * cw_ref[tap:tap + 1, cols]
        gates = jnp.dot(u.astype(BF16), wax_ref[n], preferred_element_type=F32)
        gates = jax.nn.sigmoid(gates + bax_ref[n])
        r = gates[:, :LRU_BLOCK_W]
        i = gates[:, LRU_BLOCK_W:]
        a = jnp.exp2(log2_a_scale[:, cols] * r)
        one_m_a2 = 1.0 - a * a
        mult = jnp.where(one_m_a2 > 0.0, one_m_a2 * lax.rsqrt(one_m_a2), 0.0)
        b = mult * (i * u)

        hs, ps = [b[:SUBLANES]], [a[:SUBLANES]]
        for k in range(1, LRU_SUB):
            rows = slice(k * SUBLANES, (k + 1) * SUBLANES)
            hs.append(a[rows] * hs[-1] + b[rows])
            ps.append(a[rows] * ps[-1])
        p_inc, h_inc = _sublane_scan(ps[-1], hs[-1])
        h0 = hcarry[:, cols]
        row = lax.broadcasted_iota(jnp.int32, p_inc.shape, 0)
        p_exc = jnp.where(row == 0, 1.0, pltpu.roll(p_inc, 1, 0))
        h_exc = jnp.where(row == 0, 0.0, pltpu.roll(h_inc, 1, 0))
        carry = p_exc * h0 + h_exc
        hcarry[:, cols] = (p_inc * h0 + h_inc)[SUBLANES - 1:SUBLANES, :]
        h = jnp.concatenate([hk + pk * carry for hk, pk in zip(hs, ps)], axis=0)
        ybuf[sb, :, cols] = (h * (g * jax.nn.sigmoid(g))).astype(BF16)

    units = [(sb, n) for sb in range(LRU_SB) for n in range(LRU_BLOCKS)]
    pending = [project(*u) for u in units[:LRU_AHEAD]]
    for i, unit in enumerate(units):
        ug = pending.pop(0)
        if i + LRU_AHEAD < len(units):
            pending.append(project(*units[i + LRU_AHEAD]))
        recur(*unit, ug)

    for sb in range(LRU_SB):
        y = jnp.dot(perm_ref[1], ybuf[sb], preferred_element_type=F32)
        y_ref[sb * tm:(sb + 1) * tm, :] = y.astype(y_ref.dtype)


def _lru(h, w_in, conv_w, conv_b, wax, bax, lam, batch, seq):
    m, d = h.shape
    d_rnn = lam.shape[1]
    rows = LRU_SB * LRU_TM
    nt = seq // rows
    perm = _time_permutation()
    perms = jnp.asarray(np.stack([perm, perm.T]), BF16)
    const = lambda b, t: (0, 0)
    const3 = lambda b, t: (0, 0, 0)
    hist = (CONV_W - 1) * SUBLANES
    block_f32 = _nbytes((LRU_TM, LRU_BLOCK_W), F32)
    vmem = _vmem_limit(
        pipelined=[_nbytes((rows, d), F32), _nbytes((rows, d_rnn), BF16),
                   _nbytes((2, LRU_TM, LRU_TM), BF16),
                   _nbytes((LRU_BLOCKS, LRU_BLOCK_W, 2 * LRU_BLOCK_W), BF16),
                   _nbytes((LRU_BLOCKS, SUBLANES, 2 * LRU_BLOCK_W), F32),
                   (CONV_W + 3 * SUBLANES) * d_rnn * 4],
        resident=[_nbytes((d, 2 * d_rnn), BF16), _nbytes((rows, d), BF16),
                  _nbytes((rows, d_rnn), BF16), _nbytes((hist + SUBLANES, d_rnn), F32)],
        temporaries=[_nbytes((LRU_TM, d), F32), (LRU_AHEAD + 1) * 2 * block_f32,
                     12 * block_f32])
    return pl.pallas_call(
        _lru_kernel,
        grid=(batch, nt),
        in_specs=[
            pl.BlockSpec((rows, d), lambda b, t: (b * nt + t, 0)),
            pl.BlockSpec((2, LRU_TM, LRU_TM), const3),
            pl.BlockSpec((d, 2 * d_rnn), const, pipeline_mode=pl.Buffered(1)),
            pl.BlockSpec((CONV_W, d_rnn), const),
            pl.BlockSpec((1, d_rnn), const),
            pl.BlockSpec((LRU_BLOCKS, LRU_BLOCK_W, 2 * LRU_BLOCK_W), const3),
            pl.BlockSpec((LRU_BLOCKS, 1, 2 * LRU_BLOCK_W), const3),
            pl.BlockSpec((1, d_rnn), const),
        ],
        out_specs=pl.BlockSpec((rows, d_rnn), lambda b, t: (b * nt + t, 0)),
        out_shape=jax.ShapeDtypeStruct((m, d_rnn), BF16),
        scratch_shapes=[
            pltpu.VMEM((LRU_SB, LRU_TM, d), BF16),
            pltpu.VMEM((LRU_SB, LRU_TM, d_rnn), BF16),
            pltpu.VMEM((hist, d_rnn), F32),
            pltpu.VMEM((1, d_rnn), F32),
        ],
        compiler_params=pltpu.CompilerParams(
            dimension_semantics=("parallel", "arbitrary"),
            vmem_limit_bytes=vmem),
        name="rglru",
    )(h, perms, w_in, conv_w, conv_b, wax, bax, lam)


def kernel(x, attn_w_in, attn_w_out, attn_rel_bias, lru_w_in, lru_conv_w, lru_conv_b,
           lru_wa, lru_ba, lru_wx, lru_bx, lru_lambda, lru_w_out, ln_gain, ln_bias):
    batch, seq, d_model = x.shape
    m = batch * seq
    d_att = ATT_HEADS * HEAD_DIM
    h = x.reshape(m, d_model)

    col_scale = jnp.concatenate([jnp.full((d_att,), Q_SCALE, F32),
                                 jnp.ones((3 * d_att,), F32)])[None, :]
    qkvg = _proj(h, attn_w_in[0].astype(BF16), col_scale)
    og = _attention(qkvg, _attention_bias_tiles(attn_rel_bias[0]), batch, seq)
    h = _out_ln(og, attn_w_out[0].astype(BF16), h, ln_gain[0][None, :], ln_bias[0][None, :],
                "attn_out_ln")

    wax = jnp.concatenate([lru_wa[0], lru_wx[0]], axis=-1).astype(BF16)
    bax = jnp.concatenate([lru_ba[0], lru_bx[0]], axis=-1)[:, None, :]
    y = _lru(h, lru_w_in[0].astype(BF16), lru_conv_w[0], lru_conv_b[0][None, :], wax, bax,
             lru_lambda[0][None, :], batch, seq)
    h = _out_ln(y, lru_w_out[0].astype(BF16), h, ln_gain[1][None, :], ln_bias[1][None, :],
                "lru_out_ln")
    return h.reshape(batch, seq, d_model)
```
